```python
import jax, jax.numpy as jnp
from jax import lax
import numpy as np

D_MODEL = 4096
BATCH = 4
SEQ = 2048
DEPTH = 2

D_FF = 11008
EPS = 1e-6

A_HEADS = 32
A_KV_HEADS = 4
A_HEAD_DIM = 64
WINDOW = 128
A_WIDTH = A_HEADS * A_HEAD_DIM
A_KV_WIDTH = A_KV_HEADS * A_HEAD_DIM

B_GROUPS = 16
B_GROUP_DIM = 128
B_CHUNK = 128
B_WIDTH = B_GROUPS * B_GROUP_DIM

AB_IN = A_WIDTH + 2 * A_KV_WIDTH + 2 * B_WIDTH
AB_OUT = A_WIDTH + B_WIDTH

C_EXPAND = 128
C_HEADS = D_MODEL // C_EXPAND
C_KEY_DIM = C_EXPAND
C_VAL_DIM = D_MODEL // C_HEADS
C_WIDTH = C_HEADS * C_KEY_DIM
C_VWIDTH = C_HEADS * C_VAL_DIM
C_CHUNK = 32
C_IN = 2 * C_WIDTH + 2 * C_VWIDTH

N_AB = (DEPTH + 1) // 2
N_C = DEPTH // 2

kernel_name = "hybrid_swa_gmlp_hgrn2_macaron"


def rms_norm(x, g):
    xf = x.astype(jnp.float32)
    y = xf * lax.rsqrt(jnp.mean(xf * xf, axis=-1, keepdims=True) + EPS)
    return (y * g.astype(jnp.float32)).astype(x.dtype)


def swiglu(h, w1, w3, w2):
    return (jax.nn.silu(h @ w1) * (h @ w3)) @ w2


def sliding_window_attention(q, k, v, q_gain, k_gain, sinks):
    b, s, _, hd = q.shape
    nb = s // WINDOW
    grp = A_HEADS // A_KV_HEADS
    q = rms_norm(q, q_gain)
    k = rms_norm(k, k_gain)
    qb = q.reshape(b, nb, WINDOW, A_KV_HEADS, grp, hd)
    kb = k.reshape(b, nb, WINDOW, A_KV_HEADS, hd)
    vb = v.reshape(b, nb, WINDOW, A_KV_HEADS, hd)
    prev = lambda t: jnp.concatenate([jnp.zeros_like(t[:, :1]), t[:, :-1]], axis=1)
    kk = jnp.concatenate([prev(kb), kb], axis=2)
    vv = jnp.concatenate([prev(vb), vb], axis=2)
    scores = jnp.einsum('bnqhgd,bnkhd->bhgnqk', qb, kk).astype(jnp.float32) * (hd ** -0.5)
    qi = jnp.arange(WINDOW)[:, None] + WINDOW
    kj = jnp.arange(2 * WINDOW)[None, :]
    dist = qi - kj
    band = (dist >= 0) & (dist < WINDOW)
    exists = (jnp.arange(nb)[:, None, None] * WINDOW + kj[None] - WINDOW) >= 0
    mask = band[None] & exists
    scores = jnp.where(mask, scores, -jnp.inf)
    sink = jnp.broadcast_to(sinks.astype(jnp.float32).reshape(A_KV_HEADS, grp, 1, 1, 1),
                            scores.shape[:-1] + (1,))
    probs = jax.nn.softmax(jnp.concatenate([scores, sink], axis=-1), axis=-1)[..., :-1]
    out = jnp.einsum('bhgnqk,bnkhd->bnqhgd', probs.astype(vv.dtype), vv)
    return out.reshape(b, s, A_WIDTH)


def chunked_spatial_gating(u, v, ln_g, ln_b, w_s, b_s):
    b, s, _ = u.shape
    nc = s // B_CHUNK
    vg = v.reshape(b, nc, B_CHUNK, B_GROUPS, B_GROUP_DIM).astype(jnp.float32)
    mu = jnp.mean(vg, axis=-1, keepdims=True)
    var = jnp.mean(jnp.square(vg - mu), axis=-1, keepdims=True)
    vn = ((vg - mu) * lax.rsqrt(var + EPS) * ln_g.astype(jnp.float32).reshape(B_GROUPS, B_GROUP_DIM)
          + ln_b.astype(jnp.float32).reshape(B_GROUPS, B_GROUP_DIM)).astype(u.dtype)
    causal = jnp.tril(jnp.ones((B_CHUNK, B_CHUNK), dtype=bool))
    w = jnp.where(causal[None], w_s, jnp.zeros_like(w_s))
    sg = jnp.einsum('gts,bnsgc->bntgc', w, vn) + b_s.T[None, None, :, :, None]
    ug = u.reshape(b, nc, B_CHUNK, B_GROUPS, B_GROUP_DIM)
    return (ug * sg).reshape(b, s, B_WIDTH)


def hgrn2_recurrence(q, k, v, log_f):
    b, s, h, dk = q.shape
    dv = v.shape[-1]
    nc = s // C_CHUNK
    def to_chunks(t):
        return t.astype(jnp.float32).reshape(b, nc, C_CHUNK, h, t.shape[-1]).transpose(1, 0, 3, 2, 4)
    qc, kc, vc, gc = to_chunks(q), to_chunks(k), to_chunks(v), to_chunks(log_f)
    causal = jnp.tril(jnp.ones((C_CHUNK, C_CHUNK), dtype=bool))[:, :, None]

    def step(state, inp):
        qt, kt, vt, gt = inp
        cg = jnp.cumsum(gt, axis=2)
        o_inter = jnp.einsum('bhtd,bhde->bhte', qt * jnp.exp(cg), state)
        diff = cg[:, :, :, None, :] - cg[:, :, None, :, :]
        decay = jnp.exp(jnp.where(causal, diff, -jnp.inf))
        att = jnp.einsum('bhtd,bhsd,bhtsd->bhts', qt, kt, decay)
        o_intra = jnp.einsum('bhts,bhse->bhte', att, vt)
        last = cg[:, :, -1:, :]
        state = (jnp.exp(last[:, :, 0, :])[..., None] * state
                 + jnp.einsum('bhsd,bhse->bhde', kt * jnp.exp(last - cg), vt))
        return state, o_inter + o_intra

    s0 = jnp.zeros((b, h, dk, dv), jnp.float32)
    _, o = lax.scan(step, s0, (qc, kc, vc, gc))
    return o.transpose(1, 0, 3, 2, 4).reshape(b, s, h, dv).astype(q.dtype)


def ab_mixer(h, w_in, q_gain, k_gain, sinks, v_ln_g, v_ln_b, w_s, b_s, w_out):
    b, s, _ = h.shape
    proj = h @ w_in
    i1 = A_WIDTH
    i2 = i1 + A_KV_WIDTH
    i3 = i2 + A_KV_WIDTH
    i4 = i3 + B_WIDTH
    q, k, v, ub, vb = jnp.split(proj, [i1, i2, i3, i4], axis=-1)
    out_a = sliding_window_attention(q.reshape(b, s, A_HEADS, A_HEAD_DIM),
                                     k.reshape(b, s, A_KV_HEADS, A_HEAD_DIM),
                                     v.reshape(b, s, A_KV_HEADS, A_HEAD_DIM),
                                     q_gain, k_gain, sinks)
    out_b = chunked_spatial_gating(jax.nn.gelu(ub, approximate=False), jax.nn.gelu(vb, approximate=False),
                                   v_ln_g, v_ln_b, w_s, b_s)
    return jnp.concatenate([out_a, out_b], axis=-1) @ w_out


def hgrn2_mixer(h, w_in, lb, o_gain, w_out):
    b, s, _ = h.shape
    proj = h @ w_in
    q, fz, inp, gate = jnp.split(proj, [C_WIDTH, 2 * C_WIDTH, 2 * C_WIDTH + C_VWIDTH], axis=-1)
    zf = fz.astype(jnp.float32)
    lbf = lb.astype(jnp.float32)
    log_f = jnp.logaddexp(jnp.log(lbf), jnp.log1p(-lbf) + jax.nn.log_sigmoid(zf))
    k = (1.0 - lbf) * jax.nn.sigmoid(-zf)
    o = hgrn2_recurrence(q.reshape(b, s, C_HEADS, C_KEY_DIM),
                         k.reshape(b, s, C_HEADS, C_KEY_DIM).astype(q.dtype),
                         inp.reshape(b, s, C_HEADS, C_VAL_DIM),
                         log_f.reshape(b, s, C_HEADS, C_KEY_DIM))
    o = rms_norm(o, o_gain).reshape(b, s, C_VWIDTH) * jax.nn.silu(gate)
    return o @ w_out


def setup_inputs(seed: int = 0) -> dict:
    key = jax.random.key(seed)
    ks = jax.random.split(key, 24)
    f32 = jnp.float32
    def w(k, shape, fan_in):
        return jax.random.normal(k, shape, f32) * (fan_in ** -0.5)
    def gain(k, shape):
        return 1.0 + 0.05 * jax.random.normal(k, shape, f32)
    return {
        "x": jax.random.normal(ks[0], (BATCH, SEQ, D_MODEL), f32),
        "ffn1_norm": gain(ks[1], (DEPTH, D_MODEL)),
        "ffn1_w1": w(ks[2], (DEPTH, D_MODEL, D_FF), D_MODEL),
        "ffn1_w3": w(ks[3], (DEPTH, D_MODEL, D_FF), D_MODEL),
        "ffn1_w2": w(ks[4], (DEPTH, D_FF, D_MODEL), D_FF),
        "mix_norm": gain(ks[5], (DEPTH, D_MODEL)),
        "ffn2_norm": gain(ks[6], (DEPTH, D_MODEL)),
        "ffn2_w1": w(ks[7], (DEPTH, D_MODEL, D_FF), D_MODEL),
        "ffn2_w3": w(ks[8], (DEPTH, D_MODEL, D_FF), D_MODEL),
        "ffn2_w2": w(ks[9], (DEPTH, D_FF, D_MODEL), D_FF),
        "ab_w_in": w(ks[10], (N_AB, D_MODEL, AB_IN), D_MODEL),
        "ab_q_norm": gain(ks[11], (N_AB, A_HEAD_DIM)),
        "ab_k_norm": gain(ks[12], (N_AB, A_HEAD_DIM)),
        "ab_sinks": 0.5 * jax.random.normal(ks[13], (N_AB, A_HEADS), f32),
        "ab_v_ln_g": gain(ks[14], (N_AB, B_WIDTH)),
        "ab_v_ln_b": 0.02 * jax.random.normal(ks[15], (N_AB, B_WIDTH), f32),
        "ab_w_s": w(ks[16], (N_AB, B_GROUPS, B_CHUNK, B_CHUNK), B_CHUNK),
        "ab_b_s": gain(ks[17], (N_AB, B_GROUPS, B_CHUNK)),
        "ab_w_out": w(ks[18], (N_AB, AB_OUT, D_MODEL), AB_OUT),
        "c_w_in": w(ks[19], (N_C, D_MODEL, C_IN), D_MODEL),
        "c_lb_logits": jax.random.normal(ks[20], (DEPTH, C_WIDTH), f32),
        "c_o_norm": gain(ks[21], (N_C, C_VAL_DIM)),
        "c_w_out": w(ks[22], (N_C, C_VWIDTH, D_MODEL), C_VWIDTH),
    }


def reference(x, ffn1_norm, ffn1_w1, ffn1_w3, ffn1_w2, mix_norm, ffn2_norm, ffn2_w1, ffn2_w3, ffn2_w2,
              ab_w_in, ab_q_norm, ab_k_norm, ab_sinks, ab_v_ln_g, ab_v_ln_b, ab_w_s, ab_b_s, ab_w_out,
              c_w_in, c_lb_logits, c_o_norm, c_w_out):
    p = jax.nn.softmax(c_lb_logits.astype(jnp.float32), axis=0)
    lower_bounds = jnp.cumsum(p, axis=0) - p[0:1]
    for l in range(DEPTH):
        x = x + 0.5 * swiglu(rms_norm(x, ffn1_norm[l]), ffn1_w1[l], ffn1_w3[l], ffn1_w2[l])
        h = rms_norm(x, mix_norm[l])
        if l % 2 == 0:
            j = l // 2
            x = x + ab_mixer(h, ab_w_in[j], ab_q_norm[j], ab_k_norm[j], ab_sinks[j],
                             ab_v_ln_g[j], ab_v_ln_b[j], ab_w_s[j], ab_b_s[j], ab_w_out[j])
        else:
            j = l // 2
            x = x + hgrn2_mixer(h, c_w_in[j], lower_bounds[l], c_o_norm[j], c_w_out[j])
        x = x + 0.5 * swiglu(rms_norm(x, ffn2_norm[l]), ffn2_w1[l], ffn2_w3[l], ffn2_w2[l])
    return x
```

```python
import functools

import jax
import jax.numpy as jnp
from jax import lax
from jax.experimental import pallas as pl
from jax.experimental.pallas import tpu as pltpu

F32 = jnp.float32
BF16 = jnp.bfloat16

D_MODEL = 4096
D_FF = 11008
D_FF_PAD = 11264
EPS = 1e-6

A_HEADS = 32
A_KV_HEADS = 4
A_GROUP = A_HEADS // A_KV_HEADS
A_HEAD_DIM = 64
WINDOW = 128
A_WIDTH = A_HEADS * A_HEAD_DIM
A_KV_WIDTH = A_KV_HEADS * A_HEAD_DIM
B_GROUPS = 16
B_GROUP_DIM = 128
B_WIDTH = B_GROUPS * B_GROUP_DIM

C_HEADS = 32
C_DIM = 128
C_WIDTH = C_HEADS * C_DIM
C_CHUNK = 128
C_HEADS_PER_STEP = 4

LANES = 128
SUBLANES = 8
MIB = 1 << 20

NT_DIMS = (((1,), (1,)), ((), ()))
TN_DIMS = (((0,), (0,)), ((), ()))


def _params(semantics, vmem_mib):
    return pltpu.CompilerParams(dimension_semantics=semantics, vmem_limit_bytes=vmem_mib * MIB)


def _dot(a, b):
    return jnp.dot(a, b, preferred_element_type=F32)


def _gelu(y):
    return 0.5 * y * (1.0 + lax.erf(y * (2.0 ** -0.5)))


def _rmsnorm_kernel(x_ref, g_ref, o_ref):
    x = x_ref[...]
    r = lax.rsqrt(jnp.mean(x * x, axis=-1, keepdims=True) + EPS)
    o_ref[...] = (x * r * g_ref[...]).astype(o_ref.dtype)


def _rmsnorm(x, g, rows=256):
    m, d = x.shape
    return pl.pallas_call(
        _rmsnorm_kernel,
        grid=(m // rows,),
        in_specs=[pl.BlockSpec((rows, d), lambda i: (i, 0)),
                  pl.BlockSpec((1, d), lambda i: (0, 0))],
        out_specs=pl.BlockSpec((rows, d), lambda i: (i, 0)),
        out_shape=jax.ShapeDtypeStruct((m, d), BF16),
        compiler_params=_params(("parallel",), 32),
    )(x, g.reshape(1, d))


def _matmul_kernel(a_ref, w_ref, o_ref, *, gelu_blocks):
    y = _dot(a_ref[...], w_ref[...])
    if gelu_blocks is None:
        o_ref[...] = y.astype(o_ref.dtype)
    else:
        j = pl.program_id(1)
        use_gelu = (j >= gelu_blocks[0]) & (j < gelu_blocks[1])

        @pl.when(use_gelu)
        def _():
            o_ref[...] = _gelu(y).astype(o_ref.dtype)

        @pl.when(jnp.logical_not(use_gelu))
        def _():
            o_ref[...] = y.astype(o_ref.dtype)


def _matmul(a, w, out_dtype, gelu_cols=None, tm=1024, tn=512):
    m, k = a.shape
    n = w.shape[1]
    gelu_blocks = None if gelu_cols is None else (gelu_cols[0] // tn, gelu_cols[1] // tn)
    return pl.pallas_call(
        functools.partial(_matmul_kernel, gelu_blocks=gelu_blocks),
        grid=(m // tm, n // tn),
        in_specs=[pl.BlockSpec((tm, k), lambda i, j: (i, 0)),
                  pl.BlockSpec((k, tn), lambda i, j: (0, j))],
        out_specs=pl.BlockSpec((tm, tn), lambda i, j: (i, j)),
        out_shape=jax.ShapeDtypeStruct((m, n), out_dtype),
        compiler_params=_params(("parallel", "arbitrary"), 48),
    )(a, w)


def _swiglu_up_kernel(a_ref, w1_ref, w3_ref, o_ref):
    a = a_ref[...]
    y1 = _dot(a, w1_ref[...])
    y3 = _dot(a, w3_ref[...])
    o_ref[...] = (0.5 * (y1 * jax.nn.sigmoid(y1)) * y3).astype(o_ref.dtype)


def _swiglu_up(a, w1, w3, tm=1024, tf=512):
    m, k = a.shape
    f = w1.shape[1]
    return pl.pallas_call(
        _swiglu_up_kernel,
        grid=(m // tm, f // tf),
        in_specs=[pl.BlockSpec((tm, k), lambda i, j: (i, 0)),
                  pl.BlockSpec((k, tf), lambda i, j: (0, j)),
                  pl.BlockSpec((k, tf), lambda i, j: (0, j))],
        out_specs=pl.BlockSpec((tm, tf), lambda i, j: (i, j)),
        out_shape=jax.ShapeDtypeStruct((m, f), BF16),
        compiler_params=_params(("parallel", "arbitrary"), 48),
    )(a, w1, w3)


def _matmul_res_kernel(a_ref, w_ref, r_ref, o_ref):
    kk = pl.program_id(2)
    y = _dot(a_ref[...], w_ref[...])

    @pl.when(kk == 0)
    def _():
        o_ref[...] = r_ref[...] + y

    @pl.when(kk > 0)
    def _():
        o_ref[...] += y


def _matmul_res(a, w, res, tm=1024, tn=1024, tk=None):
    m, k = a.shape
    n = w.shape[1]
    tk = k if tk is None else tk
    return pl.pallas_call(
        _matmul_res_kernel,
        grid=(m // tm, n // tn, k // tk),
        in_specs=[pl.BlockSpec((tm, tk), lambda i, j, kk: (i, kk)),
                  pl.BlockSpec((tk, tn), lambda i, j, kk: (kk, j)),
                  pl.BlockSpec((tm, tn), lambda i, j, kk: (i, j))],
        out_specs=pl.BlockSpec((tm, tn), lambda i, j, kk: (i, j)),
        out_shape=jax.ShapeDtypeStruct((m, n), F32),
        compiler_params=_params(("parallel", "parallel", "arbitrary"), 56),
    )(a, w, res)


def _ab_mixer_kernel(sinks_ref, q_ref, u_ref, vg_ref, kvp_ref, kvc_ref, qg_ref, kg_ref,
                     lng_ref, lnb_ref, ws_ref, bs_ref, o_ref):
    n = pl.program_id(1)
    w = WINDOW
    hd = A_HEAD_DIM

    qi = lax.broadcasted_iota(jnp.int32, (w, 2 * w), 0)
    kj = lax.broadcasted_iota(jnp.int32, (w, 2 * w), 1)
    dist = qi + w - kj
    mask = (dist >= 0) & (dist < w) & ((kj >= w) | (n > 0))
    mask = jnp.concatenate([mask] * A_GROUP, axis=0)

    kvp = kvp_ref[...]
    kvc = kvc_ref[...]
    qgain = qg_ref[...] * (hd ** -0.5)
    kgain = kg_ref[...]
    for h in range(A_KV_HEADS):
        k = jnp.concatenate([kvp[:, h * hd:(h + 1) * hd], kvc[:, h * hd:(h + 1) * hd]], axis=0)
        v = jnp.concatenate([kvp[:, A_KV_WIDTH + h * hd:A_KV_WIDTH + (h + 1) * hd],
                             kvc[:, A_KV_WIDTH + h * hd:A_KV_WIDTH + (h + 1) * hd]], axis=0)
        k = k * lax.rsqrt(jnp.mean(k * k, axis=-1, keepdims=True) + EPS) * kgain
        q = jnp.concatenate([q_ref[:, (h * A_GROUP + g) * hd:(h * A_GROUP + g + 1) * hd]
                             for g in range(A_GROUP)], axis=0)
        q = q * lax.rsqrt(jnp.mean(q * q, axis=-1, keepdims=True) + EPS) * qgain
        s = lax.dot_general(q.astype(BF16), k.astype(BF16), NT_DIMS, preferred_element_type=F32)
        s = jnp.where(mask, s, -jnp.inf)
        sink = jnp.concatenate([jnp.full((w, 1), sinks_ref[h * A_GROUP + g], F32)
                                for g in range(A_GROUP)], axis=0)
        mx = jnp.maximum(jnp.max(s, axis=-1, keepdims=True), sink)
        e = jnp.exp(s - mx)
        denom = jnp.sum(e, axis=-1, keepdims=True) + jnp.exp(sink - mx)
        p = e / denom
        o = _dot(p.astype(BF16), v.astype(BF16))
        for g in range(A_GROUP):
            c0 = (h * A_GROUP + g) * hd
            o_ref[:, c0:c0 + hd] = o[g * w:(g + 1) * w].astype(o_ref.dtype)

    row = lax.broadcasted_iota(jnp.int32, (w, w), 0)
    col = lax.broadcasted_iota(jnp.int32, (w, w), 1)
    causal = row >= col
    for g in range(B_GROUPS):
        sl = slice(g * B_GROUP_DIM, (g + 1) * B_GROUP_DIM)
        x = vg_ref[:, sl]
        mu = jnp.mean(x, axis=-1, keepdims=True)
        d = x - mu
        var = jnp.mean(d * d, axis=-1, keepdims=True)
        vn = d * lax.rsqrt(var + EPS) * lng_ref[:, sl] + lnb_ref[:, sl]
        wg = jnp.where(causal, ws_ref[g], 0.0).astype(BF16)
        sg = _dot(wg, vn.astype(BF16)) + bs_ref[:, g:g + 1]
        o_ref[:, A_WIDTH + g * B_GROUP_DIM:A_WIDTH + (g + 1) * B_GROUP_DIM] = (
            u_ref[:, sl] * sg).astype(o_ref.dtype)


def _ab_mixer(proj, batch, seq, q_gain, k_gain, sinks, ln_g, ln_b, w_s, b_s):
    nb = seq // WINDOW
    kv_blk = (A_WIDTH + 2 * B_WIDTH) // (2 * A_KV_WIDTH)
    row_blk = lambda b, n: b * nb + n
    return pl.pallas_call(
        _ab_mixer_kernel,
        grid=(batch, nb),
        in_specs=[
            pl.BlockSpec(memory_space=pltpu.SMEM),
            pl.BlockSpec((WINDOW, A_WIDTH), lambda b, n: (row_blk(b, n), 0)),
            pl.BlockSpec((WINDOW, B_WIDTH), lambda b, n: (row_blk(b, n), 1)),
            pl.BlockSpec((WINDOW, B_WIDTH), lambda b, n: (row_blk(b, n), 2)),
            pl.BlockSpec((WINDOW, 2 * A_KV_WIDTH), lambda b, n: (b * nb + jnp.maximum(n - 1, 0), kv_blk)),
            pl.BlockSpec((WINDOW, 2 * A_KV_WIDTH), lambda b, n: (row_blk(b, n), kv_blk)),
            pl.BlockSpec((1, A_HEAD_DIM), lambda b, n: (0, 0)),
            pl.BlockSpec((1, A_HEAD_DIM), lambda b, n: (0, 0)),
            pl.BlockSpec((1, B_WIDTH), lambda b, n: (0, 0)),
            pl.BlockSpec((1, B_WIDTH), lambda b, n: (0, 0)),
            pl.BlockSpec((B_GROUPS, WINDOW, WINDOW), lambda b, n: (0, 0, 0)),
            pl.BlockSpec((WINDOW, B_GROUPS), lambda b, n: (0, 0)),
        ],
        out_specs=pl.BlockSpec((WINDOW, A_WIDTH + B_WIDTH), lambda b, n: (row_blk(b, n), 0)),
        out_shape=jax.ShapeDtypeStruct((batch * seq, A_WIDTH + B_WIDTH), BF16),
        compiler_params=_params(("parallel", "arbitrary"), 32),
    )(sinks, proj, proj, proj, proj, proj,
      q_gain.reshape(1, A_HEAD_DIM), k_gain.reshape(1, A_HEAD_DIM),
      ln_g.reshape(1, B_WIDTH), ln_b.reshape(1, B_WIDTH), w_s, b_s.T)


def _hgrn2_kernel(lbl_ref, og_ref, q_ref, v_ref, gate_ref, fz_ref, o_ref, st_ref, *, layer):
    c = C_CHUNK

    @pl.when(pl.program_id(2) == 0)
    def _():
        st_ref[...] = jnp.zeros_like(st_ref)

    logits = lbl_ref[...]
    lmax = jnp.max(logits, axis=0, keepdims=True)
    ex = jnp.exp(logits - lmax)
    prob = ex / jnp.sum(ex, axis=0, keepdims=True)
    lb = jnp.sum(prob[:layer + 1], axis=0, keepdims=True) - prob[0:1]
    log_lb = jnp.log(lb)
    log_1m_lb = jnp.log1p(-lb)
    one_m_lb = 1.0 - lb

    row = lax.broadcasted_iota(jnp.int32, (c, c), 0)
    col = lax.broadcasted_iota(jnp.int32, (c, c), 1)
    row1 = lax.broadcasted_iota(jnp.int32, (c, 1), 0)
    cumsum_mat = (row >= col).astype(BF16)
    block_start = row - (row & (SUBLANES - 1))
    og = og_ref[...]

    for h in range(C_HEADS_PER_STEP):
        sl = slice(h * C_DIM, (h + 1) * C_DIM)
        z = fz_ref[:, sl]
        log_sig = jnp.minimum(z, 0.0) - jnp.log1p(jnp.exp(-jnp.abs(z)))
        a = log_lb[:, sl]
        b = log_1m_lb[:, sl] + log_sig
        log_f = jnp.maximum(a, b) + jnp.log1p(jnp.exp(-jnp.abs(a - b)))
        k = one_m_lb[:, sl] * jax.nn.sigmoid(-z)

        p1 = log_f.astype(BF16)
        r1 = log_f - p1.astype(F32)
        p2 = r1.astype(BF16)
        p3 = (r1 - p2.astype(F32)).astype(BF16)
        cg = _dot(cumsum_mat, p1) + _dot(cumsum_mat, p2) + _dot(cumsum_mat, p3)

        q = q_ref[:, sl].astype(F32)
        v = v_ref[:, sl]
        st = st_ref[h]
        o = lax.dot_general((q * jnp.exp(cg)).astype(BF16), st.astype(BF16), NT_DIMS,
                            preferred_element_type=F32)

        att = jnp.zeros((c, c), F32)
        n = SUBLANES
        while n < c:
            ref = jnp.concatenate([jnp.broadcast_to(cg[lo + n - 1:lo + n, :], (2 * n, C_DIM))
                                   for lo in range(0, c, 2 * n)], axis=0)
            e = jnp.exp(-jnp.abs(cg - ref))
            upper = (row1 & (2 * n - 1)) >= n
            qm = jnp.where(upper, q * e, 0.0).astype(BF16)
            km = jnp.where(upper, 0.0, k * e).astype(BF16)
            part = lax.dot_general(qm, km, NT_DIMS, preferred_element_type=F32)
            shift = (2 * n).bit_length() - 1
            att = att + jnp.where((row >> shift) == (col >> shift), part, 0.0)
            n *= 2

        cg3 = cg.reshape(c // SUBLANES, SUBLANES, C_DIM)
        k3 = k.reshape(c // SUBLANES, SUBLANES, C_DIM)
        for j in range(SUBLANES):
            cgj = jnp.broadcast_to(cg3[:, j:j + 1, :], cg3.shape).reshape(c, C_DIM)
            kj = jnp.broadcast_to(k3[:, j:j + 1, :], k3.shape).reshape(c, C_DIM)
            prod = q * kj * jnp.exp(jnp.minimum(cg - cgj, 0.0))
            rs = jnp.sum(prod, axis=-1, keepdims=True)
            keep = (col == block_start + j) & ((row & (SUBLANES - 1)) >= j)
            att = att + jnp.where(keep, rs, 0.0)

        o = o + _dot(att.astype(BF16), v)

        last = cg[c - 1:c, :]
        kd = (k * jnp.exp(last - cg)).astype(BF16)
        st_ref[h] = st * jnp.exp(last) + lax.dot_general(v, kd, TN_DIMS, preferred_element_type=F32)

        y = o * lax.rsqrt(jnp.mean(o * o, axis=-1, keepdims=True) + EPS) * og
        gt = gate_ref[:, sl].astype(F32)
        o_ref[:, sl] = (y * (gt * jax.nn.sigmoid(gt))).astype(o_ref.dtype)


def _hgrn2(qvg, fz, lb_logits, o_gain, batch, seq, layer):
    nc = seq // C_CHUNK
    hw = C_HEADS_PER_STEP * C_DIM
    nh = C_WIDTH // hw
    row_blk = lambda b, g, c: b * nc + c
    return pl.pallas_call(
        functools.partial(_hgrn2_kernel, layer=layer),
        grid=(batch, nh, nc),
        in_specs=[
            pl.BlockSpec((lb_logits.shape[0], hw), lambda b, g, c: (0, g)),
            pl.BlockSpec((1, C_DIM), lambda b, g, c: (0, 0)),
            pl.BlockSpec((C_CHUNK, hw), lambda b, g, c: (row_blk(b, g, c), g)),
            pl.BlockSpec((C_CHUNK, hw), lambda b, g, c: (row_blk(b, g, c), nh + g)),
            pl.BlockSpec((C_CHUNK, hw), lambda b, g, c: (row_blk(b, g, c), 2 * nh + g)),
            pl.BlockSpec((C_CHUNK, hw), lambda b, g, c: (row_blk(b, g, c), g)),
        ],
        out_specs=pl.BlockSpec((C_CHUNK, hw), lambda b, g, c: (row_blk(b, g, c), g)),
        out_shape=jax.ShapeDtypeStruct((batch * seq, C_WIDTH), BF16),
        scratch_shapes=[pltpu.VMEM((C_HEADS_PER_STEP, C_DIM, C_DIM), F32)],
        compiler_params=_params(("parallel", "parallel", "arbitrary"), 32),
    )(lb_logits, o_gain.reshape(1, C_DIM), qvg, qvg, qvg, fz)


def _ffn(x, norm_g, w1, w3, w2):
    pad = D_FF_PAD - D_FF
    w1 = jnp.pad(w1.astype(BF16), ((0, 0), (0, pad)))
    w3 = jnp.pad(w3.astype(BF16), ((0, 0), (0, pad)))
    w2 = jnp.pad(w2.astype(BF16), ((0, pad), (0, 0)))
    h = _swiglu_up(_rmsnorm(x, norm_g), w1, w3)
    return _matmul_res(h, w2, x, tk=D_FF_PAD // 4)


def kernel(x, ffn1_norm, ffn1_w1, ffn1_w3, ffn1_w2, mix_norm, ffn2_norm, ffn2_w1, ffn2_w3, ffn2_w2, ab_w_in, ab_q_norm, ab_k_norm, ab_sinks, ab_v_ln_g, ab_v_ln_b, ab_w_s, ab_b_s, ab_w_out, c_w_in, c_lb_logits, c_o_norm, c_w_out):
    batch, seq, d = x.shape
    depth = ffn1_norm.shape[0]
    x = x.reshape(batch * seq, d)
    for l in range(depth):
        x = _ffn(x, ffn1_norm[l], ffn1_w1[l], ffn1_w3[l], ffn1_w2[l])
        hn = _rmsnorm(x, mix_norm[l])
        j = l // 2
        if l % 2 == 0:
            w_in = ab_w_in[j]
            kv0 = A_WIDTH
            u0 = A_WIDTH + 2 * A_KV_WIDTH
            w_in = jnp.concatenate([w_in[:, :kv0], w_in[:, u0:], w_in[:, kv0:u0]], axis=1).astype(BF16)
            proj = _matmul(hn, w_in, F32, gelu_cols=(A_WIDTH, A_WIDTH + 2 * B_WIDTH))
            mixed = _ab_mixer(proj, batch, seq, ab_q_norm[j], ab_k_norm[j], ab_sinks[j],
                              ab_v_ln_g[j], ab_v_ln_b[j], ab_w_s[j], ab_b_s[j])
            x = _matmul_res(mixed, ab_w_out[j].astype(BF16), x)
        else:
            w_in = c_w_in[j]
            w_qvg = jnp.concatenate([w_in[:, :C_WIDTH], w_in[:, 2 * C_WIDTH:]], axis=1).astype(BF16)
            w_f = w_in[:, C_WIDTH:2 * C_WIDTH].astype(BF16)
            qvg = _matmul(hn, w_qvg, BF16)
            fz = _matmul(hn, w_f, F32)
            mixed = _hgrn2(qvg, fz, c_lb_logits, c_o_norm[j], batch, seq, l)
            x = _matmul_res(mixed, c_w_out[j].astype(BF16), x)
        x = _ffn(x, ffn2_norm[l], ffn2_w1[l], ffn2_w3[l], ffn2_w2[l])
    return x.reshape(batch, seq, d)
```

```python
import functools

import jax
import jax.numpy as jnp
from jax import lax
from jax.experimental import pallas as pl
from jax.experimental.pallas import tpu as pltpu

F32 = jnp.float32
BF16 = jnp.bfloat16

D_MODEL = 4096
D_FF = 11008
FF_TILE = 256
D_FF_PAD = 11264
EPS = 1e-6

A_HEADS = 32
A_KV_HEADS = 4
A_GROUP = A_HEADS // A_KV_HEADS
A_HEAD_DIM = 64
WINDOW = 128
A_WIDTH = A_HEADS * A_HEAD_DIM
A_KV_WIDTH = A_KV_HEADS * A_HEAD_DIM
B_GROUPS = 16
B_GROUP_DIM = 128
B_WIDTH = B_GROUPS * B_GROUP_DIM
AB_IN = A_WIDTH + 2 * A_KV_WIDTH + 2 * B_WIDTH

C_HEADS = 32
C_DIM = 128
C_WIDTH = C_HEADS * C_DIM
C_CHUNK = 128
C_HEADS_PER_STEP = 4

LANES = 128
SUBLANES = 8
MIB = 1 << 20

NT_DIMS = (((1,), (1,)), ((), ()))
TN_DIMS = (((0,), (0,)), ((), ()))


def _params(semantics, vmem_mib):
    return pltpu.CompilerParams(dimension_semantics=semantics, vmem_limit_bytes=vmem_mib * MIB)


def _dot(a, b):
    return jnp.dot(a, b, preferred_element_type=F32)


def _gelu(y):
    return 0.5 * y * (1.0 + lax.erf(y * (2.0 ** -0.5)))


def _rmsnorm_kernel(x_ref, g_ref, o_ref):
    x = x_ref[...]
    r = lax.rsqrt(jnp.mean(x * x, axis=-1, keepdims=True) + EPS)
    o_ref[...] = (x * r * g_ref[...]).astype(o_ref.dtype)


def _rmsnorm(x, g, rows=256):
    m, d = x.shape
    return pl.pallas_call(
        _rmsnorm_kernel,
        grid=(m // rows,),
        in_specs=[pl.BlockSpec((rows, d), lambda i: (i, 0)),
                  pl.BlockSpec((1, d), lambda i: (0, 0))],
        out_specs=pl.BlockSpec((rows, d), lambda i: (i, 0)),
        out_shape=jax.ShapeDtypeStruct((m, d), BF16),
        compiler_params=_params(("parallel",), 32),
        name="rmsnorm",
    )(x, g.reshape(1, d))


def _wres_matmul_kernel(*refs, gelu_from, has_res):
    if has_res:
        a_ref, w_ref, r_ref, o_ref, wb_ref = refs
    else:
        a_ref, w_ref, o_ref, wb_ref = refs
    j = pl.program_id(0)

    @pl.when(pl.program_id(1) == 0)
    def _():
        wb_ref[...] = w_ref[...].astype(BF16)

    y = _dot(a_ref[...], wb_ref[...])
    if has_res:
        o_ref[...] = (r_ref[...] + y).astype(o_ref.dtype)
    elif gelu_from is None:
        o_ref[...] = y.astype(o_ref.dtype)
    else:
        @pl.when(j >= gelu_from)
        def _():
            o_ref[...] = _gelu(y).astype(o_ref.dtype)

        @pl.when(j < gelu_from)
        def _():
            o_ref[...] = y.astype(o_ref.dtype)


def _wres_matmul(name, a, w, layer, n_tiles, out_dtype, w_tile=lambda j: j, o_tile=lambda j: j,
                 gelu_from=None, res=None, tm=1024, tn=512):
    m, k = a.shape
    in_specs = [pl.BlockSpec((tm, k), lambda j, i: (i, 0)),
                pl.BlockSpec((None, k, tn), lambda j, i: (layer, 0, w_tile(j)))]
    args = [a, w]
    if res is not None:
        in_specs.append(pl.BlockSpec((tm, tn), lambda j, i: (i, o_tile(j))))
        args.append(res)
    return pl.pallas_call(
        functools.partial(_wres_matmul_kernel, gelu_from=gelu_from, has_res=res is not None),
        grid=(n_tiles, m // tm),
        in_specs=in_specs,
        out_specs=pl.BlockSpec((tm, tn), lambda j, i: (i, o_tile(j))),
        out_shape=jax.ShapeDtypeStruct((m, n_tiles * tn), out_dtype),
        scratch_shapes=[pltpu.VMEM((k, tn), BF16)],
        compiler_params=_params(("parallel", "arbitrary"), 56),
        name=name,
    )(*args)


def _ffn_up_kernel(a_ref, w1_ref, w3_ref, w2_ref, h_ref, w2b_ref, w1b_ref, w3b_ref, *, n_tiles):
    j = pl.program_id(0)
    first_row_block = pl.program_id(1) == 0
    real = j < n_tiles

    @pl.when(first_row_block & real)
    def _():
        w1b_ref[...] = w1_ref[...].astype(BF16)
        w3b_ref[...] = w3_ref[...].astype(BF16)
        w2b_ref[...] = w2_ref[...].astype(BF16)

    @pl.when(first_row_block & jnp.logical_not(real))
    def _():
        w2b_ref[...] = jnp.zeros_like(w2b_ref)

    @pl.when(real)
    def _():
        a = a_ref[...]
        y1 = _dot(a, w1b_ref[...])
        y3 = _dot(a, w3b_ref[...])
        h_ref[...] = (0.5 * (y1 * jax.nn.sigmoid(y1)) * y3).astype(h_ref.dtype)

    @pl.when(jnp.logical_not(real))
    def _():
        h_ref[...] = jnp.zeros_like(h_ref)


def _ffn_up(a, w1, w3, w2, layer, tm=1024):
    m, k = a.shape
    tf = FF_TILE
    n_tiles = D_FF // tf
    w_tile = lambda j: jnp.minimum(j, n_tiles - 1)
    return pl.pallas_call(
        functools.partial(_ffn_up_kernel, n_tiles=n_tiles),
        grid=(D_FF_PAD // tf, m // tm),
        in_specs=[pl.BlockSpec((tm, k), lambda j, i: (i, 0)),
                  pl.BlockSpec((None, k, tf), lambda j, i: (layer, 0, w_tile(j))),
                  pl.BlockSpec((None, k, tf), lambda j, i: (layer, 0, w_tile(j))),
                  pl.BlockSpec((None, tf, k), lambda j, i: (layer, w_tile(j), 0))],
        out_specs=[pl.BlockSpec((tm, tf), lambda j, i: (i, j)),
                   pl.BlockSpec((tf, k), lambda j, i: (j, 0))],
        out_shape=[jax.ShapeDtypeStruct((m, D_FF_PAD), BF16),
                   jax.ShapeDtypeStruct((D_FF_PAD, k), BF16)],
        scratch_shapes=[pltpu.VMEM((k, tf), BF16), pltpu.VMEM((k, tf), BF16)],
        compiler_params=_params(("arbitrary", "arbitrary"), 58),
        name="ffn_up",
    )(a, w1, w3, w2)


def _matmul_res_kernel(a_ref, w_ref, r_ref, o_ref):
    kk = pl.program_id(2)
    y = _dot(a_ref[...], w_ref[...])

    @pl.when(kk == 0)
    def _():
        o_ref[...] = r_ref[...] + y

    @pl.when(kk > 0)
    def _():
        o_ref[...] += y


def _matmul_res(a, w, res, tm=1024, tn=1024, tk=None):
    m, k = a.shape
    n = w.shape[1]
    tk = k if tk is None else tk
    return pl.pallas_call(
        _matmul_res_kernel,
        grid=(m // tm, n // tn, k // tk),
        in_specs=[pl.BlockSpec((tm, tk), lambda i, j, kk: (i, kk)),
                  pl.BlockSpec((tk, tn), lambda i, j, kk: (kk, j)),
                  pl.BlockSpec((tm, tn), lambda i, j, kk: (i, j))],
        out_specs=pl.BlockSpec((tm, tn), lambda i, j, kk: (i, j)),
        out_shape=jax.ShapeDtypeStruct((m, n), F32),
        compiler_params=_params(("parallel", "parallel", "arbitrary"), 56),
        name="matmul_res",
    )(a, w, res)


def _ab_mixer_kernel(sinks_ref, q_ref, u_ref, vg_ref, kvp_ref, kvc_ref, qg_ref, kg_ref,
                     lng_ref, lnb_ref, ws_ref, bs_ref, o_ref):
    n = pl.program_id(1)
    w = WINDOW
    hd = A_HEAD_DIM

    qi = lax.broadcasted_iota(jnp.int32, (w, 2 * w), 0)
    kj = lax.broadcasted_iota(jnp.int32, (w, 2 * w), 1)
    dist = qi + w - kj
    mask = (dist >= 0) & (dist < w) & ((kj >= w) | (n > 0))
    mask = jnp.concatenate([mask] * A_GROUP, axis=0)

    kvp = kvp_ref[...]
    kvc = kvc_ref[...]
    qgain = qg_ref[...] * (hd ** -0.5)
    kgain = kg_ref[...]
    for h in range(A_KV_HEADS):
        k = jnp.concatenate([kvp[:, h * hd:(h + 1) * hd], kvc[:, h * hd:(h + 1) * hd]], axis=0)
        v = jnp.concatenate([kvp[:, A_KV_WIDTH + h * hd:A_KV_WIDTH + (h + 1) * hd],
                             kvc[:, A_KV_WIDTH + h * hd:A_KV_WIDTH + (h + 1) * hd]], axis=0)
        k = k * lax.rsqrt(jnp.mean(k * k, axis=-1, keepdims=True) + EPS) * kgain
        q = jnp.concatenate([q_ref[:, (h * A_GROUP + g) * hd:(h * A_GROUP + g + 1) * hd]
                             for g in range(A_GROUP)], axis=0)
        q = q * lax.rsqrt(jnp.mean(q * q, axis=-1, keepdims=True) + EPS) * qgain
        s = lax.dot_general(q.astype(BF16), k.astype(BF16), NT_DIMS, preferred_element_type=F32)
        s = jnp.where(mask, s, -jnp.inf)
        sink = jnp.concatenate([jnp.full((w, 1), sinks_ref[h * A_GROUP + g], F32)
                                for g in range(A_GROUP)], axis=0)
        mx = jnp.maximum(jnp.max(s, axis=-1, keepdims=True), sink)
        e = jnp.exp(s - mx)
        denom = jnp.sum(e, axis=-1, keepdims=True) + jnp.exp(sink - mx)
        p = e / denom
        o = _dot(p.astype(BF16), v.astype(BF16))
        for g in range(A_GROUP):
            c0 = (h * A_GROUP + g) * hd
            o_ref[:, c0:c0 + hd] = o[g * w:(g + 1) * w].astype(o_ref.dtype)

    row = lax.broadcasted_iota(jnp.int32, (w, w), 0)
    col = lax.broadcasted_iota(jnp.int32, (w, w), 1)
    causal = row >= col
    for g in range(B_GROUPS):
        sl = slice(g * B_GROUP_DIM, (g + 1) * B_GROUP_DIM)
        x = vg_ref[:, sl]
        mu = jnp.mean(x, axis=-1, keepdims=True)
        d = x - mu
        var = jnp.mean(d * d, axis=-1, keepdims=True)
        vn = d * lax.rsqrt(var + EPS) * lng_ref[:, sl] + lnb_ref[:, sl]
        wg = jnp.where(causal, ws_ref[g], 0.0).astype(BF16)
        sg = _dot(wg, vn.astype(BF16)) + bs_ref[:, g:g + 1]
        o_ref[:, A_WIDTH + g * B_GROUP_DIM:A_WIDTH + (g + 1) * B_GROUP_DIM] = (
            u_ref[:, sl] * sg).astype(o_ref.dtype)


def _ab_mixer(proj, batch, seq, q_gain, k_gain, sinks, ln_g, ln_b, w_s, b_s):
    nb = seq // WINDOW
    kv_blk = (A_WIDTH + 2 * B_WIDTH) // (2 * A_KV_WIDTH)
    row_blk = lambda b, n: b * nb + n
    return pl.pallas_call(
        _ab_mixer_kernel,
        grid=(batch, nb),
        in_specs=[
            pl.BlockSpec(memory_space=pltpu.SMEM),
            pl.BlockSpec((WINDOW, A_WIDTH), lambda b, n: (row_blk(b, n), 0)),
            pl.BlockSpec((WINDOW, B_WIDTH), lambda b, n: (row_blk(b, n), 1)),
            pl.BlockSpec((WINDOW, B_WIDTH), lambda b, n: (row_blk(b, n), 2)),
            pl.BlockSpec((WINDOW, 2 * A_KV_WIDTH), lambda b, n: (b * nb + jnp.maximum(n - 1, 0), kv_blk)),
            pl.BlockSpec((WINDOW, 2 * A_KV_WIDTH), lambda b, n: (row_blk(b, n), kv_blk)),
            pl.BlockSpec((1, A_HEAD_DIM), lambda b, n: (0, 0)),
            pl.BlockSpec((1, A_HEAD_DIM), lambda b, n: (0, 0)),
            pl.BlockSpec((1, B_WIDTH), lambda b, n: (0, 0)),
            pl.BlockSpec((1, B_WIDTH), lambda b, n: (0, 0)),
            pl.BlockSpec((B_GROUPS, WINDOW, WINDOW), lambda b, n: (0, 0, 0)),
            pl.BlockSpec((WINDOW, B_GROUPS), lambda b, n: (0, 0)),
        ],
        out_specs=pl.BlockSpec((WINDOW, A_WIDTH + B_WIDTH), lambda b, n: (row_blk(b, n), 0)),
        out_shape=jax.ShapeDtypeStruct((batch * seq, A_WIDTH + B_WIDTH), BF16),
        compiler_params=_params(("parallel", "arbitrary"), 32),
        name="ab_mixer",
    )(sinks, proj, proj, proj, proj, proj,
      q_gain.reshape(1, A_HEAD_DIM), k_gain.reshape(1, A_HEAD_DIM),
      ln_g.reshape(1, B_WIDTH), ln_b.reshape(1, B_WIDTH), w_s, b_s.T)


def _hgrn2_kernel(lbl_ref, og_ref, q_ref, v_ref, gate_ref, fz_ref, o_ref, st_ref, *, layer):
    c = C_CHUNK

    @pl.when(pl.program_id(2) == 0)
    def _():
        st_ref[...] = jnp.zeros_like(st_ref)

    logits = lbl_ref[...]
    lmax = jnp.max(logits, axis=0, keepdims=True)
    ex = jnp.exp(logits - lmax)
    prob = ex / jnp.sum(ex, axis=0, keepdims=True)
    lb = jnp.sum(prob[:layer + 1], axis=0, keepdims=True) - prob[0:1]
    log_lb = jnp.log(lb)
    log_1m_lb = jnp.log1p(-lb)
    one_m_lb = 1.0 - lb

    row = lax.broadcasted_iota(jnp.int32, (c, c), 0)
    col = lax.broadcasted_iota(jnp.int32, (c, c), 1)
    row1 = lax.broadcasted_iota(jnp.int32, (c, 1), 0)
    cumsum_mat = (row >= col).astype(BF16)
    block_start = row - (row & (SUBLANES - 1))
    og = og_ref[...]

    for h in range(C_HEADS_PER_STEP):
        sl = slice(h * C_DIM, (h + 1) * C_DIM)
        z = fz_ref[:, sl]
        log_sig = jnp.minimum(z, 0.0) - jnp.log1p(jnp.exp(-jnp.abs(z)))
        a = log_lb[:, sl]
        b = log_1m_lb[:, sl] + log_sig
        log_f = jnp.maximum(a, b) + jnp.log1p(jnp.exp(-jnp.abs(a - b)))
        k = one_m_lb[:, sl] * jax.nn.sigmoid(-z)

        p1 = log_f.astype(BF16)
        r1 = log_f - p1.astype(F32)
        p2 = r1.astype(BF16)
        p3 = (r1 - p2.astype(F32)).astype(BF16)
        cg = _dot(cumsum_mat, p1) + _dot(cumsum_mat, p2) + _dot(cumsum_mat, p3)

        q = q_ref[:, sl].astype(F32)
        v = v_ref[:, sl]
        st = st_ref[h]
        o = lax.dot_general((q * jnp.exp(cg)).astype(BF16), st.astype(BF16), NT_DIMS,
                            preferred_element_type=F32)

        att = jnp.zeros((c, c), F32)
        n = SUBLANES
        while n < c:
            ref = jnp.concatenate([jnp.broadcast_to(cg[lo + n - 1:lo + n, :], (2 * n, C_DIM))
                                   for lo in range(0, c, 2 * n)], axis=0)
            e = jnp.exp(-jnp.abs(cg - ref))
            upper = (row1 & (2 * n - 1)) >= n
            qm = jnp.where(upper, q * e, 0.0).astype(BF16)
            km = jnp.where(upper, 0.0, k * e).astype(BF16)
            part = lax.dot_general(qm, km, NT_DIMS, preferred_element_type=F32)
            shift = (2 * n).bit_length() - 1
            att = att + jnp.where((row >> shift) == (col >> shift), part, 0.0)
            n *= 2

        cg3 = cg.reshape(c // SUBLANES, SUBLANES, C_DIM)
        k3 = k.reshape(c // SUBLANES, SUBLANES, C_DIM)
        for j in range(SUBLANES):
            cgj = jnp.broadcast_to(cg3[:, j:j + 1, :], cg3.shape).reshape(c, C_DIM)
            kj = jnp.broadcast_to(k3[:, j:j + 1, :], k3.shape).reshape(c, C_DIM)
            prod = q * kj * jnp.exp(jnp.minimum(cg - cgj, 0.0))
            rs = jnp.sum(prod, axis=-1, keepdims=True)
            keep = (col == block_start + j) & ((row & (SUBLANES - 1)) >= j)
            att = att + jnp.where(keep, rs, 0.0)

        o = o + _dot(att.astype(BF16), v)

        last = cg[c - 1:c, :]
        kd = (k * jnp.exp(last - cg)).astype(BF16)
        st_ref[h] = st * jnp.exp(last) + lax.dot_general(v, kd, TN_DIMS, preferred_element_type=F32)

        y = o * lax.rsqrt(jnp.mean(o * o, axis=-1, keepdims=True) + EPS) * og
        gt = gate_ref[:, sl].astype(F32)
        o_ref[:, sl] = (y * (gt * jax.nn.sigmoid(gt))).astype(o_ref.dtype)


def _hgrn2(qvg, fz, lb_logits, o_gain, batch, seq, layer):
    nc = seq // C_CHUNK
    hw = C_HEADS_PER_STEP * C_DIM
    nh = C_WIDTH // hw
    row_blk = lambda b, g, c: b * nc + c
    return pl.pallas_call(
        functools.partial(_hgrn2_kernel, layer=layer),
        grid=(batch, nh, nc),
        in_specs=[
            pl.BlockSpec((lb_logits.shape[0], hw), lambda b, g, c: (0, g)),
            pl.BlockSpec((1, C_DIM), lambda b, g, c: (0, 0)),
            pl.BlockSpec((C_CHUNK, hw), lambda b, g, c: (row_blk(b, g, c), g)),
            pl.BlockSpec((C_CHUNK, hw), lambda b, g, c: (row_blk(b, g, c), nh + g)),
            pl.BlockSpec((C_CHUNK, hw), lambda b, g, c: (row_blk(b, g, c), 2 * nh + g)),
            pl.BlockSpec((C_CHUNK, hw), lambda b, g, c: (row_blk(b, g, c), g)),
        ],
        out_specs=pl.BlockSpec((C_CHUNK, hw), lambda b, g, c: (row_blk(b, g, c), g)),
        out_shape=jax.ShapeDtypeStruct((batch * seq, C_WIDTH), BF16),
        scratch_shapes=[pltpu.VMEM((C_HEADS_PER_STEP, C_DIM, C_DIM), F32)],
        compiler_params=_params(("parallel", "parallel", "arbitrary"), 32),
        name="hgrn2",
    )(lb_logits, o_gain.reshape(1, C_DIM), qvg, qvg, qvg, fz)


def _ffn(x, norm_g, w1, w3, w2, layer):
    h, w2b = _ffn_up(_rmsnorm(x, norm_g[layer]), w1, w3, w2, layer)
    return _matmul_res(h, w2b, x, tk=D_FF_PAD // 4)


def kernel(x, ffn1_norm, ffn1_w1, ffn1_w3, ffn1_w2, mix_norm, ffn2_norm, ffn2_w1, ffn2_w3, ffn2_w2, ab_w_in, ab_q_norm, ab_k_norm, ab_sinks, ab_v_ln_g, ab_v_ln_b, ab_w_s, ab_b_s, ab_w_out, c_w_in, c_lb_logits, c_o_norm, c_w_out):
    batch, seq, d = x.shape
    depth = ffn1_norm.shape[0]
    tn = 512
    x = x.reshape(batch * seq, d)
    for l in range(depth):
        x = _ffn(x, ffn1_norm, ffn1_w1, ffn1_w3, ffn1_w2, l)
        hn = _rmsnorm(x, mix_norm[l])
        j = l // 2
        if l % 2 == 0:
            kv_tile = A_WIDTH // tn
            n_tiles = AB_IN // tn
            o_tile = lambda t: jnp.where(t < kv_tile, t, jnp.where(t == kv_tile, n_tiles - 1, t - 1))
            proj = _wres_matmul("ab_in_proj", hn, ab_w_in, j, n_tiles, F32, o_tile=o_tile, gelu_from=kv_tile + 1)
            mixed = _ab_mixer(proj, batch, seq, ab_q_norm[j], ab_k_norm[j], ab_sinks[j],
                              ab_v_ln_g[j], ab_v_ln_b[j], ab_w_s[j], ab_b_s[j])
            x = _wres_matmul("ab_out_proj", mixed, ab_w_out, j, d // tn, F32, res=x)
        else:
            ct = C_WIDTH // tn
            qvg = _wres_matmul("c_in_proj_qvg", hn, c_w_in, j, 3 * ct, BF16,
                               w_tile=lambda t: jnp.where(t < ct, t, t + ct))
            fz = _wres_matmul("c_in_proj_f", hn, c_w_in, j, ct, F32, w_tile=lambda t: t + ct)
            mixed = _hgrn2(qvg, fz, c_lb_logits, c_o_norm[j], batch, seq, l)
            x = _wres_matmul("c_out_proj", mixed, c_w_out, j, d // tn, F32, res=x)
        x = _ffn(x, ffn2_norm, ffn2_w1, ffn2_w3, ffn2_w2, l)
    return x.reshape(batch, seq, d)
```

```python
import functools

import jax
import jax.numpy as jnp
from jax import lax
from jax.experimental import pallas as pl
from jax.experimental.pallas import tpu as pltpu

F32 = jnp.float32
BF16 = jnp.bfloat16

D_MODEL = 4096
D_FF = 11008
FF_TILE = 256
D_FF_PAD = 11264
EPS = 1e-6

A_HEADS = 32
A_KV_HEADS = 4
A_GROUP = A_HEADS // A_KV_HEADS
A_HEAD_DIM = 64
WINDOW = 128
A_WIDTH = A_HEADS * A_HEAD_DIM
A_KV_WIDTH = A_KV_HEADS * A_HEAD_DIM
B_GROUPS = 16
B_GROUP_DIM = 128
B_WIDTH = B_GROUPS * B_GROUP_DIM
AB_IN = A_WIDTH + 2 * A_KV_WIDTH + 2 * B_WIDTH

C_HEADS = 32
C_DIM = 128
C_WIDTH = C_HEADS * C_DIM
C_CHUNK = 128
C_HEADS_PER_STEP = 8

LANES = 128
SUBLANES = 8
MIB = 1 << 20

NT_DIMS = (((1,), (1,)), ((), ()))
TN_DIMS = (((0,), (0,)), ((), ()))


def _params(semantics, vmem_mib):
    return pltpu.CompilerParams(dimension_semantics=semantics, vmem_limit_bytes=vmem_mib * MIB)


def _dot(a, b):
    return jnp.dot(a, b, preferred_element_type=F32)


def _gelu(y):
    return 0.5 * y * (1.0 + lax.erf(y * (2.0 ** -0.5)))


def _silu(y):
    return y * jax.nn.sigmoid(y)


def _rmsnorm_kernel(x_ref, g_ref, o_ref):
    x = x_ref[...]
    r = lax.rsqrt(jnp.mean(x * x, axis=-1, keepdims=True) + EPS)
    o_ref[...] = (x * r * g_ref[...]).astype(o_ref.dtype)


def _rmsnorm(x, g, rows=256):
    m, d = x.shape
    return pl.pallas_call(
        _rmsnorm_kernel,
        grid=(m // rows,),
        in_specs=[pl.BlockSpec((rows, d), lambda i: (i, 0)),
                  pl.BlockSpec((1, d), lambda i: (0, 0))],
        out_specs=pl.BlockSpec((rows, d), lambda i: (i, 0)),
        out_shape=jax.ShapeDtypeStruct((m, d), BF16),
        compiler_params=_params(("parallel",), 32),
        name="rmsnorm",
    )(x, g.reshape(1, d))


def _wres_matmul_kernel(*refs, n_extra, epilogue):
    a_ref, w_ref = refs[:2]
    extra_refs = refs[2:2 + n_extra]
    out_refs = refs[2 + n_extra:-1]
    wb_ref = refs[-1]

    @pl.when(pl.program_id(1) == 0)
    def _():
        wb_ref[...] = w_ref[...].astype(BF16)

    epilogue(pl.program_id(0), _dot(a_ref[...], wb_ref[...]), extra_refs, out_refs)


def _store_epilogue(j, y, extra_refs, out_refs):
    out_refs[0][...] = y.astype(out_refs[0].dtype)


def _residual_epilogue(j, y, extra_refs, out_refs):
    out_refs[0][...] = extra_refs[0][...] + y


def _act_epilogue(j, y, extra_refs, out_refs, *, act, act_from):
    @pl.when(j >= act_from)
    def _():
        out_refs[0][...] = act(y).astype(out_refs[0].dtype)

    @pl.when(j < act_from)
    def _():
        out_refs[0][...] = y.astype(out_refs[0].dtype)


def _forget_gate_epilogue(j, z, extra_refs, out_refs, *, layer):
    logits = extra_refs[0][...]
    lmax = jnp.max(logits, axis=0, keepdims=True)
    ex = jnp.exp(logits - lmax)
    prob = ex / jnp.sum(ex, axis=0, keepdims=True)
    lb = jnp.sum(prob[:layer + 1], axis=0, keepdims=True) - prob[0:1]
    one_m_lb = 1.0 - lb
    t = jnp.exp(-jnp.abs(z))
    sig_big = 1.0 / (1.0 + t)
    sig_small = t * sig_big
    sig_z = jnp.where(z >= 0, sig_big, sig_small)
    sig_nz = jnp.where(z >= 0, sig_small, sig_big)
    out_refs[0][...] = jnp.log2(lb + one_m_lb * sig_z)
    out_refs[1][...] = (one_m_lb * sig_nz).astype(out_refs[1].dtype)


def _wres_matmul(name, a, w, layer, n_tiles, out_dtypes, epilogue=_store_epilogue, w_tile=lambda j: j,
                 o_tile=lambda j: j, tiled_extra=(), col_extra=(), tm=1024, tn=512):
    m, k = a.shape
    out_block = pl.BlockSpec((tm, tn), lambda j, i: (i, o_tile(j)))
    in_specs = [pl.BlockSpec((tm, k), lambda j, i: (i, 0)),
                pl.BlockSpec((None, k, tn), lambda j, i: (layer, 0, w_tile(j)))]
    in_specs += [out_block for _ in tiled_extra]
    in_specs += [pl.BlockSpec((e.shape[0], tn), lambda j, i: (0, o_tile(j))) for e in col_extra]
    outs = pl.pallas_call(
        functools.partial(_wres_matmul_kernel, n_extra=len(tiled_extra) + len(col_extra), epilogue=epilogue),
        grid=(n_tiles, m // tm),
        in_specs=in_specs,
        out_specs=[out_block for _ in out_dtypes],
        out_shape=[jax.ShapeDtypeStruct((m, n_tiles * tn), dt) for dt in out_dtypes],
        scratch_shapes=[pltpu.VMEM((k, tn), BF16)],
        compiler_params=_params(("parallel", "arbitrary"), 56),
        name=name,
    )(a, w, *tiled_extra, *col_extra)
    return outs if len(outs) > 1 else outs[0]


def _ffn_up_kernel(a_ref, w1_ref, w3_ref, w2_ref, h_ref, w2b_ref, w1b_ref, w3b_ref, *, n_tiles):
    j = pl.program_id(0)
    first_row_block = pl.program_id(1) == 0
    real = j < n_tiles

    @pl.when(first_row_block & real)
    def _():
        w1b_ref[...] = w1_ref[...].astype(BF16)
        w3b_ref[...] = w3_ref[...].astype(BF16)
        w2b_ref[...] = w2_ref[...].astype(BF16)

    @pl.when(first_row_block & jnp.logical_not(real))
    def _():
        w2b_ref[...] = jnp.zeros_like(w2b_ref)

    @pl.when(real)
    def _():
        a = a_ref[...]
        y1 = _dot(a, w1b_ref[...])
        y3 = _dot(a, w3b_ref[...])
        h_ref[...] = (0.5 * (y1 * jax.nn.sigmoid(y1)) * y3).astype(h_ref.dtype)

    @pl.when(jnp.logical_not(real))
    def _():
        h_ref[...] = jnp.zeros_like(h_ref)


def _ffn_up(a, w1, w3, w2, layer, tm=1024):
    m, k = a.shape
    tf = FF_TILE
    n_tiles = D_FF // tf
    w_tile = lambda j: jnp.minimum(j, n_tiles - 1)
    return pl.pallas_call(
        functools.partial(_ffn_up_kernel, n_tiles=n_tiles),
        grid=(D_FF_PAD // tf, m // tm),
        in_specs=[pl.BlockSpec((tm, k), lambda j, i: (i, 0)),
                  pl.BlockSpec((None, k, tf), lambda j, i: (layer, 0, w_tile(j))),
                  pl.BlockSpec((None, k, tf), lambda j, i: (layer, 0, w_tile(j))),
                  pl.BlockSpec((None, tf, k), lambda j, i: (layer, w_tile(j), 0))],
        out_specs=[pl.BlockSpec((tm, tf), lambda j, i: (i, j)),
                   pl.BlockSpec((tf, k), lambda j, i: (j, 0))],
        out_shape=[jax.ShapeDtypeStruct((m, D_FF_PAD), BF16),
                   jax.ShapeDtypeStruct((D_FF_PAD, k), BF16)],
        scratch_shapes=[pltpu.VMEM((k, tf), BF16), pltpu.VMEM((k, tf), BF16)],
        compiler_params=_params(("arbitrary", "arbitrary"), 58),
        name="ffn_up",
    )(a, w1, w3, w2)


def _matmul_res_kernel(a_ref, w_ref, r_ref, o_ref):
    kk = pl.program_id(2)
    y = _dot(a_ref[...], w_ref[...])

    @pl.when(kk == 0)
    def _():
        o_ref[...] = r_ref[...] + y

    @pl.when(kk > 0)
    def _():
        o_ref[...] += y


def _matmul_res(a, w, res, tm=1024, tn=1024, tk=None):
    m, k = a.shape
    n = w.shape[1]
    tk = k if tk is None else tk
    return pl.pallas_call(
        _matmul_res_kernel,
        grid=(m // tm, n // tn, k // tk),
        in_specs=[pl.BlockSpec((tm, tk), lambda i, j, kk: (i, kk)),
                  pl.BlockSpec((tk, tn), lambda i, j, kk: (kk, j)),
                  pl.BlockSpec((tm, tn), lambda i, j, kk: (i, j))],
        out_specs=pl.BlockSpec((tm, tn), lambda i, j, kk: (i, j)),
        out_shape=jax.ShapeDtypeStruct((m, n), F32),
        compiler_params=_params(("parallel", "parallel", "arbitrary"), 56),
        name="matmul_res",
    )(a, w, res)


def _ab_mixer_kernel(sinks_ref, q_ref, u_ref, vg_ref, kvp_ref, kvc_ref, qg_ref, kg_ref,
                     lng_ref, lnb_ref, ws_ref, bs_ref, o_ref):
    n = pl.program_id(1)
    w = WINDOW
    hd = A_HEAD_DIM

    qi = lax.broadcasted_iota(jnp.int32, (w, 2 * w), 0)
    kj = lax.broadcasted_iota(jnp.int32, (w, 2 * w), 1)
    dist = qi + w - kj
    mask = (dist >= 0) & (dist < w) & ((kj >= w) | (n > 0))
    mask = jnp.concatenate([mask] * A_GROUP, axis=0)

    kvp = kvp_ref[...]
    kvc = kvc_ref[...]
    qgain = qg_ref[...] * (hd ** -0.5)
    kgain = kg_ref[...]
    for h in range(A_KV_HEADS):
        k = jnp.concatenate([kvp[:, h * hd:(h + 1) * hd], kvc[:, h * hd:(h + 1) * hd]], axis=0)
        v = jnp.concatenate([kvp[:, A_KV_WIDTH + h * hd:A_KV_WIDTH + (h + 1) * hd],
                             kvc[:, A_KV_WIDTH + h * hd:A_KV_WIDTH + (h + 1) * hd]], axis=0)
        k = k * lax.rsqrt(jnp.mean(k * k, axis=-1, keepdims=True) + EPS) * kgain
        q = jnp.concatenate([q_ref[:, (h * A_GROUP + g) * hd:(h * A_GROUP + g + 1) * hd]
                             for g in range(A_GROUP)], axis=0)
        q = q * lax.rsqrt(jnp.mean(q * q, axis=-1, keepdims=True) + EPS) * qgain
        s = lax.dot_general(q.astype(BF16), k.astype(BF16), NT_DIMS, preferred_element_type=F32)
        s = jnp.where(mask, s, -jnp.inf)
        sink = jnp.concatenate([jnp.full((w, 1), sinks_ref[h * A_GROUP + g], F32)
                                for g in range(A_GROUP)], axis=0)
        mx = jnp.maximum(jnp.max(s, axis=-1, keepdims=True), sink)
        e = jnp.exp(s - mx)
        denom = jnp.sum(e, axis=-1, keepdims=True) + jnp.exp(sink - mx)
        p = e / denom
        o = _dot(p.astype(BF16), v.astype(BF16))
        for g in range(A_GROUP):
            c0 = (h * A_GROUP + g) * hd
            o_ref[:, c0:c0 + hd] = o[g * w:(g + 1) * w].astype(o_ref.dtype)

    row = lax.broadcasted_iota(jnp.int32, (w, w), 0)
    col = lax.broadcasted_iota(jnp.int32, (w, w), 1)
    causal = row >= col
    for g in range(B_GROUPS):
        sl = slice(g * B_GROUP_DIM, (g + 1) * B_GROUP_DIM)
        x = vg_ref[:, sl]
        mu = jnp.mean(x, axis=-1, keepdims=True)
        d = x - mu
        var = jnp.mean(d * d, axis=-1, keepdims=True)
        vn = d * lax.rsqrt(var + EPS) * lng_ref[:, sl] + lnb_ref[:, sl]
        wg = jnp.where(causal, ws_ref[g], 0.0).astype(BF16)
        sg = _dot(wg, vn.astype(BF16)) + bs_ref[:, g:g + 1]
        o_ref[:, A_WIDTH + g * B_GROUP_DIM:A_WIDTH + (g + 1) * B_GROUP_DIM] = (
            u_ref[:, sl] * sg).astype(o_ref.dtype)


def _ab_mixer(proj, batch, seq, q_gain, k_gain, sinks, ln_g, ln_b, w_s, b_s):
    nb = seq // WINDOW
    kv_blk = (A_WIDTH + 2 * B_WIDTH) // (2 * A_KV_WIDTH)
    row_blk = lambda b, n: b * nb + n
    return pl.pallas_call(
        _ab_mixer_kernel,
        grid=(batch, nb),
        in_specs=[
            pl.BlockSpec(memory_space=pltpu.SMEM),
            pl.BlockSpec((WINDOW, A_WIDTH), lambda b, n: (row_blk(b, n), 0)),
            pl.BlockSpec((WINDOW, B_WIDTH), lambda b, n: (row_blk(b, n), 1)),
            pl.BlockSpec((WINDOW, B_WIDTH), lambda b, n: (row_blk(b, n), 2)),
            pl.BlockSpec((WINDOW, 2 * A_KV_WIDTH), lambda b, n: (b * nb + jnp.maximum(n - 1, 0), kv_blk)),
            pl.BlockSpec((WINDOW, 2 * A_KV_WIDTH), lambda b, n: (row_blk(b, n), kv_blk)),
            pl.BlockSpec((1, A_HEAD_DIM), lambda b, n: (0, 0)),
            pl.BlockSpec((1, A_HEAD_DIM), lambda b, n: (0, 0)),
            pl.BlockSpec((1, B_WIDTH), lambda b, n: (0, 0)),
            pl.BlockSpec((1, B_WIDTH), lambda b, n: (0, 0)),
            pl.BlockSpec((B_GROUPS, WINDOW, WINDOW), lambda b, n: (0, 0, 0)),
            pl.BlockSpec((WINDOW, B_GROUPS), lambda b, n: (0, 0)),
        ],
        out_specs=pl.BlockSpec((WINDOW, A_WIDTH + B_WIDTH), lambda b, n: (row_blk(b, n), 0)),
        out_shape=jax.ShapeDtypeStruct((batch * seq, A_WIDTH + B_WIDTH), BF16),
        compiler_params=_params(("parallel", "arbitrary"), 32),
        name="ab_mixer",
    )(sinks, proj, proj, proj, proj, proj,
      q_gain.reshape(1, A_HEAD_DIM), k_gain.reshape(1, A_HEAD_DIM),
      ln_g.reshape(1, B_WIDTH), ln_b.reshape(1, B_WIDTH), w_s, b_s.T)


def _hgrn2_kernel(og_ref, q_ref, v_ref, gate_ref, lf_ref, k_ref, o_ref, st_ref):
    c = C_CHUNK

    @pl.when(pl.program_id(2) == 0)
    def _():
        st_ref[...] = jnp.zeros_like(st_ref)

    row = lax.broadcasted_iota(jnp.int32, (c, c), 0)
    col = lax.broadcasted_iota(jnp.int32, (c, c), 1)
    row1 = lax.broadcasted_iota(jnp.int32, (c, 1), 0)
    cumsum_mat = (row >= col).astype(BF16)
    diff_bits = row ^ col
    level = jnp.zeros((c, c), jnp.int32)
    for b in range(1, c.bit_length() - 1):
        level = level + (diff_bits >= (1 << b)).astype(jnp.int32)
    level = jnp.where(col < row, level, jnp.where(col == row, -1, -2))
    og = og_ref[...]

    for h in range(C_HEADS_PER_STEP):
        sl = slice(h * C_DIM, (h + 1) * C_DIM)
        log2_f = lf_ref[:, sl]
        k = k_ref[:, sl].astype(F32)

        p1 = log2_f.astype(BF16)
        p2 = (log2_f - p1.astype(F32)).astype(BF16)
        cg = _dot(cumsum_mat, p1) + _dot(cumsum_mat, p2)

        q = q_ref[:, sl].astype(F32)
        v = v_ref[:, sl]
        st = st_ref[h]
        o = lax.dot_general((q * jnp.exp2(cg)).astype(BF16), st.astype(BF16), NT_DIMS,
                            preferred_element_type=F32)

        att = jnp.where(level == -1, jnp.sum(q * k, axis=-1, keepdims=True), 0.0)
        part = lax.dot_general((q * jnp.exp2(log2_f)).astype(BF16), k_ref[:, sl], NT_DIMS,
                               preferred_element_type=F32)
        att = jnp.where(level == 0, part, att)
        cg3 = cg.reshape(c // SUBLANES, SUBLANES, C_DIM)
        sub_row = lambda r: jnp.broadcast_to(cg3[:, r:r + 1, :], cg3.shape).reshape(c, C_DIM)
        n = 2
        while n < c:
            if 4 * n == SUBLANES:
                ref = jnp.where((row1 & (2 * n)) != 0, sub_row(3 * n - 1), sub_row(n - 1))
            elif 2 * n == SUBLANES:
                ref = sub_row(n - 1)
            else:
                ref = jnp.concatenate([jnp.broadcast_to(cg[lo + n - 1:lo + n, :], (2 * n, C_DIM))
                                       for lo in range(0, c, 2 * n)], axis=0)
            e = jnp.exp2(-jnp.abs(cg - ref))
            part = lax.dot_general((q * e).astype(BF16), (k * e).astype(BF16), NT_DIMS,
                                   preferred_element_type=F32)
            att = jnp.where(level == n.bit_length() - 1, part, att)
            n *= 2

        o = o + _dot(att.astype(BF16), v)

        last = cg[c - 1:c, :]
        kd = (k * jnp.exp2(last - cg)).astype(BF16)
        st_ref[h] = st * jnp.exp2(last) + lax.dot_general(v, kd, TN_DIMS, preferred_element_type=F32)

        y = o * lax.rsqrt(jnp.mean(o * o, axis=-1, keepdims=True) + EPS) * og
        o_ref[:, sl] = (y * gate_ref[:, sl].astype(F32)).astype(o_ref.dtype)


def _hgrn2(qvg, log2_f, k, o_gain, batch, seq):
    nc = seq // C_CHUNK
    hw = C_HEADS_PER_STEP * C_DIM
    nh = C_WIDTH // hw
    blk = lambda first: pl.BlockSpec((C_CHUNK, hw), lambda b, g, c: (b * nc + c, first + g))
    return pl.pallas_call(
        _hgrn2_kernel,
        grid=(batch, nh, nc),
        in_specs=[pl.BlockSpec((1, C_DIM), lambda b, g, c: (0, 0)),
                  blk(0), blk(nh), blk(2 * nh), blk(0), blk(0)],
        out_specs=blk(0),
        out_shape=jax.ShapeDtypeStruct((batch * seq, C_WIDTH), BF16),
        scratch_shapes=[pltpu.VMEM((C_HEADS_PER_STEP, C_DIM, C_DIM), F32)],
        compiler_params=_params(("parallel", "parallel", "arbitrary"), 32),
        name="hgrn2",
    )(o_gain.reshape(1, C_DIM), qvg, qvg, qvg, log2_f, k)


def _ffn(x, norm_g, w1, w3, w2, layer):
    h, w2b = _ffn_up(_rmsnorm(x, norm_g[layer]), w1, w3, w2, layer)
    return _matmul_res(h, w2b, x, tk=D_FF_PAD // 4)


def kernel(x, ffn1_norm, ffn1_w1, ffn1_w3, ffn1_w2, mix_norm, ffn2_norm, ffn2_w1, ffn2_w3, ffn2_w2, ab_w_in, ab_q_norm, ab_k_norm, ab_sinks, ab_v_ln_g, ab_v_ln_b, ab_w_s, ab_b_s, ab_w_out, c_w_in, c_lb_logits, c_o_norm, c_w_out):
    batch, seq, d = x.shape
    depth = ffn1_norm.shape[0]
    tn = 512
    x = x.reshape(batch * seq, d)
    for l in range(depth):
        x = _ffn(x, ffn1_norm, ffn1_w1, ffn1_w3, ffn1_w2, l)
        hn = _rmsnorm(x, mix_norm[l])
        j = l // 2
        if l % 2 == 0:
            kv_tile = A_WIDTH // tn
            n_tiles = AB_IN // tn
            o_tile = lambda t: jnp.where(t < kv_tile, t, jnp.where(t == kv_tile, n_tiles - 1, t - 1))
            proj = _wres_matmul("ab_in_proj", hn, ab_w_in, j, n_tiles, [F32], o_tile=o_tile,
                                epilogue=functools.partial(_act_epilogue, act=_gelu, act_from=kv_tile + 1))
            mixed = _ab_mixer(proj, batch, seq, ab_q_norm[j], ab_k_norm[j], ab_sinks[j],
                              ab_v_ln_g[j], ab_v_ln_b[j], ab_w_s[j], ab_b_s[j])
            x = _wres_matmul("ab_out_proj", mixed, ab_w_out, j, d // tn, [F32], _residual_epilogue, tiled_extra=[x])
        else:
            ct = C_WIDTH // tn
            qvg = _wres_matmul("c_in_proj_qvg", hn, c_w_in, j, 3 * ct, [BF16],
                               functools.partial(_act_epilogue, act=_silu, act_from=2 * ct),
                               w_tile=lambda t: jnp.where(t < ct, t, t + ct))
            log2_f, k = _wres_matmul("c_in_proj_f", hn, c_w_in, j, ct, [F32, BF16],
                                     functools.partial(_forget_gate_epilogue, layer=l),
                                     w_tile=lambda t: t + ct, col_extra=[c_lb_logits])
            mixed = _hgrn2(qvg, log2_f, k, c_o_norm[j], batch, seq)
            x = _wres_matmul("c_out_proj", mixed, c_w_out, j, d // tn, [F32], _residual_epilogue, tiled_extra=[x])
        x = _ffn(x, ffn2_norm, ffn2_w1, ffn2_w3, ffn2_w2, l)
    return x.reshape(batch, seq, d)
```

```python
import functools

import jax
import jax.numpy as jnp
from jax import lax
from jax.experimental import pallas as pl
from jax.experimental.pallas import tpu as pltpu

F32 = jnp.float32
BF16 = jnp.bfloat16

D_MODEL = 4096
D_FF = 11008
FF_TILE = 256
D_FF_PAD = 11264
EPS = 1e-6

A_HEADS = 32
A_KV_HEADS = 4
A_GROUP = A_HEADS // A_KV_HEADS
A_HEAD_DIM = 64
WINDOW = 128
A_WIDTH = A_HEADS * A_HEAD_DIM
A_KV_WIDTH = A_KV_HEADS * A_HEAD_DIM
B_GROUPS = 16
B_GROUP_DIM = 128
B_WIDTH = B_GROUPS * B_GROUP_DIM
AB_IN = A_WIDTH + 2 * A_KV_WIDTH + 2 * B_WIDTH

C_HEADS = 32
C_DIM = 128
C_WIDTH = C_HEADS * C_DIM
C_CHUNK = 128
C_HEADS_PER_STEP = 8

LANES = 128
SUBLANES = 8
MIB = 1 << 20

NT_DIMS = (((1,), (1,)), ((), ()))
TN_DIMS = (((0,), (0,)), ((), ()))


def _params(semantics, vmem_mib):
    return pltpu.CompilerParams(dimension_semantics=semantics, vmem_limit_bytes=vmem_mib * MIB)


def _dot(a, b):
    return jnp.dot(a, b, preferred_element_type=F32)


def _gelu(y):
    return 0.5 * y * (1.0 + lax.erf(y * (2.0 ** -0.5)))


def _silu(y):
    return y * jax.nn.sigmoid(y)


def _rmsnorm_kernel(x_ref, g_ref, o_ref):
    x = x_ref[...]
    r = lax.rsqrt(jnp.mean(x * x, axis=-1, keepdims=True) + EPS)
    o_ref[...] = (x * r * g_ref[...]).astype(o_ref.dtype)


def _rmsnorm(x, g, rows=256):
    m, d = x.shape
    return pl.pallas_call(
        _rmsnorm_kernel,
        grid=(m // rows,),
        in_specs=[pl.BlockSpec((rows, d), lambda i: (i, 0)),
                  pl.BlockSpec((1, d), lambda i: (0, 0))],
        out_specs=pl.BlockSpec((rows, d), lambda i: (i, 0)),
        out_shape=jax.ShapeDtypeStruct((m, d), BF16),
        compiler_params=_params(("parallel",), 32),
        name="rmsnorm",
    )(x, g.reshape(1, d))


def _wres_matmul_kernel(*refs, n_extra, epilogue):
    a_ref, w_ref = refs[:2]
    extra_refs = refs[2:2 + n_extra]
    out_refs = refs[2 + n_extra:-1]
    wb_ref = refs[-1]

    @pl.when(pl.program_id(1) == 0)
    def _():
        wb_ref[...] = w_ref[...].astype(BF16)

    epilogue(pl.program_id(0), _dot(a_ref[...], wb_ref[...]), extra_refs, out_refs)


def _store_epilogue(j, y, extra_refs, out_refs):
    out_refs[0][...] = y.astype(out_refs[0].dtype)


def _residual_epilogue(j, y, extra_refs, out_refs):
    out_refs[0][...] = extra_refs[0][...] + y


def _act_epilogue(j, y, extra_refs, out_refs, *, act, act_from):
    @pl.when(j >= act_from)
    def _():
        out_refs[0][...] = act(y).astype(out_refs[0].dtype)

    @pl.when(j < act_from)
    def _():
        out_refs[0][...] = y.astype(out_refs[0].dtype)


def _forget_gate_epilogue(j, z, extra_refs, out_refs, *, layer):
    logits = extra_refs[0][...]
    lmax = jnp.max(logits, axis=0, keepdims=True)
    ex = jnp.exp(logits - lmax)
    prob = ex / jnp.sum(ex, axis=0, keepdims=True)
    lb = jnp.sum(prob[:layer + 1], axis=0, keepdims=True) - prob[0:1]
    one_m_lb = 1.0 - lb
    t = jnp.exp(-jnp.abs(z))
    sig_big = 1.0 / (1.0 + t)
    sig_small = t * sig_big
    sig_z = jnp.where(z >= 0, sig_big, sig_small)
    sig_nz = jnp.where(z >= 0, sig_small, sig_big)
    out_refs[0][...] = jnp.log2(lb + one_m_lb * sig_z)
    out_refs[1][...] = (one_m_lb * sig_nz).astype(out_refs[1].dtype)


def _wres_matmul(name, a, w, layer, n_tiles, out_dtypes, epilogue=_store_epilogue, w_tile=lambda j: j,
                 o_tile=lambda j: j, tiled_extra=(), col_extra=(), tm=1024, tn=512):
    m, k = a.shape
    out_block = pl.BlockSpec((tm, tn), lambda j, i: (i, o_tile(j)))
    in_specs = [pl.BlockSpec((tm, k), lambda j, i: (i, 0)),
                pl.BlockSpec((None, k, tn), lambda j, i: (layer, 0, w_tile(j)))]
    in_specs += [out_block for _ in tiled_extra]
    in_specs += [pl.BlockSpec((e.shape[0], tn), lambda j, i: (0, o_tile(j))) for e in col_extra]
    outs = pl.pallas_call(
        functools.partial(_wres_matmul_kernel, n_extra=len(tiled_extra) + len(col_extra), epilogue=epilogue),
        grid=(n_tiles, m // tm),
        in_specs=in_specs,
        out_specs=[out_block for _ in out_dtypes],
        out_shape=[jax.ShapeDtypeStruct((m, n_tiles * tn), dt) for dt in out_dtypes],
        scratch_shapes=[pltpu.VMEM((k, tn), BF16)],
        compiler_params=_params(("parallel", "arbitrary"), 56),
        name=name,
    )(a, w, *tiled_extra, *col_extra)
    return outs if len(outs) > 1 else outs[0]


def _ffn_up_kernel(a_ref, w1_ref, w3_ref, w2_ref, h_ref, w2b_ref, w1e_ref, w3e_ref, w1o_ref, w3o_ref, *, n_tiles):
    p = pl.program_id(0)
    i = pl.program_id(1)
    t = p - 1
    slab = w1_ref.shape[0]
    r0 = pl.multiple_of(i * slab, slab)
    slab2 = w2_ref.shape[0]
    q0 = pl.multiple_of(i * slab2, slab2)

    def step(w1_next_ref, w3_next_ref, w1_cur_ref, w3_cur_ref, real):
        w1_next_ref[pl.ds(r0, slab), :] = w1_ref[...].astype(BF16)
        w3_next_ref[pl.ds(r0, slab), :] = w3_ref[...].astype(BF16)
        if real:
            a = a_ref[...]
            y1 = _dot(a, w1_cur_ref[...])
            y3 = _dot(a, w3_cur_ref[...])
            h_ref[...] = (0.5 * (y1 * jax.nn.sigmoid(y1)) * y3).astype(h_ref.dtype)
            w2b_ref[pl.ds(q0, slab2), :] = w2_ref[...].astype(BF16)
        else:
            h_ref[...] = jnp.zeros_like(h_ref)
            w2b_ref[pl.ds(q0, slab2), :] = jnp.zeros((slab2, w2b_ref.shape[1]), BF16)

    real = (t >= 0) & (t < n_tiles)
    even = (p % 2) == 0
    pl.when(real & even)(functools.partial(step, w1e_ref, w3e_ref, w1o_ref, w3o_ref, True))
    pl.when(real & jnp.logical_not(even))(functools.partial(step, w1o_ref, w3o_ref, w1e_ref, w3e_ref, True))
    pl.when(jnp.logical_not(real))(functools.partial(step, w1e_ref, w3e_ref, w1o_ref, w3o_ref, False))


def _ffn_up(a, w1, w3, w2, layer, tm=2048):
    m, k = a.shape
    tf = FF_TILE
    n_tiles = D_FF // tf
    assert n_tiles % 2 == 1
    n_i = m // tm
    n_pos = D_FF_PAD // tf + 1
    next_tile = lambda p: jnp.minimum(p, n_tiles - 1)
    cur_tile = lambda p: jnp.clip(p - 1, 0, n_tiles - 1)
    h_tile = lambda p: jnp.where(p == 0, n_pos - 1, p - 1)
    w2b_tile = lambda p: jnp.maximum(p - 1, 0)
    wb = pltpu.VMEM((k, tf), BF16)
    return pl.pallas_call(
        functools.partial(_ffn_up_kernel, n_tiles=n_tiles),
        grid=(n_pos, n_i),
        in_specs=[pl.BlockSpec((tm, k), lambda p, i: (i, 0)),
                  pl.BlockSpec((None, k // n_i, tf), lambda p, i: (layer, i, next_tile(p))),
                  pl.BlockSpec((None, k // n_i, tf), lambda p, i: (layer, i, next_tile(p))),
                  pl.BlockSpec((None, tf // n_i, k), lambda p, i: (layer, cur_tile(p) * n_i + i, 0))],
        out_specs=[pl.BlockSpec((tm, tf), lambda p, i: (i, h_tile(p))),
                   pl.BlockSpec((tf, k), lambda p, i: (w2b_tile(p), 0))],
        out_shape=[jax.ShapeDtypeStruct((m, n_pos * tf), BF16),
                   jax.ShapeDtypeStruct((D_FF_PAD, k), BF16)],
        scratch_shapes=[wb, wb, wb, wb],
        compiler_params=_params(("arbitrary", "arbitrary"), 60),
        name="ffn_up",
    )(a, w1, w3, w2)


def _matmul_res_kernel(a_ref, w_ref, r_ref, o_ref):
    kk = pl.program_id(2)
    y = _dot(a_ref[...], w_ref[...])

    @pl.when(kk == 0)
    def _():
        o_ref[...] = r_ref[...] + y

    @pl.when(kk > 0)
    def _():
        o_ref[...] += y


def _matmul_res(a, w, res, tk, tm=1024, tn=1024):
    m = a.shape[0]
    k, n = w.shape
    return pl.pallas_call(
        _matmul_res_kernel,
        grid=(m // tm, n // tn, k // tk),
        in_specs=[pl.BlockSpec((tm, tk), lambda i, j, kk: (i, kk)),
                  pl.BlockSpec((tk, tn), lambda i, j, kk: (kk, j)),
                  pl.BlockSpec((tm, tn), lambda i, j, kk: (i, j))],
        out_specs=pl.BlockSpec((tm, tn), lambda i, j, kk: (i, j)),
        out_shape=jax.ShapeDtypeStruct((m, n), F32),
        compiler_params=_params(("parallel", "parallel", "arbitrary"), 56),
        name="matmul_res",
    )(a, w, res)


def _ab_mixer_kernel(sinks_ref, q_ref, u_ref, vg_ref, kvp_ref, kvc_ref, qg_ref, kg_ref,
                     lng_ref, lnb_ref, ws_ref, bs_ref, o_ref):
    n = pl.program_id(1)
    w = WINDOW
    hd = A_HEAD_DIM

    qi = lax.broadcasted_iota(jnp.int32, (w, 2 * w), 0)
    kj = lax.broadcasted_iota(jnp.int32, (w, 2 * w), 1)
    dist = qi + w - kj
    mask = (dist >= 0) & (dist < w) & ((kj >= w) | (n > 0))
    mask = jnp.concatenate([mask] * A_GROUP, axis=0)

    kvp = kvp_ref[...]
    kvc = kvc_ref[...]
    qgain = qg_ref[...] * (hd ** -0.5)
    kgain = kg_ref[...]
    for h in range(A_KV_HEADS):
        k = jnp.concatenate([kvp[:, h * hd:(h + 1) * hd], kvc[:, h * hd:(h + 1) * hd]], axis=0)
        v = jnp.concatenate([kvp[:, A_KV_WIDTH + h * hd:A_KV_WIDTH + (h + 1) * hd],
                             kvc[:, A_KV_WIDTH + h * hd:A_KV_WIDTH + (h + 1) * hd]], axis=0)
        k = k * lax.rsqrt(jnp.mean(k * k, axis=-1, keepdims=True) + EPS) * kgain
        q = jnp.concatenate([q_ref[:, (h * A_GROUP + g) * hd:(h * A_GROUP + g + 1) * hd]
                             for g in range(A_GROUP)], axis=0)
        q = q * lax.rsqrt(jnp.mean(q * q, axis=-1, keepdims=True) + EPS) * qgain
        s = lax.dot_general(q.astype(BF16), k.astype(BF16), NT_DIMS, preferred_element_type=F32)
        s = jnp.where(mask, s, -jnp.inf)
        sink = jnp.concatenate([jnp.full((w, 1), sinks_ref[h * A_GROUP + g], F32)
                                for g in range(A_GROUP)], axis=0)
        mx = jnp.maximum(jnp.max(s, axis=-1, keepdims=True), sink)
        e = jnp.exp(s - mx)
        denom = jnp.sum(e, axis=-1, keepdims=True) + jnp.exp(sink - mx)
        p = e / denom
        o = _dot(p.astype(BF16), v.astype(BF16))
        for g in range(A_GROUP):
            c0 = (h * A_GROUP + g) * hd
            o_ref[:, c0:c0 + hd] = o[g * w:(g + 1) * w].astype(o_ref.dtype)

    row = lax.broadcasted_iota(jnp.int32, (w, w), 0)
    col = lax.broadcasted_iota(jnp.int32, (w, w), 1)
    causal = row >= col
    for g in range(B_GROUPS):
        sl = slice(g * B_GROUP_DIM, (g + 1) * B_GROUP_DIM)
        x = vg_ref[:, sl]
        mu = jnp.mean(x, axis=-1, keepdims=True)
        d = x - mu
        var = jnp.mean(d * d, axis=-1, keepdims=True)
        vn = d * lax.rsqrt(var + EPS) * lng_ref[:, sl] + lnb_ref[:, sl]
        wg = jnp.where(causal, ws_ref[g], 0.0).astype(BF16)
        sg = _dot(wg, vn.astype(BF16)) + bs_ref[:, g:g + 1]
        o_ref[:, A_WIDTH + g * B_GROUP_DIM:A_WIDTH + (g + 1) * B_GROUP_DIM] = (
            u_ref[:, sl] * sg).astype(o_ref.dtype)


def _ab_mixer(proj, batch, seq, q_gain, k_gain, sinks, ln_g, ln_b, w_s, b_s):
    nb = seq // WINDOW
    kv_blk = (A_WIDTH + 2 * B_WIDTH) // (2 * A_KV_WIDTH)
    row_blk = lambda b, n: b * nb + n
    return pl.pallas_call(
        _ab_mixer_kernel,
        grid=(batch, nb),
        in_specs=[
            pl.BlockSpec(memory_space=pltpu.SMEM),
            pl.BlockSpec((WINDOW, A_WIDTH), lambda b, n: (row_blk(b, n), 0)),
            pl.BlockSpec((WINDOW, B_WIDTH), lambda b, n: (row_blk(b, n), 1)),
            pl.BlockSpec((WINDOW, B_WIDTH), lambda b, n: (row_blk(b, n), 2)),
            pl.BlockSpec((WINDOW, 2 * A_KV_WIDTH), lambda b, n: (b * nb + jnp.maximum(n - 1, 0), kv_blk)),
            pl.BlockSpec((WINDOW, 2 * A_KV_WIDTH), lambda b, n: (row_blk(b, n), kv_blk)),
            pl.BlockSpec((1, A_HEAD_DIM), lambda b, n: (0, 0)),
            pl.BlockSpec((1, A_HEAD_DIM), lambda b, n: (0, 0)),
            pl.BlockSpec((1, B_WIDTH), lambda b, n: (0, 0)),
            pl.BlockSpec((1, B_WIDTH), lambda b, n: (0, 0)),
            pl.BlockSpec((B_GROUPS, WINDOW, WINDOW), lambda b, n: (0, 0, 0)),
            pl.BlockSpec((WINDOW, B_GROUPS), lambda b, n: (0, 0)),
        ],
        out_specs=pl.BlockSpec((WINDOW, A_WIDTH + B_WIDTH), lambda b, n: (row_blk(b, n), 0)),
        out_shape=jax.ShapeDtypeStruct((batch * seq, A_WIDTH + B_WIDTH), BF16),
        compiler_params=_params(("parallel", "arbitrary"), 32),
        name="ab_mixer",
    )(sinks, proj, proj, proj, proj, proj,
      q_gain.reshape(1, A_HEAD_DIM), k_gain.reshape(1, A_HEAD_DIM),
      ln_g.reshape(1, B_WIDTH), ln_b.reshape(1, B_WIDTH), w_s, b_s.T)


def _hgrn2_kernel(og_ref, q_ref, v_ref, gate_ref, lf_ref, k_ref, o_ref, st_ref):
    c = C_CHUNK

    @pl.when(pl.program_id(2) == 0)
    def _():
        st_ref[...] = jnp.zeros_like(st_ref)

    row = lax.broadcasted_iota(jnp.int32, (c, c), 0)
    col = lax.broadcasted_iota(jnp.int32, (c, c), 1)
    row1 = lax.broadcasted_iota(jnp.int32, (c, 1), 0)
    cumsum_mat = (row >= col).astype(BF16)
    diff_bits = row ^ col
    level = jnp.zeros((c, c), jnp.int32)
    for b in range(1, c.bit_length() - 1):
        level = level + (diff_bits >= (1 << b)).astype(jnp.int32)
    level = jnp.where(col < row, level, jnp.where(col == row, -1, -2))
    og = og_ref[...]

    for h in range(C_HEADS_PER_STEP):
        sl = slice(h * C_DIM, (h + 1) * C_DIM)
        log2_f = lf_ref[:, sl]
        k = k_ref[:, sl].astype(F32)

        p1 = log2_f.astype(BF16)
        p2 = (log2_f - p1.astype(F32)).astype(BF16)
        cg = _dot(cumsum_mat, p1) + _dot(cumsum_mat, p2)

        q = q_ref[:, sl].astype(F32)
        v = v_ref[:, sl]
        st = st_ref[h]
        o = lax.dot_general((q * jnp.exp2(cg)).astype(BF16), st.astype(BF16), NT_DIMS,
                            preferred_element_type=F32)

        att = jnp.where(level == -1, jnp.sum(q * k, axis=-1, keepdims=True), 0.0)
        part = lax.dot_general((q * jnp.exp2(log2_f)).astype(BF16), k_ref[:, sl], NT_DIMS,
                               preferred_element_type=F32)
        att = jnp.where(level == 0, part, att)
        cg3 = cg.reshape(c // SUBLANES, SUBLANES, C_DIM)
        sub_row = lambda r: jnp.broadcast_to(cg3[:, r:r + 1, :], cg3.shape).reshape(c, C_DIM)
        n = 2
        while n < c:
            if 4 * n == SUBLANES:
                ref = jnp.where((row1 & (2 * n)) != 0, sub_row(3 * n - 1), sub_row(n - 1))
            elif 2 * n == SUBLANES:
                ref = sub_row(n - 1)
            else:
                ref = jnp.concatenate([jnp.broadcast_to(cg[lo + n - 1:lo + n, :], (2 * n, C_DIM))
                                       for lo in range(0, c, 2 * n)], axis=0)
            e = jnp.exp2(-jnp.abs(cg - ref))
            part = lax.dot_general((q * e).astype(BF16), (k * e).astype(BF16), NT_DIMS,
                                   preferred_element_type=F32)
            att = jnp.where(level == n.bit_length() - 1, part, att)
            n *= 2

        o = o + _dot(att.astype(BF16), v)

        last = cg[c - 1:c, :]
        kd = (k * jnp.exp2(last - cg)).astype(BF16)
        st_ref[h] = st * jnp.exp2(last) + lax.dot_general(v, kd, TN_DIMS, preferred_element_type=F32)

        y = o * lax.rsqrt(jnp.mean(o * o, axis=-1, keepdims=True) + EPS) * og
        o_ref[:, sl] = (y * gate_ref[:, sl].astype(F32)).astype(o_ref.dtype)


def _hgrn2(qvg, log2_f, k, o_gain, batch, seq):
    nc = seq // C_CHUNK
    hw = C_HEADS_PER_STEP * C_DIM
    nh = C_WIDTH // hw
    blk = lambda first: pl.BlockSpec((C_CHUNK, hw), lambda b, g, c: (b * nc + c, first + g))
    return pl.pallas_call(
        _hgrn2_kernel,
        grid=(batch, nh, nc),
        in_specs=[pl.BlockSpec((1, C_DIM), lambda b, g, c: (0, 0)),
                  blk(0), blk(nh), blk(2 * nh), blk(0), blk(0)],
        out_specs=blk(0),
        out_shape=jax.ShapeDtypeStruct((batch * seq, C_WIDTH), BF16),
        scratch_shapes=[pltpu.VMEM((C_HEADS_PER_STEP, C_DIM, C_DIM), F32)],
        compiler_params=_params(("parallel", "parallel", "arbitrary"), 32),
        name="hgrn2",
    )(o_gain.reshape(1, C_DIM), qvg, qvg, qvg, log2_f, k)


def _ffn(x, norm_g, w1, w3, w2, layer):
    h, w2b = _ffn_up(_rmsnorm(x, norm_g[layer]), w1, w3, w2, layer)
    return _matmul_res(h, w2b, x, tk=D_FF_PAD // 4)


def kernel(x, ffn1_norm, ffn1_w1, ffn1_w3, ffn1_w2, mix_norm, ffn2_norm, ffn2_w1, ffn2_w3, ffn2_w2, ab_w_in, ab_q_norm, ab_k_norm, ab_sinks, ab_v_ln_g, ab_v_ln_b, ab_w_s, ab_b_s, ab_w_out, c_w_in, c_lb_logits, c_o_norm, c_w_out):
    batch, seq, d = x.shape
    depth = ffn1_norm.shape[0]
    tn = 512
    x = x.reshape(batch * seq, d)
    for l in range(depth):
        x = _ffn(x, ffn1_norm, ffn1_w1, ffn1_w3, ffn1_w2, l)
        hn = _rmsnorm(x, mix_norm[l])
        j = l // 2
        if l % 2 == 0:
            kv_tile = A_WIDTH // tn
            n_tiles = AB_IN // tn
            o_tile = lambda t: jnp.where(t < kv_tile, t, jnp.where(t == kv_tile, n_tiles - 1, t - 1))
            proj = _wres_matmul("ab_in_proj", hn, ab_w_in, j, n_tiles, [F32], o_tile=o_tile,
                                epilogue=functools.partial(_act_epilogue, act=_gelu, act_from=kv_tile + 1))
            mixed = _ab_mixer(proj, batch, seq, ab_q_norm[j], ab_k_norm[j], ab_sinks[j],
                              ab_v_ln_g[j], ab_v_ln_b[j], ab_w_s[j], ab_b_s[j])
            x = _wres_matmul("ab_out_proj", mixed, ab_w_out, j, d // tn, [F32], _residual_epilogue, tiled_extra=[x])
        else:
            ct = C_WIDTH // tn
            qvg = _wres_matmul("c_in_proj_qvg", hn, c_w_in, j, 3 * ct, [BF16],
                               functools.partial(_act_epilogue, act=_silu, act_from=2 * ct),
                               w_tile=lambda t: jnp.where(t < ct, t, t + ct))
            log2_f, k = _wres_matmul("c_in_proj_f", hn, c_w_in, j, ct, [F32, BF16],
                                     functools.partial(_forget_gate_epilogue, layer=l),
                                     w_tile=lambda t: t + ct, col_extra=[c_lb_logits])
            mixed = _hgrn2(qvg, log2_f, k, c_o_norm[j], batch, seq)
            x = _wres_matmul("c_out_proj", mixed, c_w_out, j, d // tn, [F32], _residual_epilogue, tiled_extra=[x])
        x = _ffn(x, ffn2_norm, ffn2_w1, ffn2_w3, ffn2_w2, l)
    return x.reshape(batch, seq, d)
```

```python
import functools

import jax
import jax.numpy as jnp
from jax import lax
from jax.experimental import pallas as pl
from jax.experimental.pallas import tpu as pltpu

F32 = jnp.float32
BF16 = jnp.bfloat16

D_MODEL = 4096
D_FF = 11008
FF_TILE = 256
D_FF_PAD = 11264
EPS = 1e-6

A_HEADS = 32
A_KV_HEADS = 4
A_GROUP = A_HEADS // A_KV_HEADS
A_HEAD_DIM = 64
WINDOW = 128
A_WIDTH = A_HEADS * A_HEAD_DIM
A_KV_WIDTH = A_KV_HEADS * A_HEAD_DIM
B_GROUPS = 16
B_GROUP_DIM = 128
B_WIDTH = B_GROUPS * B_GROUP_DIM
AB_IN = A_WIDTH + 2 * A_KV_WIDTH + 2 * B_WIDTH

C_HEADS = 32
C_DIM = 128
C_WIDTH = C_HEADS * C_DIM
C_CHUNK = 128
C_HEADS_PER_STEP = 8

LANES = 128
SUBLANES = 8
MIB = 1 << 20
LOG2E = 1.4426950408889634

NT_DIMS = (((1,), (1,)), ((), ()))
TN_DIMS = (((0,), (0,)), ((), ()))


def _params(semantics, vmem_mib):
    return pltpu.CompilerParams(dimension_semantics=semantics, vmem_limit_bytes=vmem_mib * MIB)


def _dot(a, b):
    return jnp.dot(a, b, preferred_element_type=F32)


def _gelu(y):
    return 0.5 * y * (1.0 + lax.erf(y * (2.0 ** -0.5)))


def _silu(y):
    return y * jax.nn.sigmoid(y)


def _rmsnorm_kernel(x_ref, g_ref, o_ref):
    x = x_ref[...]
    r = lax.rsqrt(jnp.mean(x * x, axis=-1, keepdims=True) + EPS)
    o_ref[...] = (x * r * g_ref[...]).astype(o_ref.dtype)


def _rmsnorm(x, g, rows=256):
    m, d = x.shape
    return pl.pallas_call(
        _rmsnorm_kernel,
        grid=(m // rows,),
        in_specs=[pl.BlockSpec((rows, d), lambda i: (i, 0)),
                  pl.BlockSpec((1, d), lambda i: (0, 0))],
        out_specs=pl.BlockSpec((rows, d), lambda i: (i, 0)),
        out_shape=jax.ShapeDtypeStruct((m, d), BF16),
        compiler_params=_params(("parallel",), 32),
        name="rmsnorm",
    )(x, g.reshape(1, d))


def _wres_matmul_kernel(*refs, n_extra, epilogue):
    a_ref, w_ref = refs[:2]
    extra_refs = refs[2:2 + n_extra]
    out_refs = refs[2 + n_extra:-1]
    wb_ref = refs[-1]

    @pl.when(pl.program_id(1) == 0)
    def _():
        wb_ref[...] = w_ref[...].astype(BF16)

    epilogue(pl.program_id(0), _dot(a_ref[...], wb_ref[...]), extra_refs, out_refs)


def _store_epilogue(j, y, extra_refs, out_refs):
    out_refs[0][...] = y.astype(out_refs[0].dtype)


def _residual_epilogue(j, y, extra_refs, out_refs):
    out_refs[0][...] = extra_refs[0][...] + y


def _act_epilogue(j, y, extra_refs, out_refs, *, act, act_from):
    @pl.when(j >= act_from)
    def _():
        out_refs[0][...] = act(y).astype(out_refs[0].dtype)

    @pl.when(j < act_from)
    def _():
        out_refs[0][...] = y.astype(out_refs[0].dtype)


def _forget_gate_epilogue(j, z, extra_refs, out_refs, *, layer):
    logits = extra_refs[0][...]
    lmax = jnp.max(logits, axis=0, keepdims=True)
    ex = jnp.exp(logits - lmax)
    prob = ex / jnp.sum(ex, axis=0, keepdims=True)
    lb = jnp.sum(prob[:layer + 1], axis=0, keepdims=True) - prob[0:1]
    one_m_lb = 1.0 - lb
    t = jnp.exp(-jnp.abs(z))
    sig_big = 1.0 / (1.0 + t)
    sig_small = t * sig_big
    sig_z = jnp.where(z >= 0, sig_big, sig_small)
    sig_nz = jnp.where(z >= 0, sig_small, sig_big)
    out_refs[0][...] = jnp.log2(lb + one_m_lb * sig_z)
    out_refs[1][...] = (one_m_lb * sig_nz).astype(out_refs[1].dtype)


def _wres_matmul(name, a, w, layer, n_tiles, out_dtypes, epilogue=_store_epilogue, w_tile=lambda j: j,
                 o_tile=lambda j: j, tiled_extra=(), col_extra=(), tm=1024, tn=512):
    m, k = a.shape
    out_block = pl.BlockSpec((tm, tn), lambda j, i: (i, o_tile(j)))
    in_specs = [pl.BlockSpec((tm, k), lambda j, i: (i, 0)),
                pl.BlockSpec((None, k, tn), lambda j, i: (layer, 0, w_tile(j)))]
    in_specs += [out_block for _ in tiled_extra]
    in_specs += [pl.BlockSpec((e.shape[0], tn), lambda j, i: (0, o_tile(j))) for e in col_extra]
    outs = pl.pallas_call(
        functools.partial(_wres_matmul_kernel, n_extra=len(tiled_extra) + len(col_extra), epilogue=epilogue),
        grid=(n_tiles, m // tm),
        in_specs=in_specs,
        out_specs=[out_block for _ in out_dtypes],
        out_shape=[jax.ShapeDtypeStruct((m, n_tiles * tn), dt) for dt in out_dtypes],
        scratch_shapes=[pltpu.VMEM((k, tn), BF16)],
        compiler_params=_params(("parallel", "arbitrary"), 56),
        name=name,
    )(a, w, *tiled_extra, *col_extra)
    return outs if len(outs) > 1 else outs[0]


def _ffn_up_kernel(a_ref, w1_ref, w3_ref, w2_ref, h_ref, w2b_ref, w1e_ref, w3e_ref, w1o_ref, w3o_ref, *, n_tiles):
    p = pl.program_id(0)
    i = pl.program_id(1)
    t = p - 1
    slab = w1_ref.shape[0]
    r0 = pl.multiple_of(i * slab, slab)
    slab2 = w2_ref.shape[0]
    q0 = pl.multiple_of(i * slab2, slab2)

    def step(w1_next_ref, w3_next_ref, w1_cur_ref, w3_cur_ref, real):
        w1_next_ref[pl.ds(r0, slab), :] = w1_ref[...].astype(BF16)
        w3_next_ref[pl.ds(r0, slab), :] = w3_ref[...].astype(BF16)
        if real:
            a = a_ref[...]
            y1 = _dot(a, w1_cur_ref[...])
            y3 = _dot(a, w3_cur_ref[...])
            h_ref[...] = (0.5 * (y1 * jax.nn.sigmoid(y1)) * y3).astype(h_ref.dtype)
            w2b_ref[pl.ds(q0, slab2), :] = w2_ref[...].astype(BF16)
        else:
            h_ref[...] = jnp.zeros_like(h_ref)
            w2b_ref[pl.ds(q0, slab2), :] = jnp.zeros((slab2, w2b_ref.shape[1]), BF16)

    real = (t >= 0) & (t < n_tiles)
    even = (p % 2) == 0
    pl.when(real & even)(functools.partial(step, w1e_ref, w3e_ref, w1o_ref, w3o_ref, True))
    pl.when(real & jnp.logical_not(even))(functools.partial(step, w1o_ref, w3o_ref, w1e_ref, w3e_ref, True))
    pl.when(jnp.logical_not(real))(functools.partial(step, w1e_ref, w3e_ref, w1o_ref, w3o_ref, False))


def _ffn_up(a, w1, w3, w2, layer, tm=2048):
    m, k = a.shape
    tf = FF_TILE
    n_tiles = D_FF // tf
    assert n_tiles % 2 == 1
    n_i = m // tm
    n_pos = D_FF_PAD // tf + 1
    next_tile = lambda p: jnp.minimum(p, n_tiles - 1)
    cur_tile = lambda p: jnp.clip(p - 1, 0, n_tiles - 1)
    h_tile = lambda p: jnp.where(p == 0, n_pos - 1, p - 1)
    w2b_tile = lambda p: jnp.maximum(p - 1, 0)
    wb = pltpu.VMEM((k, tf), BF16)
    return pl.pallas_call(
        functools.partial(_ffn_up_kernel, n_tiles=n_tiles),
        grid=(n_pos, n_i),
        in_specs=[pl.BlockSpec((tm, k), lambda p, i: (i, 0)),
                  pl.BlockSpec((None, k // n_i, tf), lambda p, i: (layer, i, next_tile(p))),
                  pl.BlockSpec((None, k // n_i, tf), lambda p, i: (layer, i, next_tile(p))),
                  pl.BlockSpec((None, tf // n_i, k), lambda p, i: (layer, cur_tile(p) * n_i + i, 0))],
        out_specs=[pl.BlockSpec((tm, tf), lambda p, i: (i, h_tile(p))),
                   pl.BlockSpec((tf, k), lambda p, i: (w2b_tile(p), 0))],
        out_shape=[jax.ShapeDtypeStruct((m, n_pos * tf), BF16),
                   jax.ShapeDtypeStruct((D_FF_PAD, k), BF16)],
        scratch_shapes=[wb, wb, wb, wb],
        compiler_params=_params(("arbitrary", "arbitrary"), 60),
        name="ffn_up",
    )(a, w1, w3, w2)


def _matmul_res_kernel(a_ref, w_ref, r_ref, o_ref):
    base = jnp.where(pl.program_id(2) == 0, r_ref[...], o_ref[...])
    o_ref[...] = base + _dot(a_ref[...], w_ref[...])


def _matmul_res(a, w, res, tk, tm=1024, tn=1024):
    m = a.shape[0]
    k, n = w.shape
    return pl.pallas_call(
        _matmul_res_kernel,
        grid=(m // tm, n // tn, k // tk),
        in_specs=[pl.BlockSpec((tm, tk), lambda i, j, kk: (i, kk)),
                  pl.BlockSpec((tk, tn), lambda i, j, kk: (kk, j)),
                  pl.BlockSpec((tm, tn), lambda i, j, kk: (i, j))],
        out_specs=pl.BlockSpec((tm, tn), lambda i, j, kk: (i, j)),
        out_shape=jax.ShapeDtypeStruct((m, n), F32),
        compiler_params=_params(("parallel", "parallel", "arbitrary"), 56),
        name="matmul_res",
    )(a, w, res)


def _ab_mixer_kernel(sinks_ref, q_ref, u_ref, vg_ref, kvp_ref, kvc_ref, qg_ref, kg_ref,
                     lng_ref, lnb_ref, ws_ref, bs_ref, o_ref):
    n = pl.program_id(1)
    w = WINDOW
    hd = A_HEAD_DIM

    key = lax.broadcasted_iota(jnp.int32, (2 * w, w), 0)
    qry = lax.broadcasted_iota(jnp.int32, (2 * w, w), 1)
    dist = qry + w - key
    visible = (dist >= 0) & (dist < w) & ((key >= w) | (n > 0))
    low = lax.broadcasted_iota(jnp.int32, (1, LANES), 1) < hd
    sub8 = lax.broadcasted_iota(jnp.int32, (SUBLANES, LANES), 0)
    ones_low = jnp.where(low, 1.0, 0.0)
    ones_high = 1.0 - ones_low
    sel2 = jnp.where(sub8 < SUBLANES // 2, ones_low, ones_high).astype(BF16)
    sink_row = sub8 == 0

    def split_bf16(x):
        hi = x.astype(BF16)
        return hi, (x - hi.astype(F32)).astype(BF16)

    kk = jnp.concatenate([kvp_ref[:, :A_KV_WIDTH], kvc_ref[:, :A_KV_WIDTH]], axis=0)
    vv = jnp.concatenate([kvp_ref[:, A_KV_WIDTH:], kvc_ref[:, A_KV_WIDTH:]], axis=0)
    seg_r = lax.broadcasted_iota(jnp.int32, (A_KV_WIDTH, A_KV_WIDTH), 0) // hd
    seg_c = lax.broadcasted_iota(jnp.int32, (A_KV_WIDTH, A_KV_WIDTH), 1) // hd
    seg = (seg_r == seg_c).astype(BF16)
    hi, lo = split_bf16(kk * kk)
    kss = _dot(hi, seg) + _dot(lo, seg)
    kgain = jnp.concatenate([kg_ref[...]] * A_KV_HEADS, axis=1)
    khat = kk * lax.rsqrt(kss * (1.0 / hd) + EPS) * kgain
    qgain = jnp.concatenate([qg_ref[...]] * 2, axis=1) * (hd ** -0.5 * LOG2E)

    def head_pair_operands(x, h):
        tile = x[:, (h // 2) * LANES:(h // 2 + 1) * LANES]
        swapped = pltpu.roll(tile, hd, axis=1)
        in_low, in_high = (tile, swapped) if h % 2 == 0 else (swapped, tile)
        return jnp.where(low, in_low, 0.0), jnp.where(low, 0.0, in_high)

    for h in range(A_KV_HEADS):
        k_low, k_high = head_pair_operands(khat, h)
        v_low, v_high = head_pair_operands(vv, h)
        k_pad = (k_low.astype(BF16), k_high.astype(BF16))
        v_aug = tuple(
            jnp.concatenate([
                jnp.concatenate([v_p, jnp.broadcast_to(ones_p, v_p.shape)], axis=1),
                jnp.concatenate([jnp.zeros((SUBLANES, LANES), F32), jnp.broadcast_to(ones_p, (SUBLANES, LANES))], axis=1),
            ], axis=0).astype(BF16)
            for v_p, ones_p in ((v_low, ones_low), (v_high, ones_high)))
        for c in range(h * A_GROUP // 2, (h + 1) * A_GROUP // 2):
            qc = q_ref[:, c * LANES:(c + 1) * LANES]
            hi, lo = split_bf16(qc * qc)
            qss = (lax.dot_general(sel2, hi, NT_DIMS, preferred_element_type=F32)
                   + lax.dot_general(sel2, lo, NT_DIMS, preferred_element_type=F32))
            qscale = lax.rsqrt(qss * (1.0 / hd) + EPS)
            qg = (qc * qgain).astype(BF16)
            acc = jnp.zeros((w, 2 * LANES), F32)
            for parity in range(2):
                st = lax.dot_general(k_pad[parity], qg, NT_DIMS, preferred_element_type=F32)
                r = qscale[parity * (SUBLANES // 2):parity * (SUBLANES // 2) + 1]
                st = jnp.where(visible, st * r, -jnp.inf)
                s_sink = sinks_ref[2 * c + parity] * LOG2E
                mx = jnp.maximum(jnp.max(st, axis=0, keepdims=True), s_sink)
                e = jnp.exp2(st - mx)
                e_sink = jnp.where(sink_row, jnp.exp2(s_sink - mx), 0.0)
                e_aug = jnp.concatenate([e, e_sink], axis=0).astype(BF16)
                acc = acc + lax.dot_general(e_aug, v_aug[parity], TN_DIMS, preferred_element_type=F32)
            o_ref[:, c * LANES:(c + 1) * LANES] = (acc[:, :LANES] / acc[:, LANES:]).astype(o_ref.dtype)

    row = lax.broadcasted_iota(jnp.int32, (w, w), 0)
    col = lax.broadcasted_iota(jnp.int32, (w, w), 1)
    causal = row >= col
    for g in range(B_GROUPS):
        sl = slice(g * B_GROUP_DIM, (g + 1) * B_GROUP_DIM)
        x = vg_ref[:, sl]
        mu = jnp.mean(x, axis=-1, keepdims=True)
        d = x - mu
        var = jnp.mean(d * d, axis=-1, keepdims=True)
        vn = d * lax.rsqrt(var + EPS) * lng_ref[:, sl] + lnb_ref[:, sl]
        wg = jnp.where(causal, ws_ref[g], 0.0).astype(BF16)
        sg = _dot(wg, vn.astype(BF16)) + bs_ref[:, g:g + 1]
        o_ref[:, A_WIDTH + g * B_GROUP_DIM:A_WIDTH + (g + 1) * B_GROUP_DIM] = (
            u_ref[:, sl] * sg).astype(o_ref.dtype)


def _ab_mixer(proj, batch, seq, q_gain, k_gain, sinks, ln_g, ln_b, w_s, b_s):
    nb = seq // WINDOW
    kv_blk = (A_WIDTH + 2 * B_WIDTH) // (2 * A_KV_WIDTH)
    row_blk = lambda b, n: b * nb + n
    return pl.pallas_call(
        _ab_mixer_kernel,
        grid=(batch, nb),
        in_specs=[
            pl.BlockSpec(memory_space=pltpu.SMEM),
            pl.BlockSpec((WINDOW, A_WIDTH), lambda b, n: (row_blk(b, n), 0)),
            pl.BlockSpec((WINDOW, B_WIDTH), lambda b, n: (row_blk(b, n), 1)),
            pl.BlockSpec((WINDOW, B_WIDTH), lambda b, n: (row_blk(b, n), 2)),
            pl.BlockSpec((WINDOW, 2 * A_KV_WIDTH), lambda b, n: (b * nb + jnp.maximum(n - 1, 0), kv_blk)),
            pl.BlockSpec((WINDOW, 2 * A_KV_WIDTH), lambda b, n: (row_blk(b, n), kv_blk)),
            pl.BlockSpec((1, A_HEAD_DIM), lambda b, n: (0, 0)),
            pl.BlockSpec((1, A_HEAD_DIM), lambda b, n: (0, 0)),
            pl.BlockSpec((1, B_WIDTH), lambda b, n: (0, 0)),
            pl.BlockSpec((1, B_WIDTH), lambda b, n: (0, 0)),
            pl.BlockSpec((B_GROUPS, WINDOW, WINDOW), lambda b, n: (0, 0, 0)),
            pl.BlockSpec((WINDOW, B_GROUPS), lambda b, n: (0, 0)),
        ],
        out_specs=pl.BlockSpec((WINDOW, A_WIDTH + B_WIDTH), lambda b, n: (row_blk(b, n), 0)),
        out_shape=jax.ShapeDtypeStruct((batch * seq, A_WIDTH + B_WIDTH), BF16),
        compiler_params=_params(("parallel", "arbitrary"), 32),
        name="ab_mixer",
    )(sinks, proj, proj, proj, proj, proj,
      q_gain.reshape(1, A_HEAD_DIM), k_gain.reshape(1, A_HEAD_DIM),
      ln_g.reshape(1, B_WIDTH), ln_b.reshape(1, B_WIDTH), w_s, b_s.T)


def _hgrn2_kernel(og_ref, q_ref, v_ref, gate_ref, lf_ref, k_ref, o_ref, st_ref):
    c = C_CHUNK

    @pl.when(pl.program_id(2) == 0)
    def _():
        st_ref[...] = jnp.zeros_like(st_ref)

    row = lax.broadcasted_iota(jnp.int32, (c, c), 0)
    col = lax.broadcasted_iota(jnp.int32, (c, c), 1)
    row1 = lax.broadcasted_iota(jnp.int32, (c, 1), 0)
    cumsum_mat = (row >= col).astype(BF16)
    diff_bits = row ^ col
    level = jnp.zeros((c, c), jnp.int32)
    for b in range(1, c.bit_length() - 1):
        level = level + (diff_bits >= (1 << b)).astype(jnp.int32)
    level = jnp.where(col < row, level, jnp.where(col == row, -1, -2))
    og = og_ref[...]

    for h in range(C_HEADS_PER_STEP):
        sl = slice(h * C_DIM, (h + 1) * C_DIM)
        log2_f = lf_ref[:, sl]
        k = k_ref[:, sl].astype(F32)

        p1 = log2_f.astype(BF16)
        p2 = (log2_f - p1.astype(F32)).astype(BF16)
        cg = _dot(cumsum_mat, p1) + _dot(cumsum_mat, p2)

        q = q_ref[:, sl].astype(F32)
        v = v_ref[:, sl]
        st = st_ref[h]
        o = lax.dot_general((q * jnp.exp2(cg)).astype(BF16), st.astype(BF16), NT_DIMS,
                            preferred_element_type=F32)

        att = jnp.where(level == -1, jnp.sum(q * k, axis=-1, keepdims=True), 0.0)
        part = lax.dot_general((q * jnp.exp2(log2_f)).astype(BF16), k_ref[:, sl], NT_DIMS,
                               preferred_element_type=F32)
        att = jnp.where(level == 0, part, att)
        cg3 = cg.reshape(c // SUBLANES, SUBLANES, C_DIM)
        sub_row = lambda r: jnp.broadcast_to(cg3[:, r:r + 1, :], cg3.shape).reshape(c, C_DIM)
        n = 2
        while n < c:
            if 4 * n == SUBLANES:
                ref = jnp.where((row1 & (2 * n)) != 0, sub_row(3 * n - 1), sub_row(n - 1))
            elif 2 * n == SUBLANES:
                ref = sub_row(n - 1)
            else:
                ref = jnp.concatenate([jnp.broadcast_to(cg[lo + n - 1:lo + n, :], (2 * n, C_DIM))
                                       for lo in range(0, c, 2 * n)], axis=0)
            e = jnp.exp2(-jnp.abs(cg - ref))
            part = lax.dot_general((q * e).astype(BF16), (k * e).astype(BF16), NT_DIMS,
                                   preferred_element_type=F32)
            att = jnp.where(level == n.bit_length() - 1, part, att)
            n *= 2

        o = o + _dot(att.astype(BF16), v)

        last = cg[c - 1:c, :]
        kd = (k * jnp.exp2(last - cg)).astype(BF16)
        st_ref[h] = st * jnp.exp2(last) + lax.dot_general(v, kd, TN_DIMS, preferred_element_type=F32)

        y = o * lax.rsqrt(jnp.mean(o * o, axis=-1, keepdims=True) + EPS) * og
        o_ref[:, sl] = (y * gate_ref[:, sl].astype(F32)).astype(o_ref.dtype)


def _hgrn2(qvg, log2_f, k, o_gain, batch, seq):
    nc = seq // C_CHUNK
    hw = C_HEADS_PER_STEP * C_DIM
    nh = C_WIDTH // hw
    blk = lambda first: pl.BlockSpec((C_CHUNK, hw), lambda b, g, c: (b * nc + c, first + g))
    return pl.pallas_call(
        _hgrn2_kernel,
        grid=(batch, nh, nc),
        in_specs=[pl.BlockSpec((1, C_DIM), lambda b, g, c: (0, 0)),
                  blk(0), blk(nh), blk(2 * nh), blk(0), blk(0)],
        out_specs=blk(0),
        out_shape=jax.ShapeDtypeStruct((batch * seq, C_WIDTH), BF16),
        scratch_shapes=[pltpu.VMEM((C_HEADS_PER_STEP, C_DIM, C_DIM), F32)],
        compiler_params=_params(("parallel", "parallel", "arbitrary"), 32),
        name="hgrn2",
    )(o_gain.reshape(1, C_DIM), qvg, qvg, qvg, log2_f, k)


def _ffn(x, norm_g, w1, w3, w2, layer):
    h, w2b = _ffn_up(_rmsnorm(x, norm_g[layer]), w1, w3, w2, layer)
    return _matmul_res(h, w2b, x, tk=D_FF_PAD // 4)


def kernel(x, ffn1_norm, ffn1_w1, ffn1_w3, ffn1_w2, mix_norm, ffn2_norm, ffn2_w1, ffn2_w3, ffn2_w2, ab_w_in, ab_q_norm, ab_k_norm, ab_sinks, ab_v_ln_g, ab_v_ln_b, ab_w_s, ab_b_s, ab_w_out, c_w_in, c_lb_logits, c_o_norm, c_w_out):
    batch, seq, d = x.shape
    depth = ffn1_norm.shape[0]
    tn = 512
    x = x.reshape(batch * seq, d)
    for l in range(depth):
        x = _ffn(x, ffn1_norm, ffn1_w1, ffn1_w3, ffn1_w2, l)
        hn = _rmsnorm(x, mix_norm[l])
        j = l // 2
        if l % 2 == 0:
            kv_tile = A_WIDTH // tn
            n_tiles = AB_IN // tn
            o_tile = lambda t: jnp.where(t < kv_tile, t, jnp.where(t == kv_tile, n_tiles - 1, t - 1))
            proj = _wres_matmul("ab_in_proj", hn, ab_w_in, j, n_tiles, [F32], o_tile=o_tile,
                                epilogue=functools.partial(_act_epilogue, act=_gelu, act_from=kv_tile + 1))
            mixed = _ab_mixer(proj, batch, seq, ab_q_norm[j], ab_k_norm[j], ab_sinks[j],
                              ab_v_ln_g[j], ab_v_ln_b[j], ab_w_s[j], ab_b_s[j])
            x = _wres_matmul("ab_out_proj", mixed, ab_w_out, j, d // tn, [F32], _residual_epilogue, tiled_extra=[x])
        else:
            ct = C_WIDTH // tn
            qvg = _wres_matmul("c_in_proj_qvg", hn, c_w_in, j, 3 * ct, [BF16],
                               functools.partial(_act_epilogue, act=_silu, act_from=2 * ct),
                               w_tile=lambda t: jnp.where(t < ct, t, t + ct))
            log2_f, k = _wres_matmul("c_in_proj_f", hn, c_w_in, j, ct, [F32, BF16],
                                     functools.partial(_forget_gate_epilogue, layer=l),
                                     w_tile=lambda t: t + ct, col_extra=[c_lb_logits])
            mixed = _hgrn2(qvg, log2_f, k, c_o_norm[j], batch, seq)
            x = _wres_matmul("c_out_proj", mixed, c_w_out, j, d // tn, [F32], _residual_epilogue, tiled_extra=[x])
        x = _ffn(x, ffn2_norm, ffn2_w1, ffn2_w3, ffn2_w2, l)
    return x.reshape(batch, seq, d)
```

```python
import functools

import jax
import jax.numpy as jnp
from jax import lax
from jax.experimental import pallas as pl
from jax.experimental.pallas import tpu as pltpu

F32 = jnp.float32
BF16 = jnp.bfloat16

D_MODEL = 4096
D_FF = 11008
FF_TILE = 256
D_FF_PAD = 11264
EPS = 1e-6

A_HEADS = 32
A_KV_HEADS = 4
A_GROUP = A_HEADS // A_KV_HEADS
A_HEAD_DIM = 64
WINDOW = 128
A_WIDTH = A_HEADS * A_HEAD_DIM
A_KV_WIDTH = A_KV_HEADS * A_HEAD_DIM
B_GROUPS = 16
B_GROUP_DIM = 128
B_WIDTH = B_GROUPS * B_GROUP_DIM
AB_IN = A_WIDTH + 2 * A_KV_WIDTH + 2 * B_WIDTH

C_HEADS = 32
C_DIM = 128
C_WIDTH = C_HEADS * C_DIM
C_CHUNK = 128
C_HEADS_PER_STEP = 16

LANES = 128
SUBLANES = 8
MIB = 1 << 20
LOG2E = 1.4426950408889634

NT_DIMS = (((1,), (1,)), ((), ()))
TN_DIMS = (((0,), (0,)), ((), ()))


def _params(semantics, vmem_mib):
    return pltpu.CompilerParams(dimension_semantics=semantics, vmem_limit_bytes=vmem_mib * MIB)


def _dot(a, b):
    return jnp.dot(a, b, preferred_element_type=F32)


def _gelu(y):
    return 0.5 * y * (1.0 + lax.erf(y * (2.0 ** -0.5)))


def _silu(y):
    return y * jax.nn.sigmoid(y)


def _rmsnorm_kernel(x_ref, g_ref, o_ref):
    x = x_ref[...]
    r = lax.rsqrt(jnp.mean(x * x, axis=-1, keepdims=True) + EPS)
    o_ref[...] = (x * r * g_ref[...]).astype(o_ref.dtype)


def _rmsnorm(x, g, rows=256):
    m, d = x.shape
    return pl.pallas_call(
        _rmsnorm_kernel,
        grid=(m // rows,),
        in_specs=[pl.BlockSpec((rows, d), lambda i: (i, 0)),
                  pl.BlockSpec((1, d), lambda i: (0, 0))],
        out_specs=pl.BlockSpec((rows, d), lambda i: (i, 0)),
        out_shape=jax.ShapeDtypeStruct((m, d), BF16),
        compiler_params=_params(("parallel",), 32),
        name="rmsnorm",
    )(x, g.reshape(1, d))


def _wres_matmul_kernel(*refs, n_extra, epilogue):
    a_ref, w_ref = refs[:2]
    extra_refs = refs[2:2 + n_extra]
    out_refs = refs[2 + n_extra:-1]
    wb_ref = refs[-1]

    @pl.when(pl.program_id(1) == 0)
    def _():
        wb_ref[...] = w_ref[...].astype(BF16)

    epilogue(pl.program_id(0), _dot(a_ref[...], wb_ref[...]), extra_refs, out_refs)


def _store_epilogue(j, y, extra_refs, out_refs):
    out_refs[0][...] = y.astype(out_refs[0].dtype)


def _residual_epilogue(j, y, extra_refs, out_refs):
    out_refs[0][...] = extra_refs[0][...] + y


def _act_epilogue(j, y, extra_refs, out_refs, *, act, act_from):
    @pl.when(j >= act_from)
    def _():
        out_refs[0][...] = act(y).astype(out_refs[0].dtype)

    @pl.when(j < act_from)
    def _():
        out_refs[0][...] = y.astype(out_refs[0].dtype)


def _forget_gate_epilogue(j, z, extra_refs, out_refs, *, layer):
    logits = extra_refs[0][...]
    lmax = jnp.max(logits, axis=0, keepdims=True)
    ex = jnp.exp(logits - lmax)
    prob = ex / jnp.sum(ex, axis=0, keepdims=True)
    lb = jnp.sum(prob[:layer + 1], axis=0, keepdims=True) - prob[0:1]
    one_m_lb = 1.0 - lb
    t = jnp.exp(-jnp.abs(z))
    sig_big = 1.0 / (1.0 + t)
    sig_small = t * sig_big
    sig_z = jnp.where(z >= 0, sig_big, sig_small)
    sig_nz = jnp.where(z >= 0, sig_small, sig_big)
    out_refs[0][...] = jnp.log2(lb + one_m_lb * sig_z)
    out_refs[1][...] = (one_m_lb * sig_nz).astype(out_refs[1].dtype)


def _wres_matmul(name, a, w, layer, n_tiles, out_dtypes, epilogue=_store_epilogue, w_tile=lambda j: j,
                 o_tile=lambda j: j, tiled_extra=(), col_extra=(), tm=1024, tn=512):
    m, k = a.shape
    out_block = pl.BlockSpec((tm, tn), lambda j, i: (i, o_tile(j)))
    in_specs = [pl.BlockSpec((tm, k), lambda j, i: (i, 0)),
                pl.BlockSpec((None, k, tn), lambda j, i: (layer, 0, w_tile(j)))]
    in_specs += [out_block for _ in tiled_extra]
    in_specs += [pl.BlockSpec((e.shape[0], tn), lambda j, i: (0, o_tile(j))) for e in col_extra]
    outs = pl.pallas_call(
        functools.partial(_wres_matmul_kernel, n_extra=len(tiled_extra) + len(col_extra), epilogue=epilogue),
        grid=(n_tiles, m // tm),
        in_specs=in_specs,
        out_specs=[out_block for _ in out_dtypes],
        out_shape=[jax.ShapeDtypeStruct((m, n_tiles * tn), dt) for dt in out_dtypes],
        scratch_shapes=[pltpu.VMEM((k, tn), BF16)],
        compiler_params=_params(("parallel", "arbitrary"), 56),
        name=name,
    )(a, w, *tiled_extra, *col_extra)
    return outs if len(outs) > 1 else outs[0]


def _wpipe_matmul_kernel(*refs, n_extra, epilogue):
    a_ref, w_ref = refs[:2]
    extra_refs = refs[2:2 + n_extra]
    out_refs = refs[2 + n_extra:-2]
    p = pl.program_id(0)
    slab = w_ref.shape[0]
    r0 = pl.multiple_of(pl.program_id(1) * slab, slab)

    def step(w_next_ref, w_cur_ref, real):
        w_next_ref[pl.ds(r0, slab), :] = w_ref[...].astype(BF16)
        if real:
            epilogue(p - 1, _dot(a_ref[...], w_cur_ref[...]), extra_refs, out_refs)
        else:
            for o_ref in out_refs:
                o_ref[...] = jnp.zeros_like(o_ref)

    even = (p % 2) == 0
    pl.when((p > 0) & even)(functools.partial(step, refs[-2], refs[-1], True))
    pl.when((p > 0) & jnp.logical_not(even))(functools.partial(step, refs[-1], refs[-2], True))
    pl.when(p == 0)(functools.partial(step, refs[-2], refs[-1], False))


def _wpipe_matmul(name, a, w, layer, n_tiles, out_dtypes, epilogue=_store_epilogue, w_tile=lambda j: j,
                  o_tile=lambda j: j, col_extra=(), tm=2048, tn=512):
    m, k = a.shape
    n_i = m // tm
    next_tile = lambda p: w_tile(jnp.minimum(p, n_tiles - 1))
    out_tile = lambda p: jnp.where(p == 0, n_tiles, o_tile(jnp.maximum(p - 1, 0)))
    out_block = pl.BlockSpec((tm, tn), lambda p, i: (i, out_tile(p)))
    in_specs = [pl.BlockSpec((tm, k), lambda p, i: (jnp.where(p == 0, 0, i), 0)),
                pl.BlockSpec((None, k // n_i, tn), lambda p, i: (layer, i, next_tile(p)))]
    in_specs += [pl.BlockSpec((e.shape[0], tn), lambda p, i: (0, o_tile(jnp.maximum(p - 1, 0)))) for e in col_extra]
    wb = pltpu.VMEM((k, tn), BF16)
    outs = pl.pallas_call(
        functools.partial(_wpipe_matmul_kernel, n_extra=len(col_extra), epilogue=epilogue),
        grid=(n_tiles + 1, n_i),
        in_specs=in_specs,
        out_specs=[out_block for _ in out_dtypes],
        out_shape=[jax.ShapeDtypeStruct((m, (n_tiles + 1) * tn), dt) for dt in out_dtypes],
        scratch_shapes=[wb, wb],
        compiler_params=_params(("arbitrary", "arbitrary"), 60),
        name=name,
    )(a, w, *col_extra)
    return outs if len(outs) > 1 else outs[0]


def _ffn_up_kernel(a_ref, w1_ref, w3_ref, w2_ref, h_ref, w2b_ref, w1e_ref, w3e_ref, w1o_ref, w3o_ref, *, n_tiles):
    p = pl.program_id(0)
    i = pl.program_id(1)
    t = p - 1
    slab = w1_ref.shape[0]
    r0 = pl.multiple_of(i * slab, slab)
    slab2 = w2_ref.shape[0]
    q0 = pl.multiple_of(i * slab2, slab2)

    def step(w1_next_ref, w3_next_ref, w1_cur_ref, w3_cur_ref, real):
        w1_next_ref[pl.ds(r0, slab), :] = w1_ref[...].astype(BF16)
        w3_next_ref[pl.ds(r0, slab), :] = w3_ref[...].astype(BF16)
        if real:
            a = a_ref[...]
            y1 = _dot(a, w1_cur_ref[...])
            y3 = _dot(a, w3_cur_ref[...])
            h_ref[...] = (0.5 * (y1 * jax.nn.sigmoid(y1)) * y3).astype(h_ref.dtype)
            w2b_ref[pl.ds(q0, slab2), :] = w2_ref[...].astype(BF16)
        else:
            h_ref[...] = jnp.zeros_like(h_ref)
            w2b_ref[pl.ds(q0, slab2), :] = jnp.zeros((slab2, w2b_ref.shape[1]), BF16)

    real = (t >= 0) & (t < n_tiles)
    even = (p % 2) == 0
    pl.when(real & even)(functools.partial(step, w1e_ref, w3e_ref, w1o_ref, w3o_ref, True))
    pl.when(real & jnp.logical_not(even))(functools.partial(step, w1o_ref, w3o_ref, w1e_ref, w3e_ref, True))
    pl.when(jnp.logical_not(real))(functools.partial(step, w1e_ref, w3e_ref, w1o_ref, w3o_ref, False))


def _ffn_up(a, w1, w3, w2, layer, tm=2048):
    m, k = a.shape
    tf = FF_TILE
    n_tiles = D_FF // tf
    assert n_tiles % 2 == 1
    n_i = m // tm
    n_pos = D_FF_PAD // tf + 1
    next_tile = lambda p: jnp.minimum(p, n_tiles - 1)
    cur_tile = lambda p: jnp.clip(p - 1, 0, n_tiles - 1)
    h_tile = lambda p: jnp.where(p == 0, n_pos - 1, p - 1)
    w2b_tile = lambda p: jnp.maximum(p - 1, 0)
    wb = pltpu.VMEM((k, tf), BF16)
    return pl.pallas_call(
        functools.partial(_ffn_up_kernel, n_tiles=n_tiles),
        grid=(n_pos, n_i),
        in_specs=[pl.BlockSpec((tm, k), lambda p, i: (jnp.where(p == 0, 0, jnp.where(p == n_pos - 1, n_i - 1, i)), 0)),
                  pl.BlockSpec((None, k // n_i, tf), lambda p, i: (layer, i, next_tile(p))),
                  pl.BlockSpec((None, k // n_i, tf), lambda p, i: (layer, i, next_tile(p))),
                  pl.BlockSpec((None, tf // n_i, k), lambda p, i: (layer, cur_tile(p) * n_i + i, 0))],
        out_specs=[pl.BlockSpec((tm, tf), lambda p, i: (i, h_tile(p))),
                   pl.BlockSpec((tf, k), lambda p, i: (w2b_tile(p), 0))],
        out_shape=[jax.ShapeDtypeStruct((m, n_pos * tf), BF16),
                   jax.ShapeDtypeStruct((D_FF_PAD, k), BF16)],
        scratch_shapes=[wb, wb, wb, wb],
        compiler_params=_params(("arbitrary", "arbitrary"), 60),
        name="ffn_up",
    )(a, w1, w3, w2)


def _matmul_res_kernel(a_ref, w_ref, r_ref, o_ref):
    base = jnp.where(pl.program_id(2) == 0, r_ref[...], o_ref[...])
    o_ref[...] = base + _dot(a_ref[...], w_ref[...])


def _matmul_res(a, w, res, tk, tm=1024, tn=1024):
    m = a.shape[0]
    k, n = w.shape
    return pl.pallas_call(
        _matmul_res_kernel,
        grid=(m // tm, n // tn, k // tk),
        in_specs=[pl.BlockSpec((tm, tk), lambda i, j, kk: (i, kk)),
                  pl.BlockSpec((tk, tn), lambda i, j, kk: (kk, j)),
                  pl.BlockSpec((tm, tn), lambda i, j, kk: (i, j))],
        out_specs=pl.BlockSpec((tm, tn), lambda i, j, kk: (i, j)),
        out_shape=jax.ShapeDtypeStruct((m, n), F32),
        compiler_params=_params(("parallel", "parallel", "arbitrary"), 56),
        name="matmul_res",
    )(a, w, res)


def _ab_mixer_kernel(sinks_ref, q_ref, u_ref, vg_ref, kvp_ref, kvc_ref, qg_ref, kg_ref,
                     lng_ref, lnb_ref, ws_ref, bs_ref, o_ref):
    n = pl.program_id(1)
    w = WINDOW
    hd = A_HEAD_DIM

    key = lax.broadcasted_iota(jnp.int32, (2 * w, w), 0)
    qry = lax.broadcasted_iota(jnp.int32, (2 * w, w), 1)
    dist = qry + w - key
    visible = (dist >= 0) & (dist < w) & ((key >= w) | (n > 0))
    low = lax.broadcasted_iota(jnp.int32, (1, LANES), 1) < hd
    sub8 = lax.broadcasted_iota(jnp.int32, (SUBLANES, LANES), 0)
    ones_low = jnp.where(low, 1.0, 0.0)
    ones_high = 1.0 - ones_low
    sel2 = jnp.where(sub8 < SUBLANES // 2, ones_low, ones_high).astype(BF16)
    sink_row = sub8 == 0

    def split_bf16(x):
        hi = x.astype(BF16)
        return hi, (x - hi.astype(F32)).astype(BF16)

    kk = jnp.concatenate([kvp_ref[:, :A_KV_WIDTH], kvc_ref[:, :A_KV_WIDTH]], axis=0)
    vv = jnp.concatenate([kvp_ref[:, A_KV_WIDTH:], kvc_ref[:, A_KV_WIDTH:]], axis=0)
    seg_r = lax.broadcasted_iota(jnp.int32, (A_KV_WIDTH, A_KV_WIDTH), 0) // hd
    seg_c = lax.broadcasted_iota(jnp.int32, (A_KV_WIDTH, A_KV_WIDTH), 1) // hd
    seg = (seg_r == seg_c).astype(BF16)
    hi, lo = split_bf16(kk * kk)
    kss = _dot(hi, seg) + _dot(lo, seg)
    kgain = jnp.concatenate([kg_ref[...]] * A_KV_HEADS, axis=1)
    khat = kk * lax.rsqrt(kss * (1.0 / hd) + EPS) * kgain
    qgain = jnp.concatenate([qg_ref[...]] * 2, axis=1) * (hd ** -0.5 * LOG2E)

    def head_pair_operands(x, h):
        tile = x[:, (h // 2) * LANES:(h // 2 + 1) * LANES]
        swapped = pltpu.roll(tile, hd, axis=1)
        in_low, in_high = (tile, swapped) if h % 2 == 0 else (swapped, tile)
        return jnp.where(low, in_low, 0.0), jnp.where(low, 0.0, in_high)

    for h in range(A_KV_HEADS):
        k_low, k_high = head_pair_operands(khat, h)
        v_low, v_high = head_pair_operands(vv, h)
        k_pad = (k_low.astype(BF16), k_high.astype(BF16))
        v_aug = tuple(
            jnp.concatenate([
                jnp.concatenate([v_p, jnp.broadcast_to(ones_p, v_p.shape)], axis=1),
                jnp.concatenate([jnp.zeros((SUBLANES, LANES), F32), jnp.broadcast_to(ones_p, (SUBLANES, LANES))], axis=1),
            ], axis=0).astype(BF16)
            for v_p, ones_p in ((v_low, ones_low), (v_high, ones_high)))
        for c in range(h * A_GROUP // 2, (h + 1) * A_GROUP // 2):
            qc = q_ref[:, c * LANES:(c + 1) * LANES]
            hi, lo = split_bf16(qc * qc)
            qss = (lax.dot_general(sel2, hi, NT_DIMS, preferred_element_type=F32)
                   + lax.dot_general(sel2, lo, NT_DIMS, preferred_element_type=F32))
            qscale = lax.rsqrt(qss * (1.0 / hd) + EPS)
            qg = (qc * qgain).astype(BF16)
            acc = jnp.zeros((w, 2 * LANES), F32)
            for parity in range(2):
                st = lax.dot_general(k_pad[parity], qg, NT_DIMS, preferred_element_type=F32)
                r = qscale[parity * (SUBLANES // 2):parity * (SUBLANES // 2) + 1]
                st = jnp.where(visible, st * r, -jnp.inf)
                s_sink = sinks_ref[2 * c + parity] * LOG2E
                mx = jnp.maximum(jnp.max(st, axis=0, keepdims=True), s_sink)
                e = jnp.exp2(st - mx)
                e_sink = jnp.where(sink_row, jnp.exp2(s_sink - mx), 0.0)
                e_aug = jnp.concatenate([e, e_sink], axis=0).astype(BF16)
                acc = acc + lax.dot_general(e_aug, v_aug[parity], TN_DIMS, preferred_element_type=F32)
            o_ref[:, c * LANES:(c + 1) * LANES] = (acc[:, :LANES] / acc[:, LANES:]).astype(o_ref.dtype)

    row = lax.broadcasted_iota(jnp.int32, (w, w), 0)
    col = lax.broadcasted_iota(jnp.int32, (w, w), 1)
    causal = row >= col
    for g in range(B_GROUPS):
        sl = slice(g * B_GROUP_DIM, (g + 1) * B_GROUP_DIM)
        x = vg_ref[:, sl]
        mu = jnp.mean(x, axis=-1, keepdims=True)
        d = x - mu
        var = jnp.mean(d * d, axis=-1, keepdims=True)
        vn = d * lax.rsqrt(var + EPS) * lng_ref[:, sl] + lnb_ref[:, sl]
        wg = jnp.where(causal, ws_ref[g], 0.0).astype(BF16)
        sg = _dot(wg, vn.astype(BF16)) + bs_ref[:, g:g + 1]
        o_ref[:, A_WIDTH + g * B_GROUP_DIM:A_WIDTH + (g + 1) * B_GROUP_DIM] = (
            u_ref[:, sl] * sg).astype(o_ref.dtype)


def _ab_mixer(proj, batch, seq, q_gain, k_gain, sinks, ln_g, ln_b, w_s, b_s):
    nb = seq // WINDOW
    kv_blk = (A_WIDTH + 2 * B_WIDTH) // (2 * A_KV_WIDTH)
    row_blk = lambda b, n: b * nb + n
    return pl.pallas_call(
        _ab_mixer_kernel,
        grid=(batch, nb),
        in_specs=[
            pl.BlockSpec(memory_space=pltpu.SMEM),
            pl.BlockSpec((WINDOW, A_WIDTH), lambda b, n: (row_blk(b, n), 0)),
            pl.BlockSpec((WINDOW, B_WIDTH), lambda b, n: (row_blk(b, n), 1)),
            pl.BlockSpec((WINDOW, B_WIDTH), lambda b, n: (row_blk(b, n), 2)),
            pl.BlockSpec((WINDOW, 2 * A_KV_WIDTH), lambda b, n: (b * nb + jnp.maximum(n - 1, 0), kv_blk)),
            pl.BlockSpec((WINDOW, 2 * A_KV_WIDTH), lambda b, n: (row_blk(b, n), kv_blk)),
            pl.BlockSpec((1, A_HEAD_DIM), lambda b, n: (0, 0)),
            pl.BlockSpec((1, A_HEAD_DIM), lambda b, n: (0, 0)),
            pl.BlockSpec((1, B_WIDTH), lambda b, n: (0, 0)),
            pl.BlockSpec((1, B_WIDTH), lambda b, n: (0, 0)),
            pl.BlockSpec((B_GROUPS, WINDOW, WINDOW), lambda b, n: (0, 0, 0)),
            pl.BlockSpec((WINDOW, B_GROUPS), lambda b, n: (0, 0)),
        ],
        out_specs=pl.BlockSpec((WINDOW, A_WIDTH + B_WIDTH), lambda b, n: (row_blk(b, n), 0)),
        out_shape=jax.ShapeDtypeStruct((batch * seq, A_WIDTH + B_WIDTH), BF16),
        compiler_params=_params(("parallel", "arbitrary"), 32),
        name="ab_mixer",
    )(sinks, proj, proj, proj, proj, proj,
      q_gain.reshape(1, A_HEAD_DIM), k_gain.reshape(1, A_HEAD_DIM),
      ln_g.reshape(1, B_WIDTH), ln_b.reshape(1, B_WIDTH), w_s, b_s.T)


def _hgrn2_kernel(og_ref, q_ref, v_ref, gate_ref, lf_ref, k_ref, o_ref, st_ref):
    c = C_CHUNK

    @pl.when(pl.program_id(2) == 0)
    def _():
        st_ref[...] = jnp.zeros_like(st_ref)

    row = lax.broadcasted_iota(jnp.int32, (c, c), 0)
    col = lax.broadcasted_iota(jnp.int32, (c, c), 1)
    row1 = lax.broadcasted_iota(jnp.int32, (c, 1), 0)
    cumsum_mat = (row >= col).astype(BF16)
    diff_bits = row ^ col
    level = jnp.zeros((c, c), jnp.int32)
    for b in range(1, c.bit_length() - 1):
        level = level + (diff_bits >= (1 << b)).astype(jnp.int32)
    level = jnp.where(col < row, level, jnp.where(col == row, -1, -2))
    og = og_ref[...]

    for h in range(C_HEADS_PER_STEP):
        sl = slice(h * C_DIM, (h + 1) * C_DIM)
        log2_f = lf_ref[:, sl]
        k = k_ref[:, sl].astype(F32)

        p1 = log2_f.astype(BF16)
        p2 = (log2_f - p1.astype(F32)).astype(BF16)
        cg = _dot(cumsum_mat, p1) + _dot(cumsum_mat, p2)

        q = q_ref[:, sl].astype(F32)
        v = v_ref[:, sl]
        st = st_ref[h]
        o = lax.dot_general((q * jnp.exp2(cg)).astype(BF16), st.astype(BF16), NT_DIMS,
                            preferred_element_type=F32)

        att = jnp.where(level == -1, jnp.sum(q * k, axis=-1, keepdims=True), 0.0)
        part = lax.dot_general((q * jnp.exp2(log2_f)).astype(BF16), k_ref[:, sl], NT_DIMS,
                               preferred_element_type=F32)
        att = jnp.where(level == 0, part, att)
        cg3 = cg.reshape(c // SUBLANES, SUBLANES, C_DIM)
        sub_row = lambda r: jnp.broadcast_to(cg3[:, r:r + 1, :], cg3.shape).reshape(c, C_DIM)
        n = 2
        while n < c:
            if 4 * n == SUBLANES:
                ref = jnp.where((row1 & (2 * n)) != 0, sub_row(3 * n - 1), sub_row(n - 1))
            elif 2 * n == SUBLANES:
                ref = sub_row(n - 1)
            else:
                ref = jnp.concatenate([jnp.broadcast_to(cg[lo + n - 1:lo + n, :], (2 * n, C_DIM))
                                       for lo in range(0, c, 2 * n)], axis=0)
            e = jnp.exp2(-jnp.abs(cg - ref))
            part = lax.dot_general((q * e).astype(BF16), (k * e).astype(BF16), NT_DIMS,
                                   preferred_element_type=F32)
            att = jnp.where(level == n.bit_length() - 1, part, att)
            n *= 2

        o = o + _dot(att.astype(BF16), v)

        last = cg[c - 1:c, :]
        kd = (k * jnp.exp2(last - cg)).astype(BF16)
        st_ref[h] = st * jnp.exp2(last) + lax.dot_general(v, kd, TN_DIMS, preferred_element_type=F32)

        y = o * lax.rsqrt(jnp.mean(o * o, axis=-1, keepdims=True) + EPS) * og
        o_ref[:, sl] = (y * gate_ref[:, sl].astype(F32)).astype(o_ref.dtype)


def _hgrn2(qvg, log2_f, k, o_gain, batch, seq):
    nc = seq // C_CHUNK
    hw = C_HEADS_PER_STEP * C_DIM
    nh = C_WIDTH // hw
    blk = lambda first: pl.BlockSpec((C_CHUNK, hw), lambda b, g, c: (b * nc + c, first + g))
    return pl.pallas_call(
        _hgrn2_kernel,
        grid=(batch, nh, nc),
        in_specs=[pl.BlockSpec((1, C_DIM), lambda b, g, c: (0, 0)),
                  blk(0), blk(nh), blk(2 * nh), blk(0), blk(0)],
        out_specs=blk(0),
        out_shape=jax.ShapeDtypeStruct((batch * seq, C_WIDTH), BF16),
        scratch_shapes=[pltpu.VMEM((C_HEADS_PER_STEP, C_DIM, C_DIM), F32)],
        compiler_params=_params(("parallel", "parallel", "arbitrary"), 32),
        name="hgrn2",
    )(o_gain.reshape(1, C_DIM), qvg, qvg, qvg, log2_f, k)


def _ffn(x, norm_g, w1, w3, w2, layer):
    h, w2b = _ffn_up(_rmsnorm(x, norm_g[layer]), w1, w3, w2, layer)
    return _matmul_res(h, w2b, x, tk=D_FF_PAD // 4)


def kernel(x, ffn1_norm, ffn1_w1, ffn1_w3, ffn1_w2, mix_norm, ffn2_norm, ffn2_w1, ffn2_w3, ffn2_w2, ab_w_in, ab_q_norm, ab_k_norm, ab_sinks, ab_v_ln_g, ab_v_ln_b, ab_w_s, ab_b_s, ab_w_out, c_w_in, c_lb_logits, c_o_norm, c_w_out):
    batch, seq, d = x.shape
    depth = ffn1_norm.shape[0]
    tn = 512
    x = x.reshape(batch * seq, d)
    for l in range(depth):
        x = _ffn(x, ffn1_norm, ffn1_w1, ffn1_w3, ffn1_w2, l)
        hn = _rmsnorm(x, mix_norm[l])
        j = l // 2
        if l % 2 == 0:
            kv_tile = A_WIDTH // tn
            n_tiles = AB_IN // tn
            o_tile = lambda t: jnp.where(t < kv_tile, t, jnp.where(t == kv_tile, n_tiles - 1, t - 1))
            proj = _wpipe_matmul("ab_in_proj", hn, ab_w_in, j, n_tiles, [F32], o_tile=o_tile,
                                 epilogue=functools.partial(_act_epilogue, act=_gelu, act_from=kv_tile + 1))
            mixed = _ab_mixer(proj, batch, seq, ab_q_norm[j], ab_k_norm[j], ab_sinks[j],
                              ab_v_ln_g[j], ab_v_ln_b[j], ab_w_s[j], ab_b_s[j])
            x = _wres_matmul("ab_out_proj", mixed, ab_w_out, j, d // tn, [F32], _residual_epilogue, tiled_extra=[x])
        else:
            ct = C_WIDTH // tn
            qvg = _wpipe_matmul("c_in_proj_qvg", hn, c_w_in, j, 3 * ct, [BF16],
                                functools.partial(_act_epilogue, act=_silu, act_from=2 * ct),
                                w_tile=lambda t: jnp.where(t < ct, t, t + ct))
            log2_f, k = _wres_matmul("c_in_proj_f", hn, c_w_in, j, ct, [F32, BF16],
                                     functools.partial(_forget_gate_epilogue, layer=l),
                                     w_tile=lambda t: t + ct, col_extra=[c_lb_logits])
            mixed = _hgrn2(qvg, log2_f, k, c_o_norm[j], batch, seq)
            x = _wres_matmul("c_out_proj", mixed, c_w_out, j, d // tn, [F32], _residual_epilogue, tiled_extra=[x])
        x = _ffn(x, ffn2_norm, ffn2_w1, ffn2_w3, ffn2_w2, l)
    return x.reshape(batch, seq, d)
```

```python
import functools

import jax
import jax.numpy as jnp
from jax import lax
from jax.experimental import pallas as pl
from jax.experimental.pallas import tpu as pltpu

F32 = jnp.float32
BF16 = jnp.bfloat16

D_MODEL = 4096
D_FF = 11008
FF_TILE = 256
D_FF_PAD = 11264
EPS = 1e-6

A_HEADS = 32
A_KV_HEADS = 4
A_GROUP = A_HEADS // A_KV_HEADS
A_HEAD_DIM = 64
WINDOW = 128
A_WIDTH = A_HEADS * A_HEAD_DIM
A_KV_WIDTH = A_KV_HEADS * A_HEAD_DIM
B_GROUPS = 16
B_GROUP_DIM = 128
B_WIDTH = B_GROUPS * B_GROUP_DIM
AB_IN = A_WIDTH + 2 * A_KV_WIDTH + 2 * B_WIDTH

C_HEADS = 32
C_DIM = 128
C_WIDTH = C_HEADS * C_DIM
C_CHUNK = 128
C_HEADS_PER_STEP = 16

LANES = 128
SUBLANES = 8
MIB = 1 << 20
LOG2E = 1.4426950408889634

NT_DIMS = (((1,), (1,)), ((), ()))
TN_DIMS = (((0,), (0,)), ((), ()))


def _params(semantics, vmem_mib):
    return pltpu.CompilerParams(dimension_semantics=semantics, vmem_limit_bytes=vmem_mib * MIB)


def _dot(a, b):
    return jnp.dot(a, b, preferred_element_type=F32)


def _gelu(y):
    return 0.5 * y * (1.0 + lax.erf(y * (2.0 ** -0.5)))


def _silu(y):
    half = 0.5 * y
    return half + half * jnp.tanh(half)


def _rmsnorm_kernel(x_ref, g_ref, o_ref):
    x = x_ref[...]
    r = lax.rsqrt(jnp.mean(x * x, axis=-1, keepdims=True) + EPS)
    o_ref[...] = (x * r * g_ref[...]).astype(o_ref.dtype)


def _rmsnorm(x, g, rows=256):
    m, d = x.shape
    return pl.pallas_call(
        _rmsnorm_kernel,
        grid=(m // rows,),
        in_specs=[pl.BlockSpec((rows, d), lambda i: (i, 0)),
                  pl.BlockSpec((1, d), lambda i: (0, 0))],
        out_specs=pl.BlockSpec((rows, d), lambda i: (i, 0)),
        out_shape=jax.ShapeDtypeStruct((m, d), BF16),
        compiler_params=_params(("parallel",), 32),
        name="rmsnorm",
    )(x, g.reshape(1, d))


def _wres_matmul_kernel(*refs, n_extra, epilogue):
    a_ref, w_ref = refs[:2]
    extra_refs = refs[2:2 + n_extra]
    out_refs = refs[2 + n_extra:-1]
    wb_ref = refs[-1]

    @pl.when(pl.program_id(1) == 0)
    def _():
        wb_ref[...] = w_ref[...].astype(BF16)

    epilogue(pl.program_id(0), _dot(a_ref[...], wb_ref[...]), extra_refs, out_refs)


def _store_epilogue(j, y, extra_refs, out_refs):
    out_refs[0][...] = y.astype(out_refs[0].dtype)


def _residual_epilogue(j, y, extra_refs, out_refs):
    out_refs[0][...] = extra_refs[0][...] + y


def _act_epilogue(j, y, extra_refs, out_refs, *, act, act_from):
    @pl.when(j >= act_from)
    def _():
        out_refs[0][...] = act(y).astype(out_refs[0].dtype)

    @pl.when(j < act_from)
    def _():
        out_refs[0][...] = y.astype(out_refs[0].dtype)


def _forget_gate_epilogue(j, z, extra_refs, out_refs, *, layer):
    logits = extra_refs[0][...]
    lmax = jnp.max(logits, axis=0, keepdims=True)
    ex = jnp.exp(logits - lmax)
    prob = ex / jnp.sum(ex, axis=0, keepdims=True)
    lb = jnp.sum(prob[:layer + 1], axis=0, keepdims=True) - prob[0:1]
    c1 = 0.5 * (1.0 - lb)
    c0 = lb + c1
    scaled = c1 * jnp.tanh(0.5 * z)
    out_refs[0][...] = jnp.log2(c0 + scaled)
    out_refs[1][...] = (c1 - scaled).astype(out_refs[1].dtype)


def _wres_matmul(name, a, w, layer, n_tiles, out_dtypes, epilogue=_store_epilogue, w_tile=lambda j: j,
                 o_tile=lambda j: j, tiled_extra=(), col_extra=(), tm=1024, tn=512):
    m, k = a.shape
    out_block = pl.BlockSpec((tm, tn), lambda j, i: (i, o_tile(j)))
    in_specs = [pl.BlockSpec((tm, k), lambda j, i: (i, 0)),
                pl.BlockSpec((None, k, tn), lambda j, i: (layer, 0, w_tile(j)))]
    in_specs += [out_block for _ in tiled_extra]
    in_specs += [pl.BlockSpec((e.shape[0], tn), lambda j, i: (0, o_tile(j))) for e in col_extra]
    outs = pl.pallas_call(
        functools.partial(_wres_matmul_kernel, n_extra=len(tiled_extra) + len(col_extra), epilogue=epilogue),
        grid=(n_tiles, m // tm),
        in_specs=in_specs,
        out_specs=[out_block for _ in out_dtypes],
        out_shape=[jax.ShapeDtypeStruct((m, n_tiles * tn), dt) for dt in out_dtypes],
        scratch_shapes=[pltpu.VMEM((k, tn), BF16)],
        compiler_params=_params(("parallel", "arbitrary"), 56),
        name=name,
    )(a, w, *tiled_extra, *col_extra)
    return outs if len(outs) > 1 else outs[0]


def _wpipe_matmul_kernel(*refs, n_extra, epilogue):
    a_ref, w_ref = refs[:2]
    extra_refs = refs[2:2 + n_extra]
    out_refs = refs[2 + n_extra:-2]
    p = pl.program_id(0)
    slab = w_ref.shape[0]
    r0 = pl.multiple_of(pl.program_id(1) * slab, slab)

    def step(w_next_ref, w_cur_ref, real):
        w_next_ref[pl.ds(r0, slab), :] = w_ref[...].astype(BF16)
        if real:
            epilogue(p - 1, _dot(a_ref[...], w_cur_ref[...]), extra_refs, out_refs)
        else:
            for o_ref in out_refs:
                o_ref[...] = jnp.zeros_like(o_ref)

    even = (p % 2) == 0
    pl.when((p > 0) & even)(functools.partial(step, refs[-2], refs[-1], True))
    pl.when((p > 0) & jnp.logical_not(even))(functools.partial(step, refs[-1], refs[-2], True))
    pl.when(p == 0)(functools.partial(step, refs[-2], refs[-1], False))


def _wpipe_matmul(name, a, w, layer, n_tiles, out_dtypes, epilogue=_store_epilogue, w_tile=lambda j: j,
                  o_tile=lambda j: j, col_extra=(), tm=2048, tn=512):
    m, k = a.shape
    n_i = m // tm
    next_tile = lambda p: w_tile(jnp.minimum(p, n_tiles - 1))
    out_tile = lambda p: jnp.where(p == 0, n_tiles, o_tile(jnp.maximum(p - 1, 0)))
    out_block = pl.BlockSpec((tm, tn), lambda p, i: (i, out_tile(p)))
    in_specs = [pl.BlockSpec((tm, k), lambda p, i: (jnp.where(p == 0, 0, i), 0)),
                pl.BlockSpec((None, k // n_i, tn), lambda p, i: (layer, i, next_tile(p)))]
    in_specs += [pl.BlockSpec((e.shape[0], tn), lambda p, i: (0, o_tile(jnp.maximum(p - 1, 0)))) for e in col_extra]
    wb = pltpu.VMEM((k, tn), BF16)
    outs = pl.pallas_call(
        functools.partial(_wpipe_matmul_kernel, n_extra=len(col_extra), epilogue=epilogue),
        grid=(n_tiles + 1, n_i),
        in_specs=in_specs,
        out_specs=[out_block for _ in out_dtypes],
        out_shape=[jax.ShapeDtypeStruct((m, (n_tiles + 1) * tn), dt) for dt in out_dtypes],
        scratch_shapes=[wb, wb],
        compiler_params=_params(("arbitrary", "arbitrary"), 60),
        name=name,
    )(a, w, *col_extra)
    return outs if len(outs) > 1 else outs[0]


def _ffn_up_kernel(a_ref, w1_ref, w3_ref, w2_ref, h_ref, w2b_ref, w1e_ref, w3e_ref, w1o_ref, w3o_ref, *, n_tiles):
    p = pl.program_id(0)
    i = pl.program_id(1)
    t = p - 1
    slab = w1_ref.shape[0]
    r0 = pl.multiple_of(i * slab, slab)
    slab2 = w2_ref.shape[0]
    q0 = pl.multiple_of(i * slab2, slab2)

    def step(w1_next_ref, w3_next_ref, w1_cur_ref, w3_cur_ref, real):
        w1_next_ref[pl.ds(r0, slab), :] = w1_ref[...].astype(BF16)
        w3_next_ref[pl.ds(r0, slab), :] = w3_ref[...].astype(BF16)
        if real:
            a = a_ref[...]
            y1 = _dot(a, w1_cur_ref[...])
            y3 = _dot(a, w3_cur_ref[...])
            h_ref[...] = (_silu(y1) * (0.5 * y3)).astype(h_ref.dtype)
            w2b_ref[pl.ds(q0, slab2), :] = w2_ref[...].astype(BF16)
        else:
            h_ref[...] = jnp.zeros_like(h_ref)
            w2b_ref[pl.ds(q0, slab2), :] = jnp.zeros((slab2, w2b_ref.shape[1]), BF16)

    real = (t >= 0) & (t < n_tiles)
    even = (p % 2) == 0
    pl.when(real & even)(functools.partial(step, w1e_ref, w3e_ref, w1o_ref, w3o_ref, True))
    pl.when(real & jnp.logical_not(even))(functools.partial(step, w1o_ref, w3o_ref, w1e_ref, w3e_ref, True))
    pl.when(jnp.logical_not(real))(functools.partial(step, w1e_ref, w3e_ref, w1o_ref, w3o_ref, False))


def _ffn_up(a, w1, w3, w2, layer, tm=2048):
    m, k = a.shape
    tf = FF_TILE
    n_tiles = D_FF // tf
    assert n_tiles % 2 == 1
    n_i = m // tm
    n_pos = D_FF_PAD // tf + 1
    next_tile = lambda p: jnp.minimum(p, n_tiles - 1)
    cur_tile = lambda p: jnp.clip(p - 1, 0, n_tiles - 1)
    h_tile = lambda p: jnp.where(p == 0, n_pos - 1, p - 1)
    w2b_tile = lambda p: jnp.maximum(p - 1, 0)
    wb = pltpu.VMEM((k, tf), BF16)
    return pl.pallas_call(
        functools.partial(_ffn_up_kernel, n_tiles=n_tiles),
        grid=(n_pos, n_i),
        in_specs=[pl.BlockSpec((tm, k), lambda p, i: (jnp.where(p == 0, 0, jnp.where(p == n_pos - 1, n_i - 1, i)), 0)),
                  pl.BlockSpec((None, k // n_i, tf), lambda p, i: (layer, i, next_tile(p))),
                  pl.BlockSpec((None, k // n_i, tf), lambda p, i: (layer, i, next_tile(p))),
                  pl.BlockSpec((None, tf // n_i, k), lambda p, i: (layer, cur_tile(p) * n_i + i, 0))],
        out_specs=[pl.BlockSpec((tm, tf), lambda p, i: (i, h_tile(p))),
                   pl.BlockSpec((tf, k), lambda p, i: (w2b_tile(p), 0))],
        out_shape=[jax.ShapeDtypeStruct((m, n_pos * tf), BF16),
                   jax.ShapeDtypeStruct((D_FF_PAD, k), BF16)],
        scratch_shapes=[wb, wb, wb, wb],
        compiler_params=_params(("arbitrary", "arbitrary"), 60),
        name="ffn_up",
    )(a, w1, w3, w2)


def _matmul_res_kernel(a_ref, w_ref, r_ref, o_ref):
    base = jnp.where(pl.program_id(2) == 0, r_ref[...], o_ref[...])
    o_ref[...] = base + _dot(a_ref[...], w_ref[...])


def _matmul_res(a, w, res, tk, tm=1024, tn=1024):
    m = a.shape[0]
    k, n = w.shape
    return pl.pallas_call(
        _matmul_res_kernel,
        grid=(m // tm, n // tn, k // tk),
        in_specs=[pl.BlockSpec((tm, tk), lambda i, j, kk: (i, kk)),
                  pl.BlockSpec((tk, tn), lambda i, j, kk: (kk, j)),
                  pl.BlockSpec((tm, tn), lambda i, j, kk: (i, j))],
        out_specs=pl.BlockSpec((tm, tn), lambda i, j, kk: (i, j)),
        out_shape=jax.ShapeDtypeStruct((m, n), F32),
        compiler_params=_params(("parallel", "parallel", "arbitrary"), 56),
        name="matmul_res",
    )(a, w, res)


def _ab_mixer_kernel(sinks_ref, q_ref, u_ref, vg_ref, kvp_ref, kvc_ref, qg_ref, kg_ref,
                     lng_ref, lnb_ref, ws_ref, bs_ref, o_ref):
    n = pl.program_id(1)
    w = WINDOW
    hd = A_HEAD_DIM

    key = lax.broadcasted_iota(jnp.int32, (2 * w, w), 0)
    qry = lax.broadcasted_iota(jnp.int32, (2 * w, w), 1)
    dist = qry + w - key
    visible = (dist >= 0) & (dist < w) & ((key >= w) | (n > 0))
    low = lax.broadcasted_iota(jnp.int32, (1, LANES), 1) < hd
    sub8 = lax.broadcasted_iota(jnp.int32, (SUBLANES, LANES), 0)
    ones_low = jnp.where(low, 1.0, 0.0)
    ones_high = 1.0 - ones_low
    sel2 = jnp.where(sub8 < SUBLANES // 2, ones_low, ones_high).astype(BF16)
    sink_row = sub8 == 0

    def split_bf16(x):
        hi = x.astype(BF16)
        return hi, (x - hi.astype(F32)).astype(BF16)

    kk = jnp.concatenate([kvp_ref[:, :A_KV_WIDTH], kvc_ref[:, :A_KV_WIDTH]], axis=0)
    vv = jnp.concatenate([kvp_ref[:, A_KV_WIDTH:], kvc_ref[:, A_KV_WIDTH:]], axis=0)
    seg_r = lax.broadcasted_iota(jnp.int32, (A_KV_WIDTH, A_KV_WIDTH), 0) // hd
    seg_c = lax.broadcasted_iota(jnp.int32, (A_KV_WIDTH, A_KV_WIDTH), 1) // hd
    seg = (seg_r == seg_c).astype(BF16)
    hi, lo = split_bf16(kk * kk)
    kss = _dot(hi, seg) + _dot(lo, seg)
    kgain = jnp.concatenate([kg_ref[...]] * A_KV_HEADS, axis=1)
    khat = kk * lax.rsqrt(kss * (1.0 / hd) + EPS) * kgain
    qgain = jnp.concatenate([qg_ref[...]] * 2, axis=1) * (hd ** -0.5 * LOG2E)

    def head_pair_operands(x, h):
        tile = x[:, (h // 2) * LANES:(h // 2 + 1) * LANES]
        swapped = pltpu.roll(tile, hd, axis=1)
        in_low, in_high = (tile, swapped) if h % 2 == 0 else (swapped, tile)
        return jnp.where(low, in_low, 0.0), jnp.where(low, 0.0, in_high)

    for h in range(A_KV_HEADS):
        k_low, k_high = head_pair_operands(khat, h)
        v_low, v_high = head_pair_operands(vv, h)
        k_pad = (k_low.astype(BF16), k_high.astype(BF16))
        v_aug = tuple(
            jnp.concatenate([
                jnp.concatenate([v_p, jnp.broadcast_to(ones_p, v_p.shape)], axis=1),
                jnp.concatenate([jnp.zeros((SUBLANES, LANES), F32), jnp.broadcast_to(ones_p, (SUBLANES, LANES))], axis=1),
            ], axis=0).astype(BF16)
            for v_p, ones_p in ((v_low, ones_low), (v_high, ones_high)))
        for c in range(h * A_GROUP // 2, (h + 1) * A_GROUP // 2):
            qc = q_ref[:, c * LANES:(c + 1) * LANES]
            hi, lo = split_bf16(qc * qc)
            qss = (lax.dot_general(sel2, hi, NT_DIMS, preferred_element_type=F32)
                   + lax.dot_general(sel2, lo, NT_DIMS, preferred_element_type=F32))
            qscale = lax.rsqrt(qss * (1.0 / hd) + EPS)
            qg = (qc * qgain).astype(BF16)
            acc = jnp.zeros((w, 2 * LANES), F32)
            for parity in range(2):
                st = lax.dot_general(k_pad[parity], qg, NT_DIMS, preferred_element_type=F32)
                r = qscale[parity * (SUBLANES // 2):parity * (SUBLANES // 2) + 1]
                st = jnp.where(visible, st * r, -jnp.inf)
                s_sink = sinks_ref[2 * c + parity] * LOG2E
                mx = jnp.maximum(jnp.max(st, axis=0, keepdims=True), s_sink)
                e = jnp.exp2(st - mx)
                e_sink = jnp.where(sink_row, jnp.exp2(s_sink - mx), 0.0)
                e_aug = jnp.concatenate([e, e_sink], axis=0).astype(BF16)
                acc = acc + lax.dot_general(e_aug, v_aug[parity], TN_DIMS, preferred_element_type=F32)
            o_ref[:, c * LANES:(c + 1) * LANES] = (acc[:, :LANES] / acc[:, LANES:]).astype(o_ref.dtype)

    row = lax.broadcasted_iota(jnp.int32, (w, w), 0)
    col = lax.broadcasted_iota(jnp.int32, (w, w), 1)
    causal = row >= col
    for g in range(B_GROUPS):
        sl = slice(g * B_GROUP_DIM, (g + 1) * B_GROUP_DIM)
        x = vg_ref[:, sl]
        mu = jnp.mean(x, axis=-1, keepdims=True)
        d = x - mu
        var = jnp.mean(d * d, axis=-1, keepdims=True)
        vn = d * lax.rsqrt(var + EPS) * lng_ref[:, sl] + lnb_ref[:, sl]
        wg = jnp.where(causal, ws_ref[g], 0.0).astype(BF16)
        sg = _dot(wg, vn.astype(BF16)) + bs_ref[:, g:g + 1]
        o_ref[:, A_WIDTH + g * B_GROUP_DIM:A_WIDTH + (g + 1) * B_GROUP_DIM] = (
            u_ref[:, sl] * sg).astype(o_ref.dtype)


def _ab_mixer(proj, batch, seq, q_gain, k_gain, sinks, ln_g, ln_b, w_s, b_s):
    nb = seq // WINDOW
    kv_blk = (A_WIDTH + 2 * B_WIDTH) // (2 * A_KV_WIDTH)
    row_blk = lambda b, n: b * nb + n
    return pl.pallas_call(
        _ab_mixer_kernel,
        grid=(batch, nb),
        in_specs=[
            pl.BlockSpec(memory_space=pltpu.SMEM),
            pl.BlockSpec((WINDOW, A_WIDTH), lambda b, n: (row_blk(b, n), 0)),
            pl.BlockSpec((WINDOW, B_WIDTH), lambda b, n: (row_blk(b, n), 1)),
            pl.BlockSpec((WINDOW, B_WIDTH), lambda b, n: (row_blk(b, n), 2)),
            pl.BlockSpec((WINDOW, 2 * A_KV_WIDTH), lambda b, n: (b * nb + jnp.maximum(n - 1, 0), kv_blk)),
            pl.BlockSpec((WINDOW, 2 * A_KV_WIDTH), lambda b, n: (row_blk(b, n), kv_blk)),
            pl.BlockSpec((1, A_HEAD_DIM), lambda b, n: (0, 0)),
            pl.BlockSpec((1, A_HEAD_DIM), lambda b, n: (0, 0)),
            pl.BlockSpec((1, B_WIDTH), lambda b, n: (0, 0)),
            pl.BlockSpec((1, B_WIDTH), lambda b, n: (0, 0)),
            pl.BlockSpec((B_GROUPS, WINDOW, WINDOW), lambda b, n: (0, 0, 0)),
            pl.BlockSpec((WINDOW, B_GROUPS), lambda b, n: (0, 0)),
        ],
        out_specs=pl.BlockSpec((WINDOW, A_WIDTH + B_WIDTH), lambda b, n: (row_blk(b, n), 0)),
        out_shape=jax.ShapeDtypeStruct((batch * seq, A_WIDTH + B_WIDTH), BF16),
        compiler_params=_params(("parallel", "arbitrary"), 32),
        name="ab_mixer",
    )(sinks, proj, proj, proj, proj, proj,
      q_gain.reshape(1, A_HEAD_DIM), k_gain.reshape(1, A_HEAD_DIM),
      ln_g.reshape(1, B_WIDTH), ln_b.reshape(1, B_WIDTH), w_s, b_s.T)


def _hgrn2_kernel(og_ref, q_ref, v_ref, gate_ref, lf_ref, k_ref, o_ref, st_ref):
    c = C_CHUNK

    @pl.when(pl.program_id(2) == 0)
    def _():
        st_ref[...] = jnp.zeros_like(st_ref)

    row = lax.broadcasted_iota(jnp.int32, (c, c), 0)
    col = lax.broadcasted_iota(jnp.int32, (c, c), 1)
    row1 = lax.broadcasted_iota(jnp.int32, (c, 1), 0)
    cumsum_mat = (row >= col).astype(BF16)
    diff_bits = row ^ col
    level = jnp.zeros((c, c), jnp.int32)
    for b in range(1, c.bit_length() - 1):
        level = level + (diff_bits >= (1 << b)).astype(jnp.int32)
    level = jnp.where(col < row, level, jnp.where(col == row, -1, -2))
    og = og_ref[...]

    for h in range(C_HEADS_PER_STEP):
        sl = slice(h * C_DIM, (h + 1) * C_DIM)
        log2_f = lf_ref[:, sl]
        k = k_ref[:, sl].astype(F32)

        p1 = log2_f.astype(BF16)
        p2 = (log2_f - p1.astype(F32)).astype(BF16)
        cg = _dot(cumsum_mat, p1) + _dot(cumsum_mat, p2)

        q = q_ref[:, sl].astype(F32)
        v = v_ref[:, sl]
        st = st_ref[h]
        o = lax.dot_general((q * jnp.exp2(cg)).astype(BF16), st.astype(BF16), NT_DIMS,
                            preferred_element_type=F32)

        att = jnp.where(level == -1, jnp.sum(q * k, axis=-1, keepdims=True), 0.0)
        part = lax.dot_general((q * jnp.exp2(log2_f)).astype(BF16), k_ref[:, sl], NT_DIMS,
                               preferred_element_type=F32)
        att = jnp.where(level == 0, part, att)
        cg3 = cg.reshape(c // SUBLANES, SUBLANES, C_DIM)
        sub_row = lambda r: jnp.broadcast_to(cg3[:, r:r + 1, :], cg3.shape).reshape(c, C_DIM)
        n = 2
        while n < c:
            if 4 * n == SUBLANES:
                ref = jnp.where((row1 & (2 * n)) != 0, sub_row(3 * n - 1), sub_row(n - 1))
            elif 2 * n == SUBLANES:
                ref = sub_row(n - 1)
            else:
                ref = jnp.concatenate([jnp.broadcast_to(cg[lo + n - 1:lo + n, :], (2 * n, C_DIM))
                                       for lo in range(0, c, 2 * n)], axis=0)
            e = jnp.exp2(-jnp.abs(cg - ref))
            part = lax.dot_general((q * e).astype(BF16), (k * e).astype(BF16), NT_DIMS,
                                   preferred_element_type=F32)
            att = jnp.where(level == n.bit_length() - 1, part, att)
            n *= 2

        o = o + _dot(att.astype(BF16), v)

        last = cg[c - 1:c, :]
        kd = (k * jnp.exp2(last - cg)).astype(BF16)
        st_ref[h] = st * jnp.exp2(last) + lax.dot_general(v, kd, TN_DIMS, preferred_element_type=F32)

        y = o * lax.rsqrt(jnp.mean(o * o, axis=-1, keepdims=True) + EPS) * og
        o_ref[:, sl] = (y * gate_ref[:, sl].astype(F32)).astype(o_ref.dtype)


def _hgrn2(qvg, log2_f, k, o_gain, batch, seq):
    nc = seq // C_CHUNK
    hw = C_HEADS_PER_STEP * C_DIM
    nh = C_WIDTH // hw
    blk = lambda first: pl.BlockSpec((C_CHUNK, hw), lambda b, g, c: (b * nc + c, first + g))
    return pl.pallas_call(
        _hgrn2_kernel,
        grid=(batch, nh, nc),
        in_specs=[pl.BlockSpec((1, C_DIM), lambda b, g, c: (0, 0)),
                  blk(0), blk(nh), blk(2 * nh), blk(0), blk(0)],
        out_specs=blk(0),
        out_shape=jax.ShapeDtypeStruct((batch * seq, C_WIDTH), BF16),
        scratch_shapes=[pltpu.VMEM((C_HEADS_PER_STEP, C_DIM, C_DIM), F32)],
        compiler_params=_params(("parallel", "parallel", "arbitrary"), 32),
        name="hgrn2",
    )(o_gain.reshape(1, C_DIM), qvg, qvg, qvg, log2_f, k)


def _ffn(x, norm_g, w1, w3, w2, layer):
    h, w2b = _ffn_up(_rmsnorm(x, norm_g[layer]), w1, w3, w2, layer)
    return _matmul_res(h, w2b, x, tk=D_FF_PAD // 4)


def kernel(x, ffn1_norm, ffn1_w1, ffn1_w3, ffn1_w2, mix_norm, ffn2_norm, ffn2_w1, ffn2_w3, ffn2_w2, ab_w_in, ab_q_norm, ab_k_norm, ab_sinks, ab_v_ln_g, ab_v_ln_b, ab_w_s, ab_b_s, ab_w_out, c_w_in, c_lb_logits, c_o_norm, c_w_out):
    batch, seq, d = x.shape
    depth = ffn1_norm.shape[0]
    tn = 512
    x = x.reshape(batch * seq, d)
    for l in range(depth):
        x = _ffn(x, ffn1_norm, ffn1_w1, ffn1_w3, ffn1_w2, l)
        hn = _rmsnorm(x, mix_norm[l])
        j = l // 2
        if l % 2 == 0:
            kv_tile = A_WIDTH // tn
            n_tiles = AB_IN // tn
            o_tile = lambda t: jnp.where(t < kv_tile, t, jnp.where(t == kv_tile, n_tiles - 1, t - 1))
            proj = _wpipe_matmul("ab_in_proj", hn, ab_w_in, j, n_tiles, [F32], o_tile=o_tile,
                                 epilogue=functools.partial(_act_epilogue, act=_gelu, act_from=kv_tile + 1))
            mixed = _ab_mixer(proj, batch, seq, ab_q_norm[j], ab_k_norm[j], ab_sinks[j],
                              ab_v_ln_g[j], ab_v_ln_b[j], ab_w_s[j], ab_b_s[j])
            x = _wres_matmul("ab_out_proj", mixed, ab_w_out, j, d // tn, [F32], _residual_epilogue, tiled_extra=[x])
        else:
            ct = C_WIDTH // tn
            qvg = _wpipe_matmul("c_in_proj_qvg", hn, c_w_in, j, 3 * ct, [BF16],
                                functools.partial(_act_epilogue, act=_silu, act_from=2 * ct),
                                w_tile=lambda t: jnp.where(t < ct, t, t + ct))
            log2_f, k = _wres_matmul("c_in_proj_f", hn, c_w_in, j, ct, [F32, BF16],
                                     functools.partial(_forget_gate_epilogue, layer=l),
                                     w_tile=lambda t: t + ct, col_extra=[c_lb_logits])
            mixed = _hgrn2(qvg, log2_f, k, c_o_norm[j], batch, seq)
            x = _wres_matmul("c_out_proj", mixed, c_w_out, j, d // tn, [F32], _residual_epilogue, tiled_extra=[x])
        x = _ffn(x, ffn2_norm, ffn2_w1, ffn2_w3, ffn2_w2, l)
    return x.reshape(batch, seq, d)
```

```python
import functools

import jax
import jax.numpy as jnp
from jax import lax
from jax.experimental import pallas as pl
from jax.experimental.pallas import tpu as pltpu

F32 = jnp.float32
BF16 = jnp.bfloat16

D_FF = 11008
FF_TILE = 256
D_FF_PAD = 11264
EPS = 1e-6

A_HEADS = 32
A_KV_HEADS = 4
A_GROUP = A_HEADS // A_KV_HEADS
A_HEAD_DIM = 64
WINDOW = 128
A_WIDTH = A_HEADS * A_HEAD_DIM
A_KV_WIDTH = A_KV_HEADS * A_HEAD_DIM
B_GROUPS = 16
B_GROUP_DIM = 128
B_WIDTH = B_GROUPS * B_GROUP_DIM
AB_IN = A_WIDTH + 2 * A_KV_WIDTH + 2 * B_WIDTH

C_HEADS = 32
C_DIM = 128
C_WIDTH = C_HEADS * C_DIM
C_CHUNK = 128
C_HEADS_PER_STEP = 16

LANES = 128
SUBLANES = 8
MIB = 1 << 20
VMEM_SMALL_MIB = 32
VMEM_MATMUL_MIB = 56
VMEM_PIPELINED_MIB = 60
LOG2E = 1.4426950408889634

NT_DIMS = (((1,), (1,)), ((), ()))
TN_DIMS = (((0,), (0,)), ((), ()))


def _params(semantics, vmem_mib):
    return pltpu.CompilerParams(dimension_semantics=semantics, vmem_limit_bytes=vmem_mib * MIB)


def _dot(a, b):
    return jnp.dot(a, b, preferred_element_type=F32)


def _gelu(y):
    return 0.5 * y * (1.0 + lax.erf(y * (2.0 ** -0.5)))


def _silu(y):
    half = 0.5 * y
    return half + half * jnp.tanh(half)


def _rmsnorm_kernel(x_ref, g_ref, o_ref):
    x = x_ref[...]
    r = lax.rsqrt(jnp.mean(x * x, axis=-1, keepdims=True) + EPS)
    o_ref[...] = (x * r * g_ref[...]).astype(o_ref.dtype)


def _rmsnorm(x, g, rows=256):
    m, d = x.shape
    return pl.pallas_call(
        _rmsnorm_kernel,
        grid=(m // rows,),
        in_specs=[pl.BlockSpec((rows, d), lambda i: (i, 0)),
                  pl.BlockSpec((1, d), lambda i: (0, 0))],
        out_specs=pl.BlockSpec((rows, d), lambda i: (i, 0)),
        out_shape=jax.ShapeDtypeStruct((m, d), BF16),
        compiler_params=_params(("parallel",), VMEM_SMALL_MIB),
        name="rmsnorm",
    )(x, g.reshape(1, d))


def _store_epilogue(j, y, extra_refs, out_refs):
    out_refs[0][...] = y.astype(out_refs[0].dtype)


def _residual_epilogue(j, y, extra_refs, out_refs):
    out_refs[0][...] = extra_refs[0][...] + y


def _act_epilogue(j, y, extra_refs, out_refs, *, act, act_from):
    @pl.when(j >= act_from)
    def _():
        out_refs[0][...] = act(y).astype(out_refs[0].dtype)

    @pl.when(j < act_from)
    def _():
        out_refs[0][...] = y.astype(out_refs[0].dtype)


def _forget_gate_epilogue(j, z, extra_refs, out_refs, *, layer):
    logits = extra_refs[0][...]
    lmax = jnp.max(logits, axis=0, keepdims=True)
    ex = jnp.exp(logits - lmax)
    prob = ex / jnp.sum(ex, axis=0, keepdims=True)
    lb = jnp.sum(prob[:layer + 1], axis=0, keepdims=True) - prob[0:1]
    c1 = 0.5 * (1.0 - lb)
    c0 = lb + c1
    scaled = c1 * jnp.tanh(0.5 * z)
    out_refs[0][...] = jnp.log2(c0 + scaled)
    out_refs[1][...] = (c1 - scaled).astype(out_refs[1].dtype)


def _wpipe_matmul_kernel(*refs, n_extra, epilogue):
    a_ref, w_ref = refs[:2]
    extra_refs = refs[2:2 + n_extra]
    out_refs = refs[2 + n_extra:-2]
    p = pl.program_id(0)
    slab = w_ref.shape[0]
    r0 = pl.multiple_of(pl.program_id(1) * slab, slab)

    def step(w_next_ref, w_cur_ref, compute):
        w_next_ref[pl.ds(r0, slab), :] = w_ref[...].astype(BF16)
        if compute:
            epilogue(p - 1, _dot(a_ref[...], w_cur_ref[...]), extra_refs, out_refs)

    even = (p % 2) == 0
    pl.when((p > 0) & even)(functools.partial(step, refs[-2], refs[-1], True))
    pl.when((p > 0) & jnp.logical_not(even))(functools.partial(step, refs[-1], refs[-2], True))
    pl.when(p == 0)(functools.partial(step, refs[-2], refs[-1], False))


def _wpipe_matmul(name, a, w, layer, n_tiles, out_dtypes, epilogue=_store_epilogue, w_tile=lambda j: j,
                  o_tile=lambda j: j, tiled_extra=(), col_extra=(), tm=2048, tn=512):
    m, k = a.shape
    n_i = m // tm
    next_tile = lambda p: w_tile(jnp.minimum(p, n_tiles - 1))
    cur_tile = lambda p: o_tile(jnp.maximum(p - 1, 0))
    row_block = lambda p, i: jnp.where(p == 0, 0, i)
    out_block = pl.BlockSpec((tm, tn), lambda p, i: (row_block(p, i), cur_tile(p)))
    in_specs = [pl.BlockSpec((tm, k), lambda p, i: (row_block(p, i), 0)),
                pl.BlockSpec((None, k // n_i, tn), lambda p, i: (layer, i, next_tile(p)))]
    in_specs += [out_block for _ in tiled_extra]
    in_specs += [pl.BlockSpec((e.shape[0], tn), lambda p, i: (0, cur_tile(p))) for e in col_extra]
    wb = pltpu.VMEM((k, tn), BF16)
    outs = pl.pallas_call(
        functools.partial(_wpipe_matmul_kernel, n_extra=len(tiled_extra) + len(col_extra), epilogue=epilogue),
        grid=(n_tiles + 1, n_i),
        in_specs=in_specs,
        out_specs=[out_block for _ in out_dtypes],
        out_shape=[jax.ShapeDtypeStruct((m, n_tiles * tn), dt) for dt in out_dtypes],
        scratch_shapes=[wb, wb],
        compiler_params=_params(("arbitrary", "arbitrary"), VMEM_PIPELINED_MIB),
        name=name,
    )(a, w, *tiled_extra, *col_extra)
    return outs if len(outs) > 1 else outs[0]


def _ffn_up_kernel(a_ref, w1_ref, w3_ref, w2_ref, h_ref, w2b_ref, w1e_ref, w3e_ref, w1o_ref, w3o_ref, *, n_tiles):
    p = pl.program_id(0)
    i = pl.program_id(1)
    t = p - 1
    slab = w1_ref.shape[0]
    r0 = pl.multiple_of(i * slab, slab)
    slab2 = w2_ref.shape[0]
    q0 = pl.multiple_of(i * slab2, slab2)

    def compute(w1_next_ref, w3_next_ref, w1_cur_ref, w3_cur_ref):
        w1_next_ref[pl.ds(r0, slab), :] = w1_ref[...].astype(BF16)
        w3_next_ref[pl.ds(r0, slab), :] = w3_ref[...].astype(BF16)
        a = a_ref[...]
        y1 = _dot(a, w1_cur_ref[...])
        y3 = _dot(a, w3_cur_ref[...])
        h_ref[...] = (_silu(y1) * (0.5 * y3)).astype(h_ref.dtype)
        w2b_ref[pl.ds(q0, slab2), :] = w2_ref[...].astype(BF16)

    real = (t >= 0) & (t < n_tiles)
    even = (p % 2) == 0
    pl.when(real & even)(functools.partial(compute, w1e_ref, w3e_ref, w1o_ref, w3o_ref))
    pl.when(real & jnp.logical_not(even))(functools.partial(compute, w1o_ref, w3o_ref, w1e_ref, w3e_ref))

    @pl.when(p == 0)
    def _():
        w1e_ref[pl.ds(r0, slab), :] = w1_ref[...].astype(BF16)
        w3e_ref[pl.ds(r0, slab), :] = w3_ref[...].astype(BF16)

    @pl.when(t == n_tiles)
    def _():
        h_ref[...] = jnp.zeros_like(h_ref)
        w2b_ref[pl.ds(q0, slab2), :] = jnp.zeros((slab2, w2b_ref.shape[1]), BF16)


def _ffn_up(a, w1, w3, w2, layer, tm=2048):
    m, k = a.shape
    tf = FF_TILE
    n_tiles = D_FF // tf
    n_i = m // tm
    n_pos = D_FF_PAD // tf + 1
    next_tile = lambda p: jnp.minimum(p, n_tiles - 1)
    cur_tile = lambda p: jnp.clip(p - 1, 0, n_tiles - 1)
    out_tile = lambda p: jnp.maximum(p - 1, 0)
    a_block = lambda p, i: jnp.where(p == 0, 0, jnp.where(p == n_pos - 1, n_i - 1, i))
    h_block = lambda p, i: jnp.where(p == 0, 0, i)
    wb = pltpu.VMEM((k, tf), BF16)
    return pl.pallas_call(
        functools.partial(_ffn_up_kernel, n_tiles=n_tiles),
        grid=(n_pos, n_i),
        in_specs=[pl.BlockSpec((tm, k), lambda p, i: (a_block(p, i), 0)),
                  pl.BlockSpec((None, k // n_i, tf), lambda p, i: (layer, i, next_tile(p))),
                  pl.BlockSpec((None, k // n_i, tf), lambda p, i: (layer, i, next_tile(p))),
                  pl.BlockSpec((None, tf // n_i, k), lambda p, i: (layer, cur_tile(p) * n_i + i, 0))],
        out_specs=[pl.BlockSpec((tm, tf), lambda p, i: (h_block(p, i), out_tile(p))),
                   pl.BlockSpec((tf, k), lambda p, i: (out_tile(p), 0))],
        out_shape=[jax.ShapeDtypeStruct((m, D_FF_PAD), BF16),
                   jax.ShapeDtypeStruct((D_FF_PAD, k), BF16)],
        scratch_shapes=[wb, wb, wb, wb],
        compiler_params=_params(("arbitrary", "arbitrary"), VMEM_PIPELINED_MIB),
        name="ffn_up",
    )(a, w1, w3, w2)


def _matmul_res_kernel(a_ref, w_ref, r_ref, o_ref):
    base = jnp.where(pl.program_id(2) == 0, r_ref[...], o_ref[...])
    o_ref[...] = base + _dot(a_ref[...], w_ref[...])


def _matmul_res(a, w, res, tk, tm=1024, tn=1024):
    m, k = a.shape
    n = w.shape[1]
    return pl.pallas_call(
        _matmul_res_kernel,
        grid=(m // tm, n // tn, k // tk),
        in_specs=[pl.BlockSpec((tm, tk), lambda i, j, kk: (i, kk)),
                  pl.BlockSpec((tk, tn), lambda i, j, kk: (kk, j)),
                  pl.BlockSpec((tm, tn), lambda i, j, kk: (i, j))],
        out_specs=pl.BlockSpec((tm, tn), lambda i, j, kk: (i, j)),
        out_shape=jax.ShapeDtypeStruct((m, n), F32),
        compiler_params=_params(("parallel", "parallel", "arbitrary"), VMEM_MATMUL_MIB),
        name="matmul_res",
    )(a, w, res)


def _ab_mixer_kernel(sinks_ref, qug_ref, kvp_ref, kvc_ref, qg_ref, kg_ref,
                     lng_ref, lnb_ref, ws_ref, bs_ref, o_ref):
    n = pl.program_id(1)
    w = WINDOW
    hd = A_HEAD_DIM

    key = lax.broadcasted_iota(jnp.int32, (2 * w, w), 0)
    qry = lax.broadcasted_iota(jnp.int32, (2 * w, w), 1)
    dist = qry + w - key
    visible = (dist >= 0) & (dist < w) & ((key >= w) | (n > 0))
    low = lax.broadcasted_iota(jnp.int32, (1, LANES), 1) < hd
    sub8 = lax.broadcasted_iota(jnp.int32, (SUBLANES, LANES), 0)
    ones_low = jnp.where(low, 1.0, 0.0)
    ones_high = 1.0 - ones_low
    sel2 = jnp.where(sub8 < SUBLANES // 2, ones_low, ones_high).astype(BF16)
    sink_row = sub8 == 0

    def split_bf16(x):
        hi = x.astype(BF16)
        return hi, (x - hi.astype(F32)).astype(BF16)

    kk = jnp.concatenate([kvp_ref[:, :A_KV_WIDTH], kvc_ref[:, :A_KV_WIDTH]], axis=0)
    vv = jnp.concatenate([kvp_ref[:, A_KV_WIDTH:], kvc_ref[:, A_KV_WIDTH:]], axis=0)
    seg_r = lax.broadcasted_iota(jnp.int32, (A_KV_WIDTH, A_KV_WIDTH), 0) // hd
    seg_c = lax.broadcasted_iota(jnp.int32, (A_KV_WIDTH, A_KV_WIDTH), 1) // hd
    seg = (seg_r == seg_c).astype(BF16)
    hi, lo = split_bf16(kk * kk)
    kss = _dot(hi, seg) + _dot(lo, seg)
    kgain = jnp.concatenate([kg_ref[...]] * A_KV_HEADS, axis=1)
    khat = kk * lax.rsqrt(kss * (1.0 / hd) + EPS) * kgain
    qgain = jnp.concatenate([qg_ref[...]] * 2, axis=1) * (hd ** -0.5 * LOG2E)

    def head_pair_operands(x, h):
        tile = x[:, (h // 2) * LANES:(h // 2 + 1) * LANES]
        swapped = pltpu.roll(tile, hd, axis=1)
        in_low, in_high = (tile, swapped) if h % 2 == 0 else (swapped, tile)
        return jnp.where(low, in_low, 0.0), jnp.where(low, 0.0, in_high)

    for h in range(A_KV_HEADS):
        k_low, k_high = head_pair_operands(khat, h)
        v_low, v_high = head_pair_operands(vv, h)
        k_pad = (k_low.astype(BF16), k_high.astype(BF16))
        v_aug = tuple(
            jnp.concatenate([
                jnp.concatenate([v_p, jnp.broadcast_to(ones_p, v_p.shape)], axis=1),
                jnp.concatenate([jnp.zeros((SUBLANES, LANES), F32), jnp.broadcast_to(ones_p, (SUBLANES, LANES))], axis=1),
            ], axis=0).astype(BF16)
            for v_p, ones_p in ((v_low, ones_low), (v_high, ones_high)))
        for c in range(h * A_GROUP // 2, (h + 1) * A_GROUP // 2):
            qc = qug_ref[:, c * LANES:(c + 1) * LANES]
            hi, lo = split_bf16(qc * qc)
            qss = (lax.dot_general(sel2, hi, NT_DIMS, preferred_element_type=F32)
                   + lax.dot_general(sel2, lo, NT_DIMS, preferred_element_type=F32))
            qscale = lax.rsqrt(qss * (1.0 / hd) + EPS)
            qg = (qc * qgain).astype(BF16)
            acc = jnp.zeros((w, 2 * LANES), F32)
            for parity in range(2):
                st = lax.dot_general(k_pad[parity], qg, NT_DIMS, preferred_element_type=F32)
                r = qscale[parity * (SUBLANES // 2):parity * (SUBLANES // 2) + 1]
                st = jnp.where(visible, st * r, -jnp.inf)
                s_sink = sinks_ref[2 * c + parity] * LOG2E
                mx = jnp.maximum(jnp.max(st, axis=0, keepdims=True), s_sink)
                e = jnp.exp2(st - mx)
                e_sink = jnp.where(sink_row, jnp.exp2(s_sink - mx), 0.0)
                e_aug = jnp.concatenate([e, e_sink], axis=0).astype(BF16)
                acc = acc + lax.dot_general(e_aug, v_aug[parity], TN_DIMS, preferred_element_type=F32)
            o_ref[:, c * LANES:(c + 1) * LANES] = (acc[:, :LANES] / acc[:, LANES:]).astype(o_ref.dtype)

    row = lax.broadcasted_iota(jnp.int32, (w, w), 0)
    col = lax.broadcasted_iota(jnp.int32, (w, w), 1)
    causal = row >= col
    u0 = A_WIDTH
    v0 = A_WIDTH + B_WIDTH
    for g in range(B_GROUPS):
        sl = slice(g * B_GROUP_DIM, (g + 1) * B_GROUP_DIM)
        x = qug_ref[:, v0 + g * B_GROUP_DIM:v0 + (g + 1) * B_GROUP_DIM]
        mu = jnp.mean(x, axis=-1, keepdims=True)
        d = x - mu
        var = jnp.mean(d * d, axis=-1, keepdims=True)
        vn = d * lax.rsqrt(var + EPS) * lng_ref[:, sl] + lnb_ref[:, sl]
        wg = jnp.where(causal, ws_ref[g], 0.0).astype(BF16)
        sg = _dot(wg, vn.astype(BF16)) + bs_ref[:, g:g + 1]
        o_ref[:, u0 + g * B_GROUP_DIM:u0 + (g + 1) * B_GROUP_DIM] = (
            qug_ref[:, u0 + g * B_GROUP_DIM:u0 + (g + 1) * B_GROUP_DIM] * sg).astype(o_ref.dtype)


def _ab_mixer(proj, batch, seq, q_gain, k_gain, sinks, ln_g, ln_b, w_s, b_s):
    nb = seq // WINDOW
    kv_blk = (A_WIDTH + 2 * B_WIDTH) // (2 * A_KV_WIDTH)
    row_blk = lambda b, n: b * nb + n
    return pl.pallas_call(
        _ab_mixer_kernel,
        grid=(batch, nb),
        in_specs=[
            pl.BlockSpec(memory_space=pltpu.SMEM),
            pl.BlockSpec((WINDOW, A_WIDTH + 2 * B_WIDTH), lambda b, n: (row_blk(b, n), 0)),
            pl.BlockSpec((WINDOW, 2 * A_KV_WIDTH), lambda b, n: (b * nb + jnp.maximum(n - 1, 0), kv_blk)),
            pl.BlockSpec((WINDOW, 2 * A_KV_WIDTH), lambda b, n: (row_blk(b, n), kv_blk)),
            pl.BlockSpec((1, A_HEAD_DIM), lambda b, n: (0, 0)),
            pl.BlockSpec((1, A_HEAD_DIM), lambda b, n: (0, 0)),
            pl.BlockSpec((1, B_WIDTH), lambda b, n: (0, 0)),
            pl.BlockSpec((1, B_WIDTH), lambda b, n: (0, 0)),
            pl.BlockSpec((B_GROUPS, WINDOW, WINDOW), lambda b, n: (0, 0, 0)),
            pl.BlockSpec((WINDOW, B_GROUPS), lambda b, n: (0, 0)),
        ],
        out_specs=pl.BlockSpec((WINDOW, A_WIDTH + B_WIDTH), lambda b, n: (row_blk(b, n), 0)),
        out_shape=jax.ShapeDtypeStruct((batch * seq, A_WIDTH + B_WIDTH), BF16),
        compiler_params=_params(("parallel", "arbitrary"), VMEM_SMALL_MIB),
        name="ab_mixer",
    )(sinks, proj, proj, proj,
      q_gain.reshape(1, A_HEAD_DIM), k_gain.reshape(1, A_HEAD_DIM),
      ln_g.reshape(1, B_WIDTH), ln_b.reshape(1, B_WIDTH), w_s, b_s.T)


def _hgrn2_kernel(og_ref, qvg_ref, lf_ref, k_ref, o_ref, st_ref):
    c = C_CHUNK
    hw = C_HEADS_PER_STEP * C_DIM

    @pl.when(pl.program_id(2) == 0)
    def _():
        st_ref[...] = jnp.zeros_like(st_ref)

    row = lax.broadcasted_iota(jnp.int32, (c, c), 0)
    col = lax.broadcasted_iota(jnp.int32, (c, c), 1)
    row1 = lax.broadcasted_iota(jnp.int32, (c, 1), 0)
    cumsum_mat = (row >= col).astype(BF16)
    diff_bits = row ^ col
    level = jnp.zeros((c, c), jnp.int32)
    for b in range(1, c.bit_length() - 1):
        level = level + (diff_bits >= (1 << b)).astype(jnp.int32)
    level = jnp.where(col < row, level, jnp.where(col == row, -1, -2))
    og = og_ref[...]

    for h in range(C_HEADS_PER_STEP):
        sl = slice(h * C_DIM, (h + 1) * C_DIM)
        log2_f = lf_ref[:, sl]
        k = k_ref[:, sl].astype(F32)

        p1 = log2_f.astype(BF16)
        p2 = (log2_f - p1.astype(F32)).astype(BF16)
        cg = _dot(cumsum_mat, p1) + _dot(cumsum_mat, p2)

        q = qvg_ref[:, sl].astype(F32)
        v = qvg_ref[:, hw + h * C_DIM:hw + (h + 1) * C_DIM]
        st = st_ref[h]
        o = lax.dot_general((q * jnp.exp2(cg)).astype(BF16), st.astype(BF16), NT_DIMS,
                            preferred_element_type=F32)

        att = jnp.where(level == -1, jnp.sum(q * k, axis=-1, keepdims=True), 0.0)
        part = lax.dot_general((q * jnp.exp2(log2_f)).astype(BF16), k_ref[:, sl], NT_DIMS,
                               preferred_element_type=F32)
        att = jnp.where(level == 0, part, att)
        cg3 = cg.reshape(c // SUBLANES, SUBLANES, C_DIM)
        sub_row = lambda r: jnp.broadcast_to(cg3[:, r:r + 1, :], cg3.shape).reshape(c, C_DIM)
        n = 2
        while n < c:
            if 4 * n == SUBLANES:
                ref = jnp.where((row1 & (2 * n)) != 0, sub_row(3 * n - 1), sub_row(n - 1))
            elif 2 * n == SUBLANES:
                ref = sub_row(n - 1)
            else:
                ref = jnp.concatenate([jnp.broadcast_to(cg[lo + n - 1:lo + n, :], (2 * n, C_DIM))
                                       for lo in range(0, c, 2 * n)], axis=0)
            e = jnp.exp2(-jnp.abs(cg - ref))
            part = lax.dot_general((q * e).astype(BF16), (k * e).astype(BF16), NT_DIMS,
                                   preferred_element_type=F32)
            att = jnp.where(level == n.bit_length() - 1, part, att)
            n *= 2

        o = o + _dot(att.astype(BF16), v)

        last = cg[c - 1:c, :]
        kd = (k * jnp.exp2(last - cg)).astype(BF16)
        st_ref[h] = st * jnp.exp2(last) + lax.dot_general(v, kd, TN_DIMS, preferred_element_type=F32)

        y = o * lax.rsqrt(jnp.mean(o * o, axis=-1, keepdims=True) + EPS) * og
        gate = qvg_ref[:, 2 * hw + h * C_DIM:2 * hw + (h + 1) * C_DIM]
        o_ref[:, sl] = (y * gate.astype(F32)).astype(o_ref.dtype)


def _hgrn2(qvg, log2_f, k, o_gain, batch, seq):
    nc = seq // C_CHUNK
    hw = C_HEADS_PER_STEP * C_DIM
    nh = C_WIDTH // hw
    blk = lambda width: pl.BlockSpec((C_CHUNK, width), lambda b, g, c: (b * nc + c, g))
    return pl.pallas_call(
        _hgrn2_kernel,
        grid=(batch, nh, nc),
        in_specs=[pl.BlockSpec((1, C_DIM), lambda b, g, c: (0, 0)),
                  blk(3 * hw), blk(hw), blk(hw)],
        out_specs=blk(hw),
        out_shape=jax.ShapeDtypeStruct((batch * seq, C_WIDTH), BF16),
        scratch_shapes=[pltpu.VMEM((C_HEADS_PER_STEP, C_DIM, C_DIM), F32)],
        compiler_params=_params(("parallel", "parallel", "arbitrary"), VMEM_SMALL_MIB),
        name="hgrn2",
    )(o_gain.reshape(1, C_DIM), qvg, log2_f, k)


def _ffn(x, norm_g, w1, w3, w2, layer):
    h, w2b = _ffn_up(_rmsnorm(x, norm_g[layer]), w1, w3, w2, layer)
    return _matmul_res(h, w2b, x, tk=D_FF_PAD // 4)


def kernel(x, ffn1_norm, ffn1_w1, ffn1_w3, ffn1_w2, mix_norm, ffn2_norm, ffn2_w1, ffn2_w3, ffn2_w2, ab_w_in, ab_q_norm, ab_k_norm, ab_sinks, ab_v_ln_g, ab_v_ln_b, ab_w_s, ab_b_s, ab_w_out, c_w_in, c_lb_logits, c_o_norm, c_w_out):
    batch, seq, d = x.shape
    depth = ffn1_norm.shape[0]
    tn = 512
    to = 256
    x = x.reshape(batch * seq, d)
    for l in range(depth):
        x = _ffn(x, ffn1_norm, ffn1_w1, ffn1_w3, ffn1_w2, l)
        hn = _rmsnorm(x, mix_norm[l])
        j = l // 2
        if l % 2 == 0:
            kv_tile = A_WIDTH // tn
            n_tiles = AB_IN // tn
            o_tile = lambda t: jnp.where(t < kv_tile, t, jnp.where(t == kv_tile, n_tiles - 1, t - 1))
            proj = _wpipe_matmul("ab_in_proj", hn, ab_w_in, j, n_tiles, [F32], o_tile=o_tile,
                                 epilogue=functools.partial(_act_epilogue, act=_gelu, act_from=kv_tile + 1))
            mixed = _ab_mixer(proj, batch, seq, ab_q_norm[j], ab_k_norm[j], ab_sinks[j],
                              ab_v_ln_g[j], ab_v_ln_b[j], ab_w_s[j], ab_b_s[j])
            x = _wpipe_matmul("ab_out_proj", mixed, ab_w_out, j, d // to, [F32], _residual_epilogue,
                              tiled_extra=[x], tn=to)
        else:
            ct = C_WIDTH // tn
            gt = C_HEADS_PER_STEP * C_DIM // tn
            qvg = _wpipe_matmul("c_in_proj_qvg", hn, c_w_in, j, 3 * ct, [BF16],
                                functools.partial(_act_epilogue, act=_silu, act_from=2 * ct),
                                w_tile=lambda t: jnp.where(t < ct, t, t + ct),
                                o_tile=lambda t: (t % ct) // gt * (3 * gt) + t // ct * gt + t % gt)
            cf = C_WIDTH // to
            log2_f, k = _wpipe_matmul("c_in_proj_f", hn, c_w_in, j, cf, [F32, BF16],
                                      functools.partial(_forget_gate_epilogue, layer=l),
                                      w_tile=lambda t: t + cf, col_extra=[c_lb_logits], tn=to)
            mixed = _hgrn2(qvg, log2_f, k, c_o_norm[j], batch, seq)
            x = _wpipe_matmul("c_out_proj", mixed, c_w_out, j, d // to, [F32], _residual_epilogue,
                              tiled_extra=[x], tn=to)
        x = _ffn(x, ffn2_norm, ffn2_w1, ffn2_w3, ffn2_w2, l)
    return x.reshape(batch, seq, d)
```

```python
import functools

import jax
import jax.numpy as jnp
from jax import lax
from jax.experimental import pallas as pl
from jax.experimental.pallas import tpu as pltpu

F32 = jnp.float32
BF16 = jnp.bfloat16

D_FF = 11008
FF_TILE = 256
D_FF_PAD = 11264
EPS = 1e-6

A_HEADS = 32
A_KV_HEADS = 4
A_GROUP = A_HEADS // A_KV_HEADS
A_HEAD_DIM = 64
WINDOW = 128
A_WIDTH = A_HEADS * A_HEAD_DIM
A_KV_WIDTH = A_KV_HEADS * A_HEAD_DIM
B_GROUPS = 16
B_GROUP_DIM = 128
B_WIDTH = B_GROUPS * B_GROUP_DIM
AB_IN = A_WIDTH + 2 * A_KV_WIDTH + 2 * B_WIDTH

C_HEADS = 32
C_DIM = 128
C_WIDTH = C_HEADS * C_DIM
C_CHUNK = 128
C_HEADS_PER_STEP = 16

LANES = 128
SUBLANES = 8
MIB = 1 << 20
VMEM_SMALL_MIB = 32
VMEM_MATMUL_MIB = 56
VMEM_PIPELINED_MIB = 60
LOG2E = 1.4426950408889634

NT_DIMS = (((1,), (1,)), ((), ()))
TN_DIMS = (((0,), (0,)), ((), ()))


def _params(semantics, vmem_mib):
    return pltpu.CompilerParams(dimension_semantics=semantics, vmem_limit_bytes=vmem_mib * MIB)


def _dot(a, b):
    return jnp.dot(a, b, preferred_element_type=F32)


def _gelu(y):
    return 0.5 * y * (1.0 + lax.erf(y * (2.0 ** -0.5)))


def _silu(y):
    half = 0.5 * y
    return half + half * jnp.tanh(half)


def _rmsnorm_kernel(x_ref, g_ref, o_ref):
    x = x_ref[...]
    r = lax.rsqrt(jnp.mean(x * x, axis=-1, keepdims=True) + EPS)
    o_ref[...] = (x * r * g_ref[...]).astype(o_ref.dtype)


def _rmsnorm(x, g, rows=256):
    m, d = x.shape
    return pl.pallas_call(
        _rmsnorm_kernel,
        grid=(m // rows,),
        in_specs=[pl.BlockSpec((rows, d), lambda i: (i, 0)),
                  pl.BlockSpec((1, d), lambda i: (0, 0))],
        out_specs=pl.BlockSpec((rows, d), lambda i: (i, 0)),
        out_shape=jax.ShapeDtypeStruct((m, d), BF16),
        compiler_params=_params(("parallel",), VMEM_SMALL_MIB),
        name="rmsnorm",
    )(x, g.reshape(1, d))


def _store_epilogue(j, y, extra_refs, out_refs):
    out_refs[0][...] = y.astype(out_refs[0].dtype)


def _residual_epilogue(j, y, extra_refs, out_refs):
    out_refs[0][...] = extra_refs[0][...] + y


def _act_epilogue(j, y, extra_refs, out_refs, *, act, act_from):
    @pl.when(j >= act_from)
    def _():
        out_refs[0][...] = act(y).astype(out_refs[0].dtype)

    @pl.when(j < act_from)
    def _():
        out_refs[0][...] = y.astype(out_refs[0].dtype)


def _forget_gate_epilogue(j, z, extra_refs, out_refs, *, layer):
    logits = extra_refs[0][...]
    lmax = jnp.max(logits, axis=0, keepdims=True)
    ex = jnp.exp(logits - lmax)
    prob = ex / jnp.sum(ex, axis=0, keepdims=True)
    lb = jnp.sum(prob[:layer + 1], axis=0, keepdims=True) - prob[0:1]
    c1 = 0.5 * (1.0 - lb)
    c0 = lb + c1
    scaled = c1 * jnp.tanh(0.5 * z)
    out_refs[0][...] = jnp.log2(c0 + scaled)
    out_refs[1][...] = (c1 - scaled).astype(out_refs[1].dtype)


def _wpipe_matmul_kernel(*refs, n_extra, epilogue):
    a_ref, w_ref = refs[:2]
    extra_refs = refs[2:2 + n_extra]
    out_refs = refs[2 + n_extra:-2]
    p = pl.program_id(0)
    slab = w_ref.shape[0]
    r0 = pl.multiple_of(pl.program_id(1) * slab, slab)

    def step(w_next_ref, w_cur_ref, compute):
        w_next_ref[pl.ds(r0, slab), :] = w_ref[...].astype(BF16)
        if compute:
            epilogue(p - 1, _dot(a_ref[...], w_cur_ref[...]), extra_refs, out_refs)

    even = (p % 2) == 0
    pl.when((p > 0) & even)(functools.partial(step, refs[-2], refs[-1], True))
    pl.when((p > 0) & jnp.logical_not(even))(functools.partial(step, refs[-1], refs[-2], True))
    pl.when(p == 0)(functools.partial(step, refs[-2], refs[-1], False))


def _wpipe_matmul(name, a, w, layer, n_tiles, out_dtypes, epilogue=_store_epilogue, w_tile=lambda j: j,
                  o_tile=lambda j: j, tiled_extra=(), col_extra=(), tm=2048, tn=512):
    m, k = a.shape
    n_i = m // tm
    next_tile = lambda p: w_tile(jnp.minimum(p, n_tiles - 1))
    cur_tile = lambda p: o_tile(jnp.maximum(p - 1, 0))
    row_block = lambda p, i: jnp.where(p == 0, 0, i)
    out_block = pl.BlockSpec((tm, tn), lambda p, i: (row_block(p, i), cur_tile(p)))
    in_specs = [pl.BlockSpec((tm, k), lambda p, i: (row_block(p, i), 0)),
                pl.BlockSpec((None, k // n_i, tn), lambda p, i: (layer, i, next_tile(p)))]
    in_specs += [out_block for _ in tiled_extra]
    in_specs += [pl.BlockSpec((e.shape[0], tn), lambda p, i: (0, cur_tile(p))) for e in col_extra]
    wb = pltpu.VMEM((k, tn), BF16)
    outs = pl.pallas_call(
        functools.partial(_wpipe_matmul_kernel, n_extra=len(tiled_extra) + len(col_extra), epilogue=epilogue),
        grid=(n_tiles + 1, n_i),
        in_specs=in_specs,
        out_specs=[out_block for _ in out_dtypes],
        out_shape=[jax.ShapeDtypeStruct((m, n_tiles * tn), dt) for dt in out_dtypes],
        scratch_shapes=[wb, wb],
        compiler_params=_params(("arbitrary", "arbitrary"), VMEM_PIPELINED_MIB),
        name=name,
    )(a, w, *tiled_extra, *col_extra)
    return outs if len(outs) > 1 else outs[0]


def _ffn_up_kernel(a_ref, w1_ref, w3_ref, w2_ref, h_ref, w2b_ref, w1e_ref, w3e_ref, w1o_ref, w3o_ref, *, n_tiles):
    p = pl.program_id(0)
    i = pl.program_id(1)
    t = p - 1
    slab = w1_ref.shape[0]
    r0 = pl.multiple_of(i * slab, slab)
    slab2 = w2_ref.shape[0]
    q0 = pl.multiple_of(i * slab2, slab2)

    def compute(w1_next_ref, w3_next_ref, w1_cur_ref, w3_cur_ref):
        w1_next_ref[pl.ds(r0, slab), :] = w1_ref[...].astype(BF16)
        w3_next_ref[pl.ds(r0, slab), :] = w3_ref[...].astype(BF16)
        a = a_ref[...]
        y1 = _dot(a, w1_cur_ref[...])
        y3 = _dot(a, w3_cur_ref[...])
        h_ref[...] = (_silu(y1) * (0.5 * y3)).astype(h_ref.dtype)
        w2b_ref[pl.ds(q0, slab2), :] = w2_ref[...].astype(BF16)

    real = (t >= 0) & (t < n_tiles)
    even = (p % 2) == 0
    pl.when(real & even)(functools.partial(compute, w1e_ref, w3e_ref, w1o_ref, w3o_ref))
    pl.when(real & jnp.logical_not(even))(functools.partial(compute, w1o_ref, w3o_ref, w1e_ref, w3e_ref))

    @pl.when(p == 0)
    def _():
        w1e_ref[pl.ds(r0, slab), :] = w1_ref[...].astype(BF16)
        w3e_ref[pl.ds(r0, slab), :] = w3_ref[...].astype(BF16)

    @pl.when(t == n_tiles)
    def _():
        h_ref[...] = jnp.zeros_like(h_ref)
        w2b_ref[pl.ds(q0, slab2), :] = jnp.zeros((slab2, w2b_ref.shape[1]), BF16)


def _ffn_up(a, w1, w3, w2, layer, tm=2048):
    m, k = a.shape
    tf = FF_TILE
    n_tiles = D_FF // tf
    n_i = m // tm
    n_pos = D_FF_PAD // tf + 1
    next_tile = lambda p: jnp.minimum(p, n_tiles - 1)
    cur_tile = lambda p: jnp.clip(p - 1, 0, n_tiles - 1)
    out_tile = lambda p: jnp.maximum(p - 1, 0)
    a_block = lambda p, i: jnp.where(p == 0, 0, jnp.where(p == n_pos - 1, n_i - 1, i))
    h_block = lambda p, i: jnp.where(p == 0, 0, i)
    wb = pltpu.VMEM((k, tf), BF16)
    return pl.pallas_call(
        functools.partial(_ffn_up_kernel, n_tiles=n_tiles),
        grid=(n_pos, n_i),
        in_specs=[pl.BlockSpec((tm, k), lambda p, i: (a_block(p, i), 0)),
                  pl.BlockSpec((None, k // n_i, tf), lambda p, i: (layer, i, next_tile(p))),
                  pl.BlockSpec((None, k // n_i, tf), lambda p, i: (layer, i, next_tile(p))),
                  pl.BlockSpec((None, tf // n_i, k), lambda p, i: (layer, cur_tile(p) * n_i + i, 0))],
        out_specs=[pl.BlockSpec((tm, tf), lambda p, i: (h_block(p, i), out_tile(p))),
                   pl.BlockSpec((tf, k), lambda p, i: (out_tile(p), 0))],
        out_shape=[jax.ShapeDtypeStruct((m, D_FF_PAD), BF16),
                   jax.ShapeDtypeStruct((D_FF_PAD, k), BF16)],
        scratch_shapes=[wb, wb, wb, wb],
        compiler_params=_params(("arbitrary", "arbitrary"), VMEM_PIPELINED_MIB),
        name="ffn_up",
    )(a, w1, w3, w2)


def _matmul_res_kernel(a_ref, w_ref, r_ref, o_ref):
    base = jnp.where(pl.program_id(2) == 0, r_ref[...], o_ref[...])
    o_ref[...] = base + _dot(a_ref[...], w_ref[...])


def _matmul_res(a, w, res, tk, tm=1024, tn=1024):
    m, k = a.shape
    n = w.shape[1]
    return pl.pallas_call(
        _matmul_res_kernel,
        grid=(m // tm, n // tn, k // tk),
        in_specs=[pl.BlockSpec((tm, tk), lambda i, j, kk: (i, kk)),
                  pl.BlockSpec((tk, tn), lambda i, j, kk: (kk, j)),
                  pl.BlockSpec((tm, tn), lambda i, j, kk: (i, j))],
        out_specs=pl.BlockSpec((tm, tn), lambda i, j, kk: (i, j)),
        out_shape=jax.ShapeDtypeStruct((m, n), F32),
        compiler_params=_params(("parallel", "parallel", "arbitrary"), VMEM_MATMUL_MIB),
        name="matmul_res",
    )(a, w, res)


def _ab_mixer_kernel(sinks_ref, qug_ref, kvp_ref, kvc_ref, qg_ref, kg_ref,
                     lng_ref, lnb_ref, ws_ref, bs_ref, o_ref):
    n = pl.program_id(1)
    w = WINDOW
    hd = A_HEAD_DIM

    key = lax.broadcasted_iota(jnp.int32, (2 * w, w), 0)
    qry = lax.broadcasted_iota(jnp.int32, (2 * w, w), 1)
    dist = qry + w - key
    visible = (dist >= 0) & (dist < w) & ((key >= w) | (n > 0))
    low = lax.broadcasted_iota(jnp.int32, (1, LANES), 1) < hd
    sub8 = lax.broadcasted_iota(jnp.int32, (SUBLANES, LANES), 0)
    ones_low = jnp.where(low, 1.0, 0.0)
    ones_high = 1.0 - ones_low
    sel2 = jnp.where(sub8 < SUBLANES // 2, ones_low, ones_high).astype(BF16)
    sink_row = sub8 == 0

    def split_bf16(x):
        hi = x.astype(BF16)
        return hi, (x - hi.astype(F32)).astype(BF16)

    kk = jnp.concatenate([kvp_ref[:, :A_KV_WIDTH], kvc_ref[:, :A_KV_WIDTH]], axis=0)
    vv = jnp.concatenate([kvp_ref[:, A_KV_WIDTH:], kvc_ref[:, A_KV_WIDTH:]], axis=0)
    seg_r = lax.broadcasted_iota(jnp.int32, (A_KV_WIDTH, A_KV_WIDTH), 0) // hd
    seg_c = lax.broadcasted_iota(jnp.int32, (A_KV_WIDTH, A_KV_WIDTH), 1) // hd
    seg = (seg_r == seg_c).astype(BF16)
    hi, lo = split_bf16(kk * kk)
    kss = _dot(hi, seg) + _dot(lo, seg)
    kgain = jnp.concatenate([kg_ref[...]] * A_KV_HEADS, axis=1)
    khat = kk * lax.rsqrt(kss * (1.0 / hd) + EPS) * kgain
    qgain = jnp.concatenate([qg_ref[...]] * 2, axis=1) * (hd ** -0.5 * LOG2E)

    def head_pair_operands(x, h):
        tile = x[:, (h // 2) * LANES:(h // 2 + 1) * LANES]
        swapped = pltpu.roll(tile, hd, axis=1)
        in_low, in_high = (tile, swapped) if h % 2 == 0 else (swapped, tile)
        return jnp.where(low, in_low, 0.0), jnp.where(low, 0.0, in_high)

    for h in range(A_KV_HEADS):
        k_low, k_high = head_pair_operands(khat, h)
        v_low, v_high = head_pair_operands(vv, h)
        k_pad = (k_low.astype(BF16), k_high.astype(BF16))
        v_aug = tuple(
            jnp.concatenate([
                jnp.concatenate([v_p, jnp.broadcast_to(ones_p, v_p.shape)], axis=1),
                jnp.concatenate([jnp.zeros((SUBLANES, LANES), F32), jnp.broadcast_to(ones_p, (SUBLANES, LANES))], axis=1),
            ], axis=0).astype(BF16)
            for v_p, ones_p in ((v_low, ones_low), (v_high, ones_high)))
        for c in range(h * A_GROUP // 2, (h + 1) * A_GROUP // 2):
            qc = qug_ref[:, c * LANES:(c + 1) * LANES]
            hi, lo = split_bf16(qc * qc)
            qss = (lax.dot_general(sel2, hi, NT_DIMS, preferred_element_type=F32)
                   + lax.dot_general(sel2, lo, NT_DIMS, preferred_element_type=F32))
            qscale = lax.rsqrt(qss * (1.0 / hd) + EPS)
            qg = (qc * qgain).astype(BF16)
            acc = jnp.zeros((w, 2 * LANES), F32)
            for parity in range(2):
                st = lax.dot_general(k_pad[parity], qg, NT_DIMS, preferred_element_type=F32)
                r = qscale[parity * (SUBLANES // 2):parity * (SUBLANES // 2) + 1]
                st = jnp.where(visible, st * r, -jnp.inf)
                s_sink = sinks_ref[2 * c + parity] * LOG2E
                mx = jnp.maximum(jnp.max(st, axis=0, keepdims=True), s_sink)
                e = jnp.exp2(st - mx)
                e_sink = jnp.where(sink_row, jnp.exp2(s_sink - mx), 0.0)
                e_aug = jnp.concatenate([e, e_sink], axis=0).astype(BF16)
                acc = acc + lax.dot_general(e_aug, v_aug[parity], TN_DIMS, preferred_element_type=F32)
            o_ref[:, c * LANES:(c + 1) * LANES] = (acc[:, :LANES] / acc[:, LANES:]).astype(o_ref.dtype)

    row = lax.broadcasted_iota(jnp.int32, (w, w), 0)
    col = lax.broadcasted_iota(jnp.int32, (w, w), 1)
    causal = row >= col
    u0 = A_WIDTH
    v0 = A_WIDTH + B_WIDTH
    for g in range(B_GROUPS):
        sl = slice(g * B_GROUP_DIM, (g + 1) * B_GROUP_DIM)
        x = qug_ref[:, v0 + g * B_GROUP_DIM:v0 + (g + 1) * B_GROUP_DIM]
        mu = jnp.mean(x, axis=-1, keepdims=True)
        d = x - mu
        var = jnp.mean(d * d, axis=-1, keepdims=True)
        vn = d * lax.rsqrt(var + EPS) * lng_ref[:, sl] + lnb_ref[:, sl]
        wg = jnp.where(causal, ws_ref[g], 0.0).astype(BF16)
        sg = _dot(wg, vn.astype(BF16)) + bs_ref[:, g:g + 1]
        o_ref[:, u0 + g * B_GROUP_DIM:u0 + (g + 1) * B_GROUP_DIM] = (
            qug_ref[:, u0 + g * B_GROUP_DIM:u0 + (g + 1) * B_GROUP_DIM] * sg).astype(o_ref.dtype)


def _ab_mixer(proj, batch, seq, q_gain, k_gain, sinks, ln_g, ln_b, w_s, b_s):
    nb = seq // WINDOW
    kv_blk = (A_WIDTH + 2 * B_WIDTH) // (2 * A_KV_WIDTH)
    row_blk = lambda b, n: b * nb + n
    return pl.pallas_call(
        _ab_mixer_kernel,
        grid=(batch, nb),
        in_specs=[
            pl.BlockSpec(memory_space=pltpu.SMEM),
            pl.BlockSpec((WINDOW, A_WIDTH + 2 * B_WIDTH), lambda b, n: (row_blk(b, n), 0)),
            pl.BlockSpec((WINDOW, 2 * A_KV_WIDTH), lambda b, n: (b * nb + jnp.maximum(n - 1, 0), kv_blk)),
            pl.BlockSpec((WINDOW, 2 * A_KV_WIDTH), lambda b, n: (row_blk(b, n), kv_blk)),
            pl.BlockSpec((1, A_HEAD_DIM), lambda b, n: (0, 0)),
            pl.BlockSpec((1, A_HEAD_DIM), lambda b, n: (0, 0)),
            pl.BlockSpec((1, B_WIDTH), lambda b, n: (0, 0)),
            pl.BlockSpec((1, B_WIDTH), lambda b, n: (0, 0)),
            pl.BlockSpec((B_GROUPS, WINDOW, WINDOW), lambda b, n: (0, 0, 0)),
            pl.BlockSpec((WINDOW, B_GROUPS), lambda b, n: (0, 0)),
        ],
        out_specs=pl.BlockSpec((WINDOW, A_WIDTH + B_WIDTH), lambda b, n: (row_blk(b, n), 0)),
        out_shape=jax.ShapeDtypeStruct((batch * seq, A_WIDTH + B_WIDTH), BF16),
        compiler_params=_params(("parallel", "arbitrary"), VMEM_SMALL_MIB),
        name="ab_mixer",
    )(sinks, proj, proj, proj,
      q_gain.reshape(1, A_HEAD_DIM), k_gain.reshape(1, A_HEAD_DIM),
      ln_g.reshape(1, B_WIDTH), ln_b.reshape(1, B_WIDTH), w_s, b_s.T)


def _hgrn2_kernel(og_ref, qvg_ref, lf_ref, k_ref, o_ref, st_ref):
    c = C_CHUNK
    hw = C_HEADS_PER_STEP * C_DIM

    @pl.when(pl.program_id(2) == 0)
    def _():
        st_ref[...] = jnp.zeros_like(st_ref)

    row = lax.broadcasted_iota(jnp.int32, (c, c), 0)
    col = lax.broadcasted_iota(jnp.int32, (c, c), 1)
    row1 = lax.broadcasted_iota(jnp.int32, (c, 1), 0)
    cumsum_mat = (row >= col).astype(BF16)
    diff_bits = row ^ col
    level = jnp.zeros((c, c), jnp.int32)
    for b in range(1, c.bit_length() - 1):
        level = level + (diff_bits >= (1 << b)).astype(jnp.int32)
    level = jnp.where(col < row, level, jnp.where(col == row, -1, -2))
    og = og_ref[...]

    for h in range(C_HEADS_PER_STEP):
        sl = slice(h * C_DIM, (h + 1) * C_DIM)
        log2_f = lf_ref[:, sl]
        k = k_ref[:, sl].astype(F32)

        p1 = log2_f.astype(BF16)
        p2 = (log2_f - p1.astype(F32)).astype(BF16)
        cg = _dot(cumsum_mat, p1) + _dot(cumsum_mat, p2)

        q = qvg_ref[:, sl].astype(F32)
        v = qvg_ref[:, hw + h * C_DIM:hw + (h + 1) * C_DIM]
        st = st_ref[h]
        o = lax.dot_general((q * jnp.exp2(cg)).astype(BF16), st.astype(BF16), NT_DIMS,
                            preferred_element_type=F32)

        att = jnp.where(level == -1, jnp.sum(q * k, axis=-1, keepdims=True), 0.0)
        part = lax.dot_general((q * jnp.exp2(log2_f)).astype(BF16), k_ref[:, sl], NT_DIMS,
                               preferred_element_type=F32)
        att = jnp.where(level == 0, part, att)
        cg3 = cg.reshape(c // SUBLANES, SUBLANES, C_DIM)
        sub_row = lambda r: jnp.broadcast_to(cg3[:, r:r + 1, :], cg3.shape).reshape(c, C_DIM)
        n = 2
        while n < c:
            if 4 * n == SUBLANES:
                ref = jnp.where((row1 & (2 * n)) != 0, sub_row(3 * n - 1), sub_row(n - 1))
            elif 2 * n == SUBLANES:
                ref = sub_row(n - 1)
            else:
                ref = jnp.concatenate([jnp.broadcast_to(cg[lo + n - 1:lo + n, :], (2 * n, C_DIM))
                                       for lo in range(0, c, 2 * n)], axis=0)
            e = jnp.exp2(-jnp.abs(cg - ref))
            part = lax.dot_general((q * e).astype(BF16), (k * e).astype(BF16), NT_DIMS,
                                   preferred_element_type=F32)
            att = jnp.where(level == n.bit_length() - 1, part, att)
            n *= 2

        o = o + _dot(att.astype(BF16), v)

        last = cg[c - 1:c, :]
        kd = (k * jnp.exp2(last - cg)).astype(BF16)
        st_ref[h] = st * jnp.exp2(last) + lax.dot_general(v, kd, TN_DIMS, preferred_element_type=F32)

        y = o * lax.rsqrt(jnp.mean(o * o, axis=-1, keepdims=True) + EPS) * og
        gate = qvg_ref[:, 2 * hw + h * C_DIM:2 * hw + (h + 1) * C_DIM]
        o_ref[:, sl] = (y * gate.astype(F32)).astype(o_ref.dtype)


def _hgrn2(qvg, log2_f, k, o_gain, batch, seq):
    nc = seq // C_CHUNK
    hw = C_HEADS_PER_STEP * C_DIM
    nh = C_WIDTH // hw
    blk = lambda width: pl.BlockSpec((C_CHUNK, width), lambda b, g, c: (b * nc + c, g))
    return pl.pallas_call(
        _hgrn2_kernel,
        grid=(batch, nh, nc),
        in_specs=[pl.BlockSpec((1, C_DIM), lambda b, g, c: (0, 0)),
                  blk(3 * hw), blk(hw), blk(hw)],
        out_specs=blk(hw),
        out_shape=jax.ShapeDtypeStruct((batch * seq, C_WIDTH), BF16),
        scratch_shapes=[pltpu.VMEM((C_HEADS_PER_STEP, C_DIM, C_DIM), F32)],
        compiler_params=_params(("parallel", "parallel", "arbitrary"), VMEM_SMALL_MIB),
        name="hgrn2",
    )(o_gain.reshape(1, C_DIM), qvg, log2_f, k)


def _ffn(x, norm_g, w1, w3, w2, layer):
    h, w2b = _ffn_up(_rmsnorm(x, norm_g[layer]), w1, w3, w2, layer)
    return _matmul_res(h, w2b, x, tk=D_FF_PAD // 4)


def kernel(x, ffn1_norm, ffn1_w1, ffn1_w3, ffn1_w2, mix_norm, ffn2_norm, ffn2_w1, ffn2_w3, ffn2_w2, ab_w_in, ab_q_norm, ab_k_norm, ab_sinks, ab_v_ln_g, ab_v_ln_b, ab_w_s, ab_b_s, ab_w_out, c_w_in, c_lb_logits, c_o_norm, c_w_out):
    batch, seq, d = x.shape
    depth = ffn1_norm.shape[0]
    tn = 512
    tm2 = 1024
    x = x.reshape(batch * seq, d)
    for l in range(depth):
        x = _ffn(x, ffn1_norm, ffn1_w1, ffn1_w3, ffn1_w2, l)
        hn = _rmsnorm(x, mix_norm[l])
        j = l // 2
        if l % 2 == 0:
            kv_tile = A_WIDTH // tn
            n_tiles = AB_IN // tn
            o_tile = lambda t: jnp.where(t < kv_tile, t, jnp.where(t == kv_tile, n_tiles - 1, t - 1))
            proj = _wpipe_matmul("ab_in_proj", hn, ab_w_in, j, n_tiles, [F32], o_tile=o_tile,
                                 epilogue=functools.partial(_act_epilogue, act=_gelu, act_from=kv_tile + 1))
            mixed = _ab_mixer(proj, batch, seq, ab_q_norm[j], ab_k_norm[j], ab_sinks[j],
                              ab_v_ln_g[j], ab_v_ln_b[j], ab_w_s[j], ab_b_s[j])
            x = _wpipe_matmul("ab_out_proj", mixed, ab_w_out, j, d // tn, [F32], _residual_epilogue,
                              tiled_extra=[x], tm=tm2)
        else:
            ct = C_WIDTH // tn
            gt = C_HEADS_PER_STEP * C_DIM // tn
            qvg = _wpipe_matmul("c_in_proj_qvg", hn, c_w_in, j, 3 * ct, [BF16],
                                functools.partial(_act_epilogue, act=_silu, act_from=2 * ct),
                                w_tile=lambda t: jnp.where(t < ct, t, t + ct),
                                o_tile=lambda t: (t % ct) // gt * (3 * gt) + t // ct * gt + t % gt)
            log2_f, k = _wpipe_matmul("c_in_proj_f", hn, c_w_in, j, ct, [F32, BF16],
                                      functools.partial(_forget_gate_epilogue, layer=l),
                                      w_tile=lambda t: t + ct, col_extra=[c_lb_logits], tm=tm2)
            mixed = _hgrn2(qvg, log2_f, k, c_o_norm[j], batch, seq)
            x = _wpipe_matmul("c_out_proj", mixed, c_w_out, j, d // tn, [F32], _residual_epilogue,
                              tiled_extra=[x], tm=tm2)
        x = _ffn(x, ffn2_norm, ffn2_w1, ffn2_w3, ffn2_w2, l)
    return x.reshape(batch, seq, d)
```

```python
import functools

import jax
import jax.numpy as jnp
from jax import lax
from jax.experimental import pallas as pl
from jax.experimental.pallas import tpu as pltpu

F32 = jnp.float32
BF16 = jnp.bfloat16

D_FF = 11008
FF_TILE = 256
D_FF_PAD = 11264
EPS = 1e-6

A_HEADS = 32
A_KV_HEADS = 4
A_GROUP = A_HEADS // A_KV_HEADS
A_HEAD_DIM = 64
WINDOW = 128
A_WIDTH = A_HEADS * A_HEAD_DIM
A_KV_WIDTH = A_KV_HEADS * A_HEAD_DIM
B_GROUPS = 16
B_GROUP_DIM = 128
B_WIDTH = B_GROUPS * B_GROUP_DIM
AB_IN = A_WIDTH + 2 * A_KV_WIDTH + 2 * B_WIDTH

C_HEADS = 32
C_DIM = 128
C_WIDTH = C_HEADS * C_DIM
C_CHUNK = 128
C_HEADS_PER_STEP = 32

LANES = 128
SUBLANES = 8
MIB = 1 << 20
DOT_ROWS = 1024
VMEM_SMALL_MIB = 32
VMEM_MATMUL_MIB = 56
VMEM_PIPELINED_MIB = 60
LOG2E = 1.4426950408889634

NT_DIMS = (((1,), (1,)), ((), ()))
TN_DIMS = (((0,), (0,)), ((), ()))


def _params(semantics, vmem_mib):
    return pltpu.CompilerParams(dimension_semantics=semantics, vmem_limit_bytes=vmem_mib * MIB)


def _dot(a, b):
    return jnp.dot(a, b, preferred_element_type=F32)


def _dot_rows(a_ref, rows, w_ref):
    return _dot(a_ref[rows, :], w_ref[...])


def _gelu(y):
    return 0.5 * y * (1.0 + lax.erf(y * (2.0 ** -0.5)))


def _silu(y):
    half = 0.5 * y
    return half + half * jnp.tanh(half)


def _rmsnorm_kernel(x_ref, g_ref, o_ref):
    x = x_ref[...]
    r = lax.rsqrt(jnp.mean(x * x, axis=-1, keepdims=True) + EPS)
    o_ref[...] = (x * r * g_ref[...]).astype(o_ref.dtype)


def _rmsnorm(x, g, rows=256):
    m, d = x.shape
    return pl.pallas_call(
        _rmsnorm_kernel,
        grid=(m // rows,),
        in_specs=[pl.BlockSpec((rows, d), lambda i: (i, 0)),
                  pl.BlockSpec((1, d), lambda i: (0, 0))],
        out_specs=pl.BlockSpec((rows, d), lambda i: (i, 0)),
        out_shape=jax.ShapeDtypeStruct((m, d), BF16),
        compiler_params=_params(("parallel",), VMEM_SMALL_MIB),
        name="rmsnorm",
    )(x, g.reshape(1, d))


def _store_epilogue(j, chunks):
    for dot, _, out_refs in chunks:
        out_refs[0][...] = dot().astype(out_refs[0].dtype)


def _residual_epilogue(j, chunks):
    for dot, extra_refs, out_refs in chunks:
        out_refs[0][...] = extra_refs[0][...] + dot()


def _act_epilogue(j, chunks, *, act, act_from):
    @pl.when(j >= act_from)
    def _():
        for dot, _, out_refs in chunks:
            out_refs[0][...] = act(dot()).astype(out_refs[0].dtype)

    @pl.when(j < act_from)
    def _():
        _store_epilogue(j, chunks)


def _forget_gate_epilogue(j, chunks, *, layer):
    logits = chunks[0][1][0][...]
    lmax = jnp.max(logits, axis=0, keepdims=True)
    ex = jnp.exp(logits - lmax)
    prob = ex / jnp.sum(ex, axis=0, keepdims=True)
    lb = jnp.sum(prob[:layer + 1], axis=0, keepdims=True) - prob[0:1]
    c1 = 0.5 * (1.0 - lb)
    c0 = lb + c1
    for dot, _, out_refs in chunks:
        scaled = c1 * jnp.tanh(0.5 * dot())
        out_refs[0][...] = jnp.log2(c0 + scaled)
        out_refs[1][...] = (c1 - scaled).astype(out_refs[1].dtype)


def _wpipe_matmul_kernel(*refs, n_tiled, n_col, epilogue):
    a_ref, w_ref = refs[:2]
    tiled_refs = refs[2:2 + n_tiled]
    col_refs = refs[2 + n_tiled:2 + n_tiled + n_col]
    out_refs = refs[2 + n_tiled + n_col:-2]
    p = pl.program_id(0)
    slab = w_ref.shape[0]
    r0 = pl.multiple_of(pl.program_id(1) * slab, slab)

    def step(w_next_ref, w_cur_ref, compute):
        w_next_ref[pl.ds(r0, slab), :] = w_ref[...].astype(BF16)
        if compute:
            chunks = []
            for r in range(0, a_ref.shape[0], DOT_ROWS):
                rows = pl.ds(r, DOT_ROWS)
                chunks.append((functools.partial(_dot_rows, a_ref, rows, w_cur_ref),
                               [t.at[rows, :] for t in tiled_refs] + list(col_refs),
                               [o.at[rows, :] for o in out_refs]))
            epilogue(p - 1, chunks)

    even = (p % 2) == 0
    pl.when((p > 0) & even)(functools.partial(step, refs[-2], refs[-1], True))
    pl.when((p > 0) & jnp.logical_not(even))(functools.partial(step, refs[-1], refs[-2], True))
    pl.when(p == 0)(functools.partial(step, refs[-2], refs[-1], False))


def _wpipe_matmul(name, a, w, layer, n_tiles, out_dtypes, epilogue=_store_epilogue, w_tile=lambda j: j,
                  o_tile=lambda j: j, tiled_extra=(), col_extra=(), tm=2048, tn=512):
    m, k = a.shape
    n_i = m // tm
    next_tile = lambda p: w_tile(jnp.minimum(p, n_tiles - 1))
    cur_tile = lambda p: o_tile(jnp.maximum(p - 1, 0))
    row_block = lambda p, i: jnp.where(p == 0, 0, i)
    out_block = pl.BlockSpec((tm, tn), lambda p, i: (row_block(p, i), cur_tile(p)))
    in_specs = [pl.BlockSpec((tm, k), lambda p, i: (row_block(p, i), 0)),
                pl.BlockSpec((None, k // n_i, tn), lambda p, i: (layer, i, next_tile(p)))]
    in_specs += [out_block for _ in tiled_extra]
    in_specs += [pl.BlockSpec((e.shape[0], tn), lambda p, i: (0, cur_tile(p))) for e in col_extra]
    wb = pltpu.VMEM((k, tn), BF16)
    outs = pl.pallas_call(
        functools.partial(_wpipe_matmul_kernel, n_tiled=len(tiled_extra), n_col=len(col_extra), epilogue=epilogue),
        grid=(n_tiles + 1, n_i),
        in_specs=in_specs,
        out_specs=[out_block for _ in out_dtypes],
        out_shape=[jax.ShapeDtypeStruct((m, n_tiles * tn), dt) for dt in out_dtypes],
        scratch_shapes=[wb, wb],
        compiler_params=_params(("arbitrary", "arbitrary"), VMEM_PIPELINED_MIB),
        name=name,
    )(a, w, *tiled_extra, *col_extra)
    return outs if len(outs) > 1 else outs[0]


def _ffn_up_kernel(a_ref, w1_ref, w3_ref, w2_ref, h_ref, w2b_ref, w1e_ref, w3e_ref, w1o_ref, w3o_ref, *, n_tiles):
    p = pl.program_id(0)
    i = pl.program_id(1)
    t = p - 1
    slab = w1_ref.shape[0]
    r0 = pl.multiple_of(i * slab, slab)
    slab2 = w2_ref.shape[0]
    q0 = pl.multiple_of(i * slab2, slab2)

    def compute(w1_next_ref, w3_next_ref, w1_cur_ref, w3_cur_ref):
        w1_next_ref[pl.ds(r0, slab), :] = w1_ref[...].astype(BF16)
        w3_next_ref[pl.ds(r0, slab), :] = w3_ref[...].astype(BF16)
        for r in range(0, a_ref.shape[0], DOT_ROWS):
            rows = pl.ds(r, DOT_ROWS)
            a = a_ref[rows, :]
            y1 = _dot(a, w1_cur_ref[...])
            y3 = _dot(a, w3_cur_ref[...])
            h_ref[rows, :] = (_silu(y1) * (0.5 * y3)).astype(h_ref.dtype)
        w2b_ref[pl.ds(q0, slab2), :] = w2_ref[...].astype(BF16)

    real = (t >= 0) & (t < n_tiles)
    even = (p % 2) == 0
    pl.when(real & even)(functools.partial(compute, w1e_ref, w3e_ref, w1o_ref, w3o_ref))
    pl.when(real & jnp.logical_not(even))(functools.partial(compute, w1o_ref, w3o_ref, w1e_ref, w3e_ref))

    @pl.when(p == 0)
    def _():
        w1e_ref[pl.ds(r0, slab), :] = w1_ref[...].astype(BF16)
        w3e_ref[pl.ds(r0, slab), :] = w3_ref[...].astype(BF16)

    @pl.when(t == n_tiles)
    def _():
        h_ref[...] = jnp.zeros_like(h_ref)
        w2b_ref[pl.ds(q0, slab2), :] = jnp.zeros((slab2, w2b_ref.shape[1]), BF16)


def _ffn_up(a, w1, w3, w2, layer, tm=2048):
    m, k = a.shape
    tf = FF_TILE
    n_tiles = D_FF // tf
    n_i = m // tm
    n_pos = D_FF_PAD // tf + 1
    next_tile = lambda p: jnp.minimum(p, n_tiles - 1)
    cur_tile = lambda p: jnp.clip(p - 1, 0, n_tiles - 1)
    out_tile = lambda p: jnp.maximum(p - 1, 0)
    a_block = lambda p, i: jnp.where(p == 0, 0, jnp.where(p == n_pos - 1, n_i - 1, i))
    h_block = lambda p, i: jnp.where(p == 0, 0, i)
    wb = pltpu.VMEM((k, tf), BF16)
    return pl.pallas_call(
        functools.partial(_ffn_up_kernel, n_tiles=n_tiles),
        grid=(n_pos, n_i),
        in_specs=[pl.BlockSpec((tm, k), lambda p, i: (a_block(p, i), 0)),
                  pl.BlockSpec((None, k // n_i, tf), lambda p, i: (layer, i, next_tile(p))),
                  pl.BlockSpec((None, k // n_i, tf), lambda p, i: (layer, i, next_tile(p))),
                  pl.BlockSpec((None, tf // n_i, k), lambda p, i: (layer, cur_tile(p) * n_i + i, 0))],
        out_specs=[pl.BlockSpec((tm, tf), lambda p, i: (h_block(p, i), out_tile(p))),
                   pl.BlockSpec((tf, k), lambda p, i: (out_tile(p), 0))],
        out_shape=[jax.ShapeDtypeStruct((m, D_FF_PAD), BF16),
                   jax.ShapeDtypeStruct((D_FF_PAD, k), BF16)],
        scratch_shapes=[wb, wb, wb, wb],
        compiler_params=_params(("arbitrary", "arbitrary"), VMEM_PIPELINED_MIB),
        name="ffn_up",
    )(a, w1, w3, w2)


def _matmul_res_kernel(a_ref, w_ref, r_ref, o_ref):
    base = jnp.where(pl.program_id(2) == 0, r_ref[...], o_ref[...])
    o_ref[...] = base + _dot(a_ref[...], w_ref[...])


def _matmul_res(a, w, res, tk, tm=1024, tn=1024):
    m, k = a.shape
    n = w.shape[1]
    return pl.pallas_call(
        _matmul_res_kernel,
        grid=(m // tm, n // tn, k // tk),
        in_specs=[pl.BlockSpec((tm, tk), lambda i, j, kk: (i, kk)),
                  pl.BlockSpec((tk, tn), lambda i, j, kk: (kk, j)),
                  pl.BlockSpec((tm, tn), lambda i, j, kk: (i, j))],
        out_specs=pl.BlockSpec((tm, tn), lambda i, j, kk: (i, j)),
        out_shape=jax.ShapeDtypeStruct((m, n), F32),
        compiler_params=_params(("parallel", "parallel", "arbitrary"), VMEM_MATMUL_MIB),
        name="matmul_res",
    )(a, w, res)


def _ab_mixer_kernel(sinks_ref, qug_ref, kvp_ref, kvc_ref, qg_ref, kg_ref,
                     lng_ref, lnb_ref, ws_ref, bs_ref, o_ref):
    n = pl.program_id(1)
    w = WINDOW
    hd = A_HEAD_DIM

    key = lax.broadcasted_iota(jnp.int32, (2 * w, w), 0)
    qry = lax.broadcasted_iota(jnp.int32, (2 * w, w), 1)
    dist = qry + w - key
    visible = (dist >= 0) & (dist < w) & ((key >= w) | (n > 0))
    low = lax.broadcasted_iota(jnp.int32, (1, LANES), 1) < hd
    sub8 = lax.broadcasted_iota(jnp.int32, (SUBLANES, LANES), 0)
    ones_low = jnp.where(low, 1.0, 0.0)
    ones_high = 1.0 - ones_low
    sel2 = jnp.where(sub8 < SUBLANES // 2, ones_low, ones_high).astype(BF16)
    sink_row = sub8 == 0

    def split_bf16(x):
        hi = x.astype(BF16)
        return hi, (x - hi.astype(F32)).astype(BF16)

    kk = jnp.concatenate([kvp_ref[:, :A_KV_WIDTH], kvc_ref[:, :A_KV_WIDTH]], axis=0)
    vv = jnp.concatenate([kvp_ref[:, A_KV_WIDTH:], kvc_ref[:, A_KV_WIDTH:]], axis=0)
    seg_r = lax.broadcasted_iota(jnp.int32, (A_KV_WIDTH, A_KV_WIDTH), 0) // hd
    seg_c = lax.broadcasted_iota(jnp.int32, (A_KV_WIDTH, A_KV_WIDTH), 1) // hd
    seg = (seg_r == seg_c).astype(BF16)
    hi, lo = split_bf16(kk * kk)
    kss = _dot(hi, seg) + _dot(lo, seg)
    kgain = jnp.concatenate([kg_ref[...]] * A_KV_HEADS, axis=1)
    khat = kk * lax.rsqrt(kss * (1.0 / hd) + EPS) * kgain
    qgain = jnp.concatenate([qg_ref[...]] * 2, axis=1) * (hd ** -0.5 * LOG2E)

    def head_pair_operands(x, h):
        tile = x[:, (h // 2) * LANES:(h // 2 + 1) * LANES]
        swapped = pltpu.roll(tile, hd, axis=1)
        in_low, in_high = (tile, swapped) if h % 2 == 0 else (swapped, tile)
        return jnp.where(low, in_low, 0.0), jnp.where(low, 0.0, in_high)

    for h in range(A_KV_HEADS):
        k_low, k_high = head_pair_operands(khat, h)
        v_low, v_high = head_pair_operands(vv, h)
        k_pad = (k_low.astype(BF16), k_high.astype(BF16))
        v_aug = tuple(
            jnp.concatenate([
                jnp.concatenate([v_p, jnp.broadcast_to(ones_p, v_p.shape)], axis=1),
                jnp.concatenate([jnp.zeros((SUBLANES, LANES), F32), jnp.broadcast_to(ones_p, (SUBLANES, LANES))], axis=1),
            ], axis=0).astype(BF16)
            for v_p, ones_p in ((v_low, ones_low), (v_high, ones_high)))
        for c in range(h * A_GROUP // 2, (h + 1) * A_GROUP // 2):
            qc = qug_ref[:, c * LANES:(c + 1) * LANES]
            hi, lo = split_bf16(qc * qc)
            qss = (lax.dot_general(sel2, hi, NT_DIMS, preferred_element_type=F32)
                   + lax.dot_general(sel2, lo, NT_DIMS, preferred_element_type=F32))
            qscale = lax.rsqrt(qss * (1.0 / hd) + EPS)
            qg = (qc * qgain).astype(BF16)
            acc = jnp.zeros((w, 2 * LANES), F32)
            for parity in range(2):
                st = lax.dot_general(k_pad[parity], qg, NT_DIMS, preferred_element_type=F32)
                r = qscale[parity * (SUBLANES // 2):parity * (SUBLANES // 2) + 1]
                st = jnp.where(visible, st * r, -jnp.inf)
                s_sink = sinks_ref[2 * c + parity] * LOG2E
                mx = jnp.maximum(jnp.max(st, axis=0, keepdims=True), s_sink)
                e = jnp.exp2(st - mx)
                e_sink = jnp.where(sink_row, jnp.exp2(s_sink - mx), 0.0)
                e_aug = jnp.concatenate([e, e_sink], axis=0).astype(BF16)
                acc = acc + lax.dot_general(e_aug, v_aug[parity], TN_DIMS, preferred_element_type=F32)
            o_ref[:, c * LANES:(c + 1) * LANES] = (acc[:, :LANES] / acc[:, LANES:]).astype(o_ref.dtype)

    row = lax.broadcasted_iota(jnp.int32, (w, w), 0)
    col = lax.broadcasted_iota(jnp.int32, (w, w), 1)
    causal = row >= col
    u0 = A_WIDTH
    v0 = A_WIDTH + B_WIDTH
    for g in range(B_GROUPS):
        sl = slice(g * B_GROUP_DIM, (g + 1) * B_GROUP_DIM)
        x = qug_ref[:, v0 + g * B_GROUP_DIM:v0 + (g + 1) * B_GROUP_DIM]
        mu = jnp.mean(x, axis=-1, keepdims=True)
        d = x - mu
        var = jnp.mean(d * d, axis=-1, keepdims=True)
        vn = d * lax.rsqrt(var + EPS) * lng_ref[:, sl] + lnb_ref[:, sl]
        wg = jnp.where(causal, ws_ref[g], 0.0).astype(BF16)
        sg = _dot(wg, vn.astype(BF16)) + bs_ref[:, g:g + 1]
        o_ref[:, u0 + g * B_GROUP_DIM:u0 + (g + 1) * B_GROUP_DIM] = (
            qug_ref[:, u0 + g * B_GROUP_DIM:u0 + (g + 1) * B_GROUP_DIM] * sg).astype(o_ref.dtype)


def _ab_mixer(proj, batch, seq, q_gain, k_gain, sinks, ln_g, ln_b, w_s, b_s):
    nb = seq // WINDOW
    kv_blk = (A_WIDTH + 2 * B_WIDTH) // (2 * A_KV_WIDTH)
    row_blk = lambda b, n: b * nb + n
    return pl.pallas_call(
        _ab_mixer_kernel,
        grid=(batch, nb),
        in_specs=[
            pl.BlockSpec(memory_space=pltpu.SMEM),
            pl.BlockSpec((WINDOW, A_WIDTH + 2 * B_WIDTH), lambda b, n: (row_blk(b, n), 0)),
            pl.BlockSpec((WINDOW, 2 * A_KV_WIDTH), lambda b, n: (b * nb + jnp.maximum(n - 1, 0), kv_blk)),
            pl.BlockSpec((WINDOW, 2 * A_KV_WIDTH), lambda b, n: (row_blk(b, n), kv_blk)),
            pl.BlockSpec((1, A_HEAD_DIM), lambda b, n: (0, 0)),
            pl.BlockSpec((1, A_HEAD_DIM), lambda b, n: (0, 0)),
            pl.BlockSpec((1, B_WIDTH), lambda b, n: (0, 0)),
            pl.BlockSpec((1, B_WIDTH), lambda b, n: (0, 0)),
            pl.BlockSpec((B_GROUPS, WINDOW, WINDOW), lambda b, n: (0, 0, 0)),
            pl.BlockSpec((WINDOW, B_GROUPS), lambda b, n: (0, 0)),
        ],
        out_specs=pl.BlockSpec((WINDOW, A_WIDTH + B_WIDTH), lambda b, n: (row_blk(b, n), 0)),
        out_shape=jax.ShapeDtypeStruct((batch * seq, A_WIDTH + B_WIDTH), BF16),
        compiler_params=_params(("parallel", "arbitrary"), VMEM_SMALL_MIB),
        name="ab_mixer",
    )(sinks, proj, proj, proj,
      q_gain.reshape(1, A_HEAD_DIM), k_gain.reshape(1, A_HEAD_DIM),
      ln_g.reshape(1, B_WIDTH), ln_b.reshape(1, B_WIDTH), w_s, b_s.T)


def _hgrn2_kernel(og_ref, qvg_ref, lf_ref, k_ref, o_ref, st_ref):
    c = C_CHUNK
    hw = C_HEADS_PER_STEP * C_DIM

    @pl.when(pl.program_id(2) == 0)
    def _():
        st_ref[...] = jnp.zeros_like(st_ref)

    row = lax.broadcasted_iota(jnp.int32, (c, c), 0)
    col = lax.broadcasted_iota(jnp.int32, (c, c), 1)
    row1 = lax.broadcasted_iota(jnp.int32, (c, 1), 0)
    cumsum_mat = (row >= col).astype(BF16)
    diff_bits = row ^ col
    level = jnp.zeros((c, c), jnp.int32)
    for b in range(1, c.bit_length() - 1):
        level = level + (diff_bits >= (1 << b)).astype(jnp.int32)
    level = jnp.where(col < row, level, jnp.where(col == row, -1, -2))
    og = og_ref[...]

    for h in range(C_HEADS_PER_STEP):
        sl = slice(h * C_DIM, (h + 1) * C_DIM)
        log2_f = lf_ref[:, sl]
        k = k_ref[:, sl].astype(F32)

        p1 = log2_f.astype(BF16)
        p2 = (log2_f - p1.astype(F32)).astype(BF16)
        cg = _dot(cumsum_mat, p1) + _dot(cumsum_mat, p2)

        q = qvg_ref[:, sl].astype(F32)
        v = qvg_ref[:, hw + h * C_DIM:hw + (h + 1) * C_DIM]
        st = st_ref[h]
        o = lax.dot_general((q * jnp.exp2(cg)).astype(BF16), st.astype(BF16), NT_DIMS,
                            preferred_element_type=F32)

        att = jnp.where(level == -1, jnp.sum(q * k, axis=-1, keepdims=True), 0.0)
        part = lax.dot_general((q * jnp.exp2(log2_f)).astype(BF16), k_ref[:, sl], NT_DIMS,
                               preferred_element_type=F32)
        att = jnp.where(level == 0, part, att)
        cg3 = cg.reshape(c // SUBLANES, SUBLANES, C_DIM)
        sub_row = lambda r: jnp.broadcast_to(cg3[:, r:r + 1, :], cg3.shape).reshape(c, C_DIM)
        n = 2
        while n < c:
            if 4 * n == SUBLANES:
                ref = jnp.where((row1 & (2 * n)) != 0, sub_row(3 * n - 1), sub_row(n - 1))
            elif 2 * n == SUBLANES:
                ref = sub_row(n - 1)
            else:
                ref = jnp.concatenate([jnp.broadcast_to(cg[lo + n - 1:lo + n, :], (2 * n, C_DIM))
                                       for lo in range(0, c, 2 * n)], axis=0)
            e = jnp.exp2(-jnp.abs(cg - ref))
            part = lax.dot_general((q * e).astype(BF16), (k * e).astype(BF16), NT_DIMS,
                                   preferred_element_type=F32)
            att = jnp.where(level == n.bit_length() - 1, part, att)
            n *= 2

        o = o + _dot(att.astype(BF16), v)

        last = cg[c - 1:c, :]
        kd = (k * jnp.exp2(last - cg)).astype(BF16)
        st_ref[h] = st * jnp.exp2(last) + lax.dot_general(v, kd, TN_DIMS, preferred_element_type=F32)

        y = o * lax.rsqrt(jnp.mean(o * o, axis=-1, keepdims=True) + EPS) * og
        gate = qvg_ref[:, 2 * hw + h * C_DIM:2 * hw + (h + 1) * C_DIM]
        o_ref[:, sl] = (y * gate.astype(F32)).astype(o_ref.dtype)


def _hgrn2(qvg, log2_f, k, o_gain, batch, seq):
    nc = seq // C_CHUNK
    hw = C_HEADS_PER_STEP * C_DIM
    nh = C_WIDTH // hw
    blk = lambda width: pl.BlockSpec((C_CHUNK, width), lambda b, g, c: (b * nc + c, g))
    return pl.pallas_call(
        _hgrn2_kernel,
        grid=(batch, nh, nc),
        in_specs=[pl.BlockSpec((1, C_DIM), lambda b, g, c: (0, 0)),
                  blk(3 * hw), blk(hw), blk(hw)],
        out_specs=blk(hw),
        out_shape=jax.ShapeDtypeStruct((batch * seq, C_WIDTH), BF16),
        scratch_shapes=[pltpu.VMEM((C_HEADS_PER_STEP, C_DIM, C_DIM), F32)],
        compiler_params=_params(("parallel", "parallel", "arbitrary"), VMEM_SMALL_MIB),
        name="hgrn2",
    )(o_gain.reshape(1, C_DIM), qvg, log2_f, k)


def _ffn(x, norm_g, w1, w3, w2, layer):
    h, w2b = _ffn_up(_rmsnorm(x, norm_g[layer]), w1, w3, w2, layer)
    return _matmul_res(h, w2b, x, tk=D_FF_PAD // 4)


def kernel(x, ffn1_norm, ffn1_w1, ffn1_w3, ffn1_w2, mix_norm, ffn2_norm, ffn2_w1, ffn2_w3, ffn2_w2, ab_w_in, ab_q_norm, ab_k_norm, ab_sinks, ab_v_ln_g, ab_v_ln_b, ab_w_s, ab_b_s, ab_w_out, c_w_in, c_lb_logits, c_o_norm, c_w_out):
    batch, seq, d = x.shape
    depth = ffn1_norm.shape[0]
    tn = 512
    tm2 = 1024
    x = x.reshape(batch * seq, d)
    for l in range(depth):
        x = _ffn(x, ffn1_norm, ffn1_w1, ffn1_w3, ffn1_w2, l)
        hn = _rmsnorm(x, mix_norm[l])
        j = l // 2
        if l % 2 == 0:
            kv_tile = A_WIDTH // tn
            n_tiles = AB_IN // tn
            o_tile = lambda t: jnp.where(t < kv_tile, t, jnp.where(t == kv_tile, n_tiles - 1, t - 1))
            proj = _wpipe_matmul("ab_in_proj", hn, ab_w_in, j, n_tiles, [F32], o_tile=o_tile,
                                 epilogue=functools.partial(_act_epilogue, act=_gelu, act_from=kv_tile + 1))
            mixed = _ab_mixer(proj, batch, seq, ab_q_norm[j], ab_k_norm[j], ab_sinks[j],
                              ab_v_ln_g[j], ab_v_ln_b[j], ab_w_s[j], ab_b_s[j])
            x = _wpipe_matmul("ab_out_proj", mixed, ab_w_out, j, d // tn, [F32], _residual_epilogue,
                              tiled_extra=[x], tm=tm2)
        else:
            ct = C_WIDTH // tn
            gt = C_HEADS_PER_STEP * C_DIM // tn
            qvg = _wpipe_matmul("c_in_proj_qvg", hn, c_w_in, j, 3 * ct, [BF16],
                                functools.partial(_act_epilogue, act=_silu, act_from=2 * ct),
                                w_tile=lambda t: jnp.where(t < ct, t, t + ct),
                                o_tile=lambda t: (t % ct) // gt * (3 * gt) + t // ct * gt + t % gt)
            log2_f, k = _wpipe_matmul("c_in_proj_f", hn, c_w_in, j, ct, [F32, BF16],
                                      functools.partial(_forget_gate_epilogue, layer=l),
                                      w_tile=lambda t: t + ct, col_extra=[c_lb_logits], tm=tm2)
            mixed = _hgrn2(qvg, log2_f, k, c_o_norm[j], batch, seq)
            x = _wpipe_matmul("c_out_proj", mixed, c_w_out, j, d // tn, [F32], _residual_epilogue,
                              tiled_extra=[x], tm=tm2)
        x = _ffn(x, ffn2_norm, ffn2_w1, ffn2_w3, ffn2_w2, l)
    return x.reshape(batch, seq, d)
```

```python
import functools

import jax
import jax.numpy as jnp
from jax import lax
from jax.experimental import pallas as pl
from jax.experimental.pallas import tpu as pltpu

F32 = jnp.float32
BF16 = jnp.bfloat16

D_FF = 11008
FF_TILE = 256
D_FF_PAD = 11264
EPS = 1e-6

A_HEADS = 32
A_KV_HEADS = 4
A_GROUP = A_HEADS // A_KV_HEADS
A_HEAD_DIM = 64
WINDOW = 128
A_WIDTH = A_HEADS * A_HEAD_DIM
A_KV_WIDTH = A_KV_HEADS * A_HEAD_DIM
B_GROUPS = 16
B_GROUP_DIM = 128
B_WIDTH = B_GROUPS * B_GROUP_DIM
AB_IN = A_WIDTH + 2 * A_KV_WIDTH + 2 * B_WIDTH

C_HEADS = 32
C_DIM = 128
C_WIDTH = C_HEADS * C_DIM
C_CHUNK = 128
C_HEADS_PER_STEP = 32

LANES = 128
SUBLANES = 8
MIB = 1 << 20
DOT_ROWS = 1024
VMEM_SMALL_MIB = 32
VMEM_MATMUL_MIB = 56
VMEM_PIPELINED_MIB = 60
LOG2E = 1.4426950408889634

NT_DIMS = (((1,), (1,)), ((), ()))
TN_DIMS = (((0,), (0,)), ((), ()))


def _params(semantics, vmem_mib):
    return pltpu.CompilerParams(dimension_semantics=semantics, vmem_limit_bytes=vmem_mib * MIB)


def _dot(a, b):
    return jnp.dot(a, b, preferred_element_type=F32)


def _dot_rows(a_ref, rows, w_ref):
    return _dot(a_ref[rows, :], w_ref[...])


def _gelu(y):
    return 0.5 * y * (1.0 + lax.erf(y * (2.0 ** -0.5)))


def _silu(y):
    half = 0.5 * y
    return half + half * jnp.tanh(half)


def _rmsnorm_kernel(x_ref, g_ref, o_ref):
    x = x_ref[...]
    r = lax.rsqrt(jnp.mean(x * x, axis=-1, keepdims=True) + EPS)
    o_ref[...] = (x * r * g_ref[...]).astype(o_ref.dtype)


def _rmsnorm(x, g, rows=512):
    m, d = x.shape
    return pl.pallas_call(
        _rmsnorm_kernel,
        grid=(m // rows,),
        in_specs=[pl.BlockSpec((rows, d), lambda i: (i, 0)),
                  pl.BlockSpec((1, d), lambda i: (0, 0))],
        out_specs=pl.BlockSpec((rows, d), lambda i: (i, 0)),
        out_shape=jax.ShapeDtypeStruct((m, d), BF16),
        compiler_params=_params(("parallel",), VMEM_SMALL_MIB),
        name="rmsnorm",
    )(x, g.reshape(1, d))


def _store_epilogue(j, chunks):
    for dot, _, out_refs in chunks:
        out_refs[0][...] = dot().astype(out_refs[0].dtype)


def _residual_epilogue(j, chunks):
    for dot, extra_refs, out_refs in chunks:
        out_refs[0][...] = extra_refs[0][...] + dot()


def _act_epilogue(j, chunks, *, act, act_from):
    @pl.when(j >= act_from)
    def _():
        for dot, _, out_refs in chunks:
            out_refs[0][...] = act(dot()).astype(out_refs[0].dtype)

    @pl.when(j < act_from)
    def _():
        _store_epilogue(j, chunks)


def _forget_gate_epilogue(j, chunks, *, layer):
    logits = chunks[0][1][0][...]
    lmax = jnp.max(logits, axis=0, keepdims=True)
    ex = jnp.exp(logits - lmax)
    prob = ex / jnp.sum(ex, axis=0, keepdims=True)
    lb = jnp.sum(prob[:layer + 1], axis=0, keepdims=True) - prob[0:1]
    c1 = 0.5 * (1.0 - lb)
    c0 = lb + c1
    for dot, _, out_refs in chunks:
        scaled = c1 * jnp.tanh(0.5 * dot())
        out_refs[0][...] = jnp.log2(c0 + scaled)
        out_refs[1][...] = (c1 - scaled).astype(out_refs[1].dtype)


def _wpipe_matmul_kernel(*refs, n_tiled, n_col, epilogue):
    a_ref, w_ref = refs[:2]
    tiled_refs = refs[2:2 + n_tiled]
    col_refs = refs[2 + n_tiled:2 + n_tiled + n_col]
    out_refs = refs[2 + n_tiled + n_col:-2]
    p = pl.program_id(0)
    slab = w_ref.shape[0]
    r0 = pl.multiple_of(pl.program_id(1) * slab, slab)

    def step(w_next_ref, w_cur_ref, compute):
        w_next_ref[pl.ds(r0, slab), :] = w_ref[...].astype(BF16)
        if compute:
            chunks = []
            for r in range(0, a_ref.shape[0], DOT_ROWS):
                rows = pl.ds(r, DOT_ROWS)
                chunks.append((functools.partial(_dot_rows, a_ref, rows, w_cur_ref),
                               [t.at[rows, :] for t in tiled_refs] + list(col_refs),
                               [o.at[rows, :] for o in out_refs]))
            epilogue(p - 1, chunks)

    even = (p % 2) == 0
    pl.when((p > 0) & even)(functools.partial(step, refs[-2], refs[-1], True))
    pl.when((p > 0) & jnp.logical_not(even))(functools.partial(step, refs[-1], refs[-2], True))
    pl.when(p == 0)(functools.partial(step, refs[-2], refs[-1], False))


def _wpipe_matmul(name, a, w, layer, n_tiles, out_dtypes, epilogue=_store_epilogue, w_tile=lambda j: j,
                  o_tile=lambda j: j, tiled_extra=(), col_extra=(), tm=2048, tn=512):
    m, k = a.shape
    n_i = m // tm
    next_tile = lambda p: w_tile(jnp.minimum(p, n_tiles - 1))
    cur_tile = lambda p: o_tile(jnp.maximum(p - 1, 0))
    row_block = lambda p, i: jnp.where(p == 0, 0, i)
    out_block = pl.BlockSpec((tm, tn), lambda p, i: (row_block(p, i), cur_tile(p)))
    in_specs = [pl.BlockSpec((tm, k), lambda p, i: (row_block(p, i), 0)),
                pl.BlockSpec((None, k // n_i, tn), lambda p, i: (layer, i, next_tile(p)))]
    in_specs += [out_block for _ in tiled_extra]
    in_specs += [pl.BlockSpec((e.shape[0], tn), lambda p, i: (0, cur_tile(p))) for e in col_extra]
    wb = pltpu.VMEM((k, tn), BF16)
    outs = pl.pallas_call(
        functools.partial(_wpipe_matmul_kernel, n_tiled=len(tiled_extra), n_col=len(col_extra), epilogue=epilogue),
        grid=(n_tiles + 1, n_i),
        in_specs=in_specs,
        out_specs=[out_block for _ in out_dtypes],
        out_shape=[jax.ShapeDtypeStruct((m, n_tiles * tn), dt) for dt in out_dtypes],
        scratch_shapes=[wb, wb],
        compiler_params=_params(("arbitrary", "arbitrary"), VMEM_PIPELINED_MIB),
        name=name,
    )(a, w, *tiled_extra, *col_extra)
    return outs if len(outs) > 1 else outs[0]


def _ffn_up_kernel(a_ref, w1_ref, w3_ref, w2_ref, h_ref, w2b_ref, w1e_ref, w3e_ref, w1o_ref, w3o_ref, *, n_tiles):
    p = pl.program_id(0)
    i = pl.program_id(1)
    t = p - 1
    slab = w1_ref.shape[0]
    r0 = pl.multiple_of(i * slab, slab)
    slab2 = w2_ref.shape[0]
    q0 = pl.multiple_of(i * slab2, slab2)

    def compute(w1_next_ref, w3_next_ref, w1_cur_ref, w3_cur_ref):
        w1_next_ref[pl.ds(r0, slab), :] = w1_ref[...].astype(BF16)
        w3_next_ref[pl.ds(r0, slab), :] = w3_ref[...].astype(BF16)
        for r in range(0, a_ref.shape[0], DOT_ROWS):
            rows = pl.ds(r, DOT_ROWS)
            a = a_ref[rows, :]
            y1 = _dot(a, w1_cur_ref[...])
            y3 = _dot(a, w3_cur_ref[...])
            h_ref[rows, :] = (_silu(y1) * (0.5 * y3)).astype(h_ref.dtype)
        w2b_ref[pl.ds(q0, slab2), :] = w2_ref[...].astype(BF16)

    real = (t >= 0) & (t < n_tiles)
    even = (p % 2) == 0
    pl.when(real & even)(functools.partial(compute, w1e_ref, w3e_ref, w1o_ref, w3o_ref))
    pl.when(real & jnp.logical_not(even))(functools.partial(compute, w1o_ref, w3o_ref, w1e_ref, w3e_ref))

    @pl.when(p == 0)
    def _():
        w1e_ref[pl.ds(r0, slab), :] = w1_ref[...].astype(BF16)
        w3e_ref[pl.ds(r0, slab), :] = w3_ref[...].astype(BF16)

    @pl.when(t == n_tiles)
    def _():
        h_ref[...] = jnp.zeros_like(h_ref)
        w2b_ref[pl.ds(q0, slab2), :] = jnp.zeros((slab2, w2b_ref.shape[1]), BF16)


def _ffn_up(a, w1, w3, w2, layer, tm=2048):
    m, k = a.shape
    tf = FF_TILE
    n_tiles = D_FF // tf
    n_i = m // tm
    n_pos = D_FF_PAD // tf + 1
    next_tile = lambda p: jnp.minimum(p, n_tiles - 1)
    cur_tile = lambda p: jnp.clip(p - 1, 0, n_tiles - 1)
    out_tile = lambda p: jnp.maximum(p - 1, 0)
    a_block = lambda p, i: jnp.where(p == 0, 0, jnp.where(p == n_pos - 1, n_i - 1, i))
    h_block = lambda p, i: jnp.where(p == 0, 0, i)
    wb = pltpu.VMEM((k, tf), BF16)
    return pl.pallas_call(
        functools.partial(_ffn_up_kernel, n_tiles=n_tiles),
        grid=(n_pos, n_i),
        in_specs=[pl.BlockSpec((tm, k), lambda p, i: (a_block(p, i), 0)),
                  pl.BlockSpec((None, k // n_i, tf), lambda p, i: (layer, i, next_tile(p))),
                  pl.BlockSpec((None, k // n_i, tf), lambda p, i: (layer, i, next_tile(p))),
                  pl.BlockSpec((None, tf // n_i, k), lambda p, i: (layer, cur_tile(p) * n_i + i, 0))],
        out_specs=[pl.BlockSpec((tm, tf), lambda p, i: (h_block(p, i), out_tile(p))),
                   pl.BlockSpec((tf, k), lambda p, i: (out_tile(p), 0))],
        out_shape=[jax.ShapeDtypeStruct((m, D_FF_PAD), BF16),
                   jax.ShapeDtypeStruct((D_FF_PAD, k), BF16)],
        scratch_shapes=[wb, wb, wb, wb],
        compiler_params=_params(("arbitrary", "arbitrary"), VMEM_PIPELINED_MIB),
        name="ffn_up",
    )(a, w1, w3, w2)


def _matmul_res_kernel(a_ref, w_ref, r_ref, o_ref):
    @pl.when(pl.program_id(2) == 0)
    def _():
        o_ref[...] = r_ref[...] + _dot(a_ref[...], w_ref[...])

    @pl.when(pl.program_id(2) > 0)
    def _():
        o_ref[...] += _dot(a_ref[...], w_ref[...])


def _matmul_res(a, w, res, tk, tm=1024, tn=1024):
    m, k = a.shape
    n = w.shape[1]
    return pl.pallas_call(
        _matmul_res_kernel,
        grid=(m // tm, n // tn, k // tk),
        in_specs=[pl.BlockSpec((tm, tk), lambda i, j, kk: (i, kk)),
                  pl.BlockSpec((tk, tn), lambda i, j, kk: (kk, j)),
                  pl.BlockSpec((tm, tn), lambda i, j, kk: (i, j))],
        out_specs=pl.BlockSpec((tm, tn), lambda i, j, kk: (i, j)),
        out_shape=jax.ShapeDtypeStruct((m, n), F32),
        compiler_params=_params(("parallel", "parallel", "arbitrary"), VMEM_MATMUL_MIB),
        name="matmul_res",
    )(a, w, res)


def _ab_mixer_kernel(sinks_ref, qug_ref, kvp_ref, kvc_ref, qg_ref, kg_ref,
                     lng_ref, lnb_ref, ws_ref, bs_ref, o_ref):
    n = pl.program_id(1)
    w = WINDOW
    hd = A_HEAD_DIM

    key = lax.broadcasted_iota(jnp.int32, (2 * w, w), 0)
    qry = lax.broadcasted_iota(jnp.int32, (2 * w, w), 1)
    dist = qry + w - key
    visible = (dist >= 0) & (dist < w) & ((key >= w) | (n > 0))
    low = lax.broadcasted_iota(jnp.int32, (1, LANES), 1) < hd
    sub8 = lax.broadcasted_iota(jnp.int32, (SUBLANES, LANES), 0)
    ones_low = jnp.where(low, 1.0, 0.0)
    ones_high = 1.0 - ones_low
    sel2 = jnp.where(sub8 < SUBLANES // 2, ones_low, ones_high).astype(BF16)
    sink_row = sub8 == 0

    def split_bf16(x):
        hi = x.astype(BF16)
        return hi, (x - hi.astype(F32)).astype(BF16)

    kk = jnp.concatenate([kvp_ref[:, :A_KV_WIDTH], kvc_ref[:, :A_KV_WIDTH]], axis=0)
    vv = jnp.concatenate([kvp_ref[:, A_KV_WIDTH:], kvc_ref[:, A_KV_WIDTH:]], axis=0)
    seg_r = lax.broadcasted_iota(jnp.int32, (A_KV_WIDTH, A_KV_WIDTH), 0) // hd
    seg_c = lax.broadcasted_iota(jnp.int32, (A_KV_WIDTH, A_KV_WIDTH), 1) // hd
    seg = (seg_r == seg_c).astype(BF16)
    hi, lo = split_bf16(kk * kk)
    kss = _dot(hi, seg) + _dot(lo, seg)
    kgain = jnp.concatenate([kg_ref[...]] * A_KV_HEADS, axis=1)
    khat = kk * lax.rsqrt(kss * (1.0 / hd) + EPS) * kgain
    qgain = jnp.concatenate([qg_ref[...]] * 2, axis=1) * (hd ** -0.5 * LOG2E)

    def head_pair_operands(x, h):
        tile = x[:, (h // 2) * LANES:(h // 2 + 1) * LANES]
        swapped = pltpu.roll(tile, hd, axis=1)
        in_low, in_high = (tile, swapped) if h % 2 == 0 else (swapped, tile)
        return jnp.where(low, in_low, 0.0), jnp.where(low, 0.0, in_high)

    for h in range(A_KV_HEADS):
        k_low, k_high = head_pair_operands(khat, h)
        v_low, v_high = head_pair_operands(vv, h)
        k_pad = (k_low.astype(BF16), k_high.astype(BF16))
        v_aug = tuple(
            jnp.concatenate([
                jnp.concatenate([v_p, jnp.broadcast_to(ones_p, v_p.shape)], axis=1),
                jnp.concatenate([jnp.zeros((SUBLANES, LANES), F32), jnp.broadcast_to(ones_p, (SUBLANES, LANES))], axis=1),
            ], axis=0).astype(BF16)
            for v_p, ones_p in ((v_low, ones_low), (v_high, ones_high)))
        for c in range(h * A_GROUP // 2, (h + 1) * A_GROUP // 2):
            qc = qug_ref[:, c * LANES:(c + 1) * LANES]
            hi, lo = split_bf16(qc * qc)
            qss = (lax.dot_general(sel2, hi, NT_DIMS, preferred_element_type=F32)
                   + lax.dot_general(sel2, lo, NT_DIMS, preferred_element_type=F32))
            qscale = lax.rsqrt(qss * (1.0 / hd) + EPS)
            qg = (qc * qgain).astype(BF16)
            acc = jnp.zeros((w, 2 * LANES), F32)
            for parity in range(2):
                st = lax.dot_general(k_pad[parity], qg, NT_DIMS, preferred_element_type=F32)
                r = qscale[parity * (SUBLANES // 2):parity * (SUBLANES // 2) + 1]
                st = jnp.where(visible, st * r, -jnp.inf)
                s_sink = sinks_ref[2 * c + parity] * LOG2E
                mx = jnp.maximum(jnp.max(st, axis=0, keepdims=True), s_sink)
                e = jnp.exp2(st - mx)
                e_sink = jnp.where(sink_row, jnp.exp2(s_sink - mx), 0.0)
                e_aug = jnp.concatenate([e, e_sink], axis=0).astype(BF16)
                acc = acc + lax.dot_general(e_aug, v_aug[parity], TN_DIMS, preferred_element_type=F32)
            o_ref[:, c * LANES:(c + 1) * LANES] = (acc[:, :LANES] / acc[:, LANES:]).astype(o_ref.dtype)

    row = lax.broadcasted_iota(jnp.int32, (w, w), 0)
    col = lax.broadcasted_iota(jnp.int32, (w, w), 1)
    causal = row >= col
    u0 = A_WIDTH
    v0 = A_WIDTH + B_WIDTH
    for g in range(B_GROUPS):
        sl = slice(g * B_GROUP_DIM, (g + 1) * B_GROUP_DIM)
        x = qug_ref[:, v0 + g * B_GROUP_DIM:v0 + (g + 1) * B_GROUP_DIM]
        mu = jnp.mean(x, axis=-1, keepdims=True)
        d = x - mu
        var = jnp.mean(d * d, axis=-1, keepdims=True)
        vn = d * lax.rsqrt(var + EPS) * lng_ref[:, sl] + lnb_ref[:, sl]
        wg = jnp.where(causal, ws_ref[g], 0.0).astype(BF16)
        sg = _dot(wg, vn.astype(BF16)) + bs_ref[:, g:g + 1]
        o_ref[:, u0 + g * B_GROUP_DIM:u0 + (g + 1) * B_GROUP_DIM] = (
            qug_ref[:, u0 + g * B_GROUP_DIM:u0 + (g + 1) * B_GROUP_DIM] * sg).astype(o_ref.dtype)


def _ab_mixer(proj, batch, seq, q_gain, k_gain, sinks, ln_g, ln_b, w_s, b_s):
    nb = seq // WINDOW
    kv_blk = (A_WIDTH + 2 * B_WIDTH) // (2 * A_KV_WIDTH)
    row_blk = lambda b, n: b * nb + n
    return pl.pallas_call(
        _ab_mixer_kernel,
        grid=(batch, nb),
        in_specs=[
            pl.BlockSpec(memory_space=pltpu.SMEM),
            pl.BlockSpec((WINDOW, A_WIDTH + 2 * B_WIDTH), lambda b, n: (row_blk(b, n), 0)),
            pl.BlockSpec((WINDOW, 2 * A_KV_WIDTH), lambda b, n: (b * nb + jnp.maximum(n - 1, 0), kv_blk)),
            pl.BlockSpec((WINDOW, 2 * A_KV_WIDTH), lambda b, n: (row_blk(b, n), kv_blk)),
            pl.BlockSpec((1, A_HEAD_DIM), lambda b, n: (0, 0)),
            pl.BlockSpec((1, A_HEAD_DIM), lambda b, n: (0, 0)),
            pl.BlockSpec((1, B_WIDTH), lambda b, n: (0, 0)),
            pl.BlockSpec((1, B_WIDTH), lambda b, n: (0, 0)),
            pl.BlockSpec((B_GROUPS, WINDOW, WINDOW), lambda b, n: (0, 0, 0)),
            pl.BlockSpec((WINDOW, B_GROUPS), lambda b, n: (0, 0)),
        ],
        out_specs=pl.BlockSpec((WINDOW, A_WIDTH + B_WIDTH), lambda b, n: (row_blk(b, n), 0)),
        out_shape=jax.ShapeDtypeStruct((batch * seq, A_WIDTH + B_WIDTH), BF16),
        compiler_params=_params(("parallel", "arbitrary"), VMEM_SMALL_MIB),
        name="ab_mixer",
    )(sinks, proj, proj, proj,
      q_gain.reshape(1, A_HEAD_DIM), k_gain.reshape(1, A_HEAD_DIM),
      ln_g.reshape(1, B_WIDTH), ln_b.reshape(1, B_WIDTH), w_s, b_s.T)


def _hgrn2_kernel(og_ref, qvg_ref, lf_ref, k_ref, o_ref, st_ref):
    c = C_CHUNK
    hw = C_HEADS_PER_STEP * C_DIM

    @pl.when(pl.program_id(2) == 0)
    def _():
        st_ref[...] = jnp.zeros_like(st_ref)

    row = lax.broadcasted_iota(jnp.int32, (c, c), 0)
    col = lax.broadcasted_iota(jnp.int32, (c, c), 1)
    row1 = lax.broadcasted_iota(jnp.int32, (c, 1), 0)
    cumsum_mat = (row >= col).astype(BF16)
    diff_bits = row ^ col
    level = jnp.zeros((c, c), jnp.int32)
    for b in range(1, c.bit_length() - 1):
        level = level + (diff_bits >= (1 << b)).astype(jnp.int32)
    level = jnp.where(col < row, level, jnp.where(col == row, -1, -2))
    og = og_ref[...]

    for h in range(C_HEADS_PER_STEP):
        sl = slice(h * C_DIM, (h + 1) * C_DIM)
        log2_f = lf_ref[:, sl]
        k = k_ref[:, sl].astype(F32)

        p1 = log2_f.astype(BF16)
        p2 = (log2_f - p1.astype(F32)).astype(BF16)
        cg = _dot(cumsum_mat, p1) + _dot(cumsum_mat, p2)

        q = qvg_ref[:, sl].astype(F32)
        v = qvg_ref[:, hw + h * C_DIM:hw + (h + 1) * C_DIM]
        st = st_ref[h]
        o = lax.dot_general((q * jnp.exp2(cg)).astype(BF16), st.astype(BF16), NT_DIMS,
                            preferred_element_type=F32)

        att = jnp.where(level == -1, jnp.sum(q * k, axis=-1, keepdims=True), 0.0)
        part = lax.dot_general((q * jnp.exp2(log2_f)).astype(BF16), k_ref[:, sl], NT_DIMS,
                               preferred_element_type=F32)
        att = jnp.where(level == 0, part, att)
        cg3 = cg.reshape(c // SUBLANES, SUBLANES, C_DIM)
        sub_row = lambda r: jnp.broadcast_to(cg3[:, r:r + 1, :], cg3.shape).reshape(c, C_DIM)
        n = 2
        while n < c:
            if 4 * n == SUBLANES:
                ref = jnp.where((row1 & (2 * n)) != 0, sub_row(3 * n - 1), sub_row(n - 1))
            elif 2 * n == SUBLANES:
                ref = sub_row(n - 1)
            else:
                ref = jnp.concatenate([jnp.broadcast_to(cg[lo + n - 1:lo + n, :], (2 * n, C_DIM))
                                       for lo in range(0, c, 2 * n)], axis=0)
            e = jnp.exp2(-jnp.abs(cg - ref))
            part = lax.dot_general((q * e).astype(BF16), (k * e).astype(BF16), NT_DIMS,
                                   preferred_element_type=F32)
            att = jnp.where(level == n.bit_length() - 1, part, att)
            n *= 2

        o = o + _dot(att.astype(BF16), v)

        last = cg[c - 1:c, :]
        kd = (k * jnp.exp2(last - cg)).astype(BF16)
        st_ref[h] = st * jnp.exp2(last) + lax.dot_general(v, kd, TN_DIMS, preferred_element_type=F32)

        y = o * lax.rsqrt(jnp.mean(o * o, axis=-1, keepdims=True) + EPS) * og
        gate = qvg_ref[:, 2 * hw + h * C_DIM:2 * hw + (h + 1) * C_DIM]
        o_ref[:, sl] = (y * gate.astype(F32)).astype(o_ref.dtype)


def _hgrn2(qvg, log2_f, k, o_gain, batch, seq):
    nc = seq // C_CHUNK
    hw = C_HEADS_PER_STEP * C_DIM
    nh = C_WIDTH // hw
    blk = lambda width: pl.BlockSpec((C_CHUNK, width), lambda b, g, c: (b * nc + c, g))
    return pl.pallas_call(
        _hgrn2_kernel,
        grid=(batch, nh, nc),
        in_specs=[pl.BlockSpec((1, C_DIM), lambda b, g, c: (0, 0)),
                  blk(3 * hw), blk(hw), blk(hw)],
        out_specs=blk(hw),
        out_shape=jax.ShapeDtypeStruct((batch * seq, C_WIDTH), BF16),
        scratch_shapes=[pltpu.VMEM((C_HEADS_PER_STEP, C_DIM, C_DIM), F32)],
        compiler_params=_params(("parallel", "parallel", "arbitrary"), VMEM_SMALL_MIB),
        name="hgrn2",
    )(o_gain.reshape(1, C_DIM), qvg, log2_f, k)


def _ffn(x, norm_g, w1, w3, w2, layer):
    h, w2b = _ffn_up(_rmsnorm(x, norm_g[layer]), w1, w3, w2, layer)
    return _matmul_res(h, w2b, x, tk=D_FF_PAD // 4)


def kernel(x, ffn1_norm, ffn1_w1, ffn1_w3, ffn1_w2, mix_norm, ffn2_norm, ffn2_w1, ffn2_w3, ffn2_w2, ab_w_in, ab_q_norm, ab_k_norm, ab_sinks, ab_v_ln_g, ab_v_ln_b, ab_w_s, ab_b_s, ab_w_out, c_w_in, c_lb_logits, c_o_norm, c_w_out):
    batch, seq, d = x.shape
    depth = ffn1_norm.shape[0]
    tn = 512
    tm2 = 1024
    x = x.reshape(batch * seq, d)
    for l in range(depth):
        x = _ffn(x, ffn1_norm, ffn1_w1, ffn1_w3, ffn1_w2, l)
        hn = _rmsnorm(x, mix_norm[l])
        j = l // 2
        if l % 2 == 0:
            kv_tile = A_WIDTH // tn
            n_tiles = AB_IN // tn
            o_tile = lambda t: jnp.where(t < kv_tile, t, jnp.where(t == kv_tile, n_tiles - 1, t - 1))
            proj = _wpipe_matmul("ab_in_proj", hn, ab_w_in, j, n_tiles, [F32], o_tile=o_tile,
                                 epilogue=functools.partial(_act_epilogue, act=_gelu, act_from=kv_tile + 1))
            mixed = _ab_mixer(proj, batch, seq, ab_q_norm[j], ab_k_norm[j], ab_sinks[j],
                              ab_v_ln_g[j], ab_v_ln_b[j], ab_w_s[j], ab_b_s[j])
            x = _wpipe_matmul("ab_out_proj", mixed, ab_w_out, j, d // tn, [F32], _residual_epilogue,
                              tiled_extra=[x], tm=tm2)
        else:
            ct = C_WIDTH // tn
            gt = C_HEADS_PER_STEP * C_DIM // tn
            qvg = _wpipe_matmul("c_in_proj_qvg", hn, c_w_in, j, 3 * ct, [BF16],
                                functools.partial(_act_epilogue, act=_silu, act_from=2 * ct),
                                w_tile=lambda t: jnp.where(t < ct, t, t + ct),
                                o_tile=lambda t: (t % ct) // gt * (3 * gt) + t // ct * gt + t % gt)
            log2_f, k = _wpipe_matmul("c_in_proj_f", hn, c_w_in, j, ct, [F32, BF16],
                                      functools.partial(_forget_gate_epilogue, layer=l),
                                      w_tile=lambda t: t + ct, col_extra=[c_lb_logits])
            mixed = _hgrn2(qvg, log2_f, k, c_o_norm[j], batch, seq)
            x = _wpipe_matmul("c_out_proj", mixed, c_w_out, j, d // tn, [F32], _residual_epilogue,
                              tiled_extra=[x], tm=tm2)
        x = _ffn(x, ffn2_norm, ffn2_w1, ffn2_w3, ffn2_w2, l)
    return x.reshape(batch, seq, d)
```

```python
import functools

import jax
import jax.numpy as jnp
from jax import lax
from jax.experimental import pallas as pl
from jax.experimental.pallas import tpu as pltpu

F32 = jnp.float32
BF16 = jnp.bfloat16

D_FF = 11008
FF_TILE = 256
D_FF_PAD = 11264
EPS = 1e-6

A_HEADS = 32
A_KV_HEADS = 4
A_GROUP = A_HEADS // A_KV_HEADS
A_HEAD_DIM = 64
WINDOW = 128
A_WIDTH = A_HEADS * A_HEAD_DIM
A_KV_WIDTH = A_KV_HEADS * A_HEAD_DIM
B_GROUPS = 16
B_GROUP_DIM = 128
B_WIDTH = B_GROUPS * B_GROUP_DIM
AB_IN = A_WIDTH + 2 * A_KV_WIDTH + 2 * B_WIDTH

C_HEADS = 32
C_DIM = 128
C_WIDTH = C_HEADS * C_DIM
C_CHUNK = 128
C_HEADS_PER_STEP = 32

LANES = 128
SUBLANES = 8
MIB = 1 << 20
DOT_ROWS = 1024
VMEM_SMALL_MIB = 32
VMEM_MATMUL_MIB = 56
VMEM_PIPELINED_MIB = 60
VMEM_RESIDUAL_MIB = 62
LOG2E = 1.4426950408889634

NT_DIMS = (((1,), (1,)), ((), ()))
TN_DIMS = (((0,), (0,)), ((), ()))


def _params(semantics, vmem_mib):
    return pltpu.CompilerParams(dimension_semantics=semantics, vmem_limit_bytes=vmem_mib * MIB)


def _dot(a, b):
    return jnp.dot(a, b, preferred_element_type=F32)


def _dot_rows(a_ref, rows, w_ref):
    return _dot(a_ref[rows, :], w_ref[...])


def _gelu(y):
    return 0.5 * y * (1.0 + lax.erf(y * (2.0 ** -0.5)))


def _silu(y):
    half = 0.5 * y
    return half + half * jnp.tanh(half)


def _rmsnorm_kernel(x_ref, g_ref, o_ref):
    x = x_ref[...]
    r = lax.rsqrt(jnp.mean(x * x, axis=-1, keepdims=True) + EPS)
    o_ref[...] = (x * r * g_ref[...]).astype(o_ref.dtype)


def _rmsnorm(x, g, rows=512):
    m, d = x.shape
    return pl.pallas_call(
        _rmsnorm_kernel,
        grid=(m // rows,),
        in_specs=[pl.BlockSpec((rows, d), lambda i: (i, 0)),
                  pl.BlockSpec((1, d), lambda i: (0, 0))],
        out_specs=pl.BlockSpec((rows, d), lambda i: (i, 0)),
        out_shape=jax.ShapeDtypeStruct((m, d), BF16),
        compiler_params=_params(("parallel",), VMEM_SMALL_MIB),
        name="rmsnorm",
    )(x, g.reshape(1, d))


def _store_epilogue(j, chunks):
    for dot, _, out_refs in chunks:
        out_refs[0][...] = dot().astype(out_refs[0].dtype)


def _residual_epilogue(j, chunks):
    for dot, extra_refs, out_refs in chunks:
        out_refs[0][...] = extra_refs[0][...] + dot()


def _act_epilogue(j, chunks, *, act, act_from):
    @pl.when(j >= act_from)
    def _():
        for dot, _, out_refs in chunks:
            out_refs[0][...] = act(dot()).astype(out_refs[0].dtype)

    @pl.when(j < act_from)
    def _():
        _store_epilogue(j, chunks)


def _forget_gate_epilogue(j, chunks, *, layer):
    logits = chunks[0][1][0][...]
    lmax = jnp.max(logits, axis=0, keepdims=True)
    ex = jnp.exp(logits - lmax)
    prob = ex / jnp.sum(ex, axis=0, keepdims=True)
    lb = jnp.sum(prob[:layer + 1], axis=0, keepdims=True) - prob[0:1]
    c1 = 0.5 * (1.0 - lb)
    c0 = lb + c1
    for dot, _, out_refs in chunks:
        scaled = c1 * jnp.tanh(0.5 * dot())
        out_refs[0][...] = jnp.log2(c0 + scaled)
        out_refs[1][...] = (c1 - scaled).astype(out_refs[1].dtype)


def _wpipe_matmul_kernel(*refs, n_tiled, n_col, epilogue):
    a_ref, w_ref = refs[:2]
    tiled_refs = refs[2:2 + n_tiled]
    col_refs = refs[2 + n_tiled:2 + n_tiled + n_col]
    out_refs = refs[2 + n_tiled + n_col:-2]
    p = pl.program_id(0)
    slab = w_ref.shape[0]
    r0 = pl.multiple_of(pl.program_id(1) * slab, slab)

    def step(w_next_ref, w_cur_ref, compute):
        w_next_ref[pl.ds(r0, slab), :] = w_ref[...].astype(BF16)
        if compute:
            chunks = []
            for r in range(0, a_ref.shape[0], DOT_ROWS):
                rows = pl.ds(r, DOT_ROWS)
                chunks.append((functools.partial(_dot_rows, a_ref, rows, w_cur_ref),
                               [t.at[rows, :] for t in tiled_refs] + list(col_refs),
                               [o.at[rows, :] for o in out_refs]))
            epilogue(p - 1, chunks)

    even = (p % 2) == 0
    pl.when((p > 0) & even)(functools.partial(step, refs[-2], refs[-1], True))
    pl.when((p > 0) & jnp.logical_not(even))(functools.partial(step, refs[-1], refs[-2], True))
    pl.when(p == 0)(functools.partial(step, refs[-2], refs[-1], False))


def _wpipe_matmul(name, a, w, layer, n_tiles, out_dtypes, epilogue=_store_epilogue, w_tile=lambda j: j,
                  o_tile=lambda j: j, tiled_extra=(), col_extra=(), tm=2048, tn=512, vmem_mib=VMEM_PIPELINED_MIB):
    m, k = a.shape
    n_i = m // tm
    next_tile = lambda p: w_tile(jnp.minimum(p, n_tiles - 1))
    cur_tile = lambda p: o_tile(jnp.maximum(p - 1, 0))
    row_block = lambda p, i: jnp.where(p == 0, 0, i)
    out_block = pl.BlockSpec((tm, tn), lambda p, i: (row_block(p, i), cur_tile(p)))
    in_specs = [pl.BlockSpec((tm, k), lambda p, i: (row_block(p, i), 0)),
                pl.BlockSpec((None, k // n_i, tn), lambda p, i: (layer, i, next_tile(p)))]
    in_specs += [out_block for _ in tiled_extra]
    in_specs += [pl.BlockSpec((e.shape[0], tn), lambda p, i: (0, cur_tile(p))) for e in col_extra]
    wb = pltpu.VMEM((k, tn), BF16)
    outs = pl.pallas_call(
        functools.partial(_wpipe_matmul_kernel, n_tiled=len(tiled_extra), n_col=len(col_extra), epilogue=epilogue),
        grid=(n_tiles + 1, n_i),
        in_specs=in_specs,
        out_specs=[out_block for _ in out_dtypes],
        out_shape=[jax.ShapeDtypeStruct((m, n_tiles * tn), dt) for dt in out_dtypes],
        scratch_shapes=[wb, wb],
        compiler_params=_params(("arbitrary", "arbitrary"), vmem_mib),
        name=name,
    )(a, w, *tiled_extra, *col_extra)
    return outs if len(outs) > 1 else outs[0]


def _ffn_up_kernel(a_ref, w1_ref, w3_ref, w2_ref, h_ref, w2b_ref, w1e_ref, w3e_ref, w1o_ref, w3o_ref, *, n_tiles):
    p = pl.program_id(0)
    i = pl.program_id(1)
    t = p - 1
    slab = w1_ref.shape[0]
    r0 = pl.multiple_of(i * slab, slab)
    slab2 = w2_ref.shape[0]
    q0 = pl.multiple_of(i * slab2, slab2)

    def compute(w1_next_ref, w3_next_ref, w1_cur_ref, w3_cur_ref):
        w1_next_ref[pl.ds(r0, slab), :] = w1_ref[...].astype(BF16)
        w3_next_ref[pl.ds(r0, slab), :] = w3_ref[...].astype(BF16)
        for r in range(0, a_ref.shape[0], DOT_ROWS):
            rows = pl.ds(r, DOT_ROWS)
            a = a_ref[rows, :]
            y1 = _dot(a, w1_cur_ref[...])
            y3 = _dot(a, w3_cur_ref[...])
            h_ref[rows, :] = (_silu(y1) * (0.5 * y3)).astype(h_ref.dtype)
        w2b_ref[pl.ds(q0, slab2), :] = w2_ref[...].astype(BF16)

    real = (t >= 0) & (t < n_tiles)
    even = (p % 2) == 0
    pl.when(real & even)(functools.partial(compute, w1e_ref, w3e_ref, w1o_ref, w3o_ref))
    pl.when(real & jnp.logical_not(even))(functools.partial(compute, w1o_ref, w3o_ref, w1e_ref, w3e_ref))

    @pl.when(p == 0)
    def _():
        w1e_ref[pl.ds(r0, slab), :] = w1_ref[...].astype(BF16)
        w3e_ref[pl.ds(r0, slab), :] = w3_ref[...].astype(BF16)

    @pl.when(t == n_tiles)
    def _():
        h_ref[...] = jnp.zeros_like(h_ref)
        w2b_ref[pl.ds(q0, slab2), :] = jnp.zeros((slab2, w2b_ref.shape[1]), BF16)


def _ffn_up(a, w1, w3, w2, layer, tm=2048):
    m, k = a.shape
    tf = FF_TILE
    n_tiles = D_FF // tf
    n_i = m // tm
    n_pos = D_FF_PAD // tf + 1
    next_tile = lambda p: jnp.minimum(p, n_tiles - 1)
    cur_tile = lambda p: jnp.clip(p - 1, 0, n_tiles - 1)
    out_tile = lambda p: jnp.maximum(p - 1, 0)
    a_block = lambda p, i: jnp.where(p == 0, 0, jnp.where(p == n_pos - 1, n_i - 1, i))
    h_block = lambda p, i: jnp.where(p == 0, 0, i)
    wb = pltpu.VMEM((k, tf), BF16)
    return pl.pallas_call(
        functools.partial(_ffn_up_kernel, n_tiles=n_tiles),
        grid=(n_pos, n_i),
        in_specs=[pl.BlockSpec((tm, k), lambda p, i: (a_block(p, i), 0)),
                  pl.BlockSpec((None, k // n_i, tf), lambda p, i: (layer, i, next_tile(p))),
                  pl.BlockSpec((None, k // n_i, tf), lambda p, i: (layer, i, next_tile(p))),
                  pl.BlockSpec((None, tf // n_i, k), lambda p, i: (layer, cur_tile(p) * n_i + i, 0))],
        out_specs=[pl.BlockSpec((tm, tf), lambda p, i: (h_block(p, i), out_tile(p))),
                   pl.BlockSpec((tf, k), lambda p, i: (out_tile(p), 0))],
        out_shape=[jax.ShapeDtypeStruct((m, D_FF_PAD), BF16),
                   jax.ShapeDtypeStruct((D_FF_PAD, k), BF16)],
        scratch_shapes=[wb, wb, wb, wb],
        compiler_params=_params(("arbitrary", "arbitrary"), VMEM_PIPELINED_MIB),
        name="ffn_up",
    )(a, w1, w3, w2)


def _matmul_res_kernel(a_ref, w_ref, r_ref, o_ref):
    @pl.when(pl.program_id(2) == 0)
    def _():
        o_ref[...] = r_ref[...] + _dot(a_ref[...], w_ref[...])

    @pl.when(pl.program_id(2) > 0)
    def _():
        o_ref[...] += _dot(a_ref[...], w_ref[...])


def _matmul_res(a, w, res, tk, tm=1024, tn=1024):
    m, k = a.shape
    n = w.shape[1]
    return pl.pallas_call(
        _matmul_res_kernel,
        grid=(m // tm, n // tn, k // tk),
        in_specs=[pl.BlockSpec((tm, tk), lambda i, j, kk: (i, kk)),
                  pl.BlockSpec((tk, tn), lambda i, j, kk: (kk, j)),
                  pl.BlockSpec((tm, tn), lambda i, j, kk: (i, j))],
        out_specs=pl.BlockSpec((tm, tn), lambda i, j, kk: (i, j)),
        out_shape=jax.ShapeDtypeStruct((m, n), F32),
        compiler_params=_params(("parallel", "parallel", "arbitrary"), VMEM_MATMUL_MIB),
        name="matmul_res",
    )(a, w, res)


def _ab_mixer_kernel(sinks_ref, qug_ref, kvp_ref, kvc_ref, qg_ref, kg_ref,
                     lng_ref, lnb_ref, ws_ref, bs_ref, o_ref):
    n = pl.program_id(1)
    w = WINDOW
    hd = A_HEAD_DIM

    key = lax.broadcasted_iota(jnp.int32, (2 * w, w), 0)
    qry = lax.broadcasted_iota(jnp.int32, (2 * w, w), 1)
    dist = qry + w - key
    visible = (dist >= 0) & (dist < w) & ((key >= w) | (n > 0))
    low = lax.broadcasted_iota(jnp.int32, (1, LANES), 1) < hd
    sub8 = lax.broadcasted_iota(jnp.int32, (SUBLANES, LANES), 0)
    ones_low = jnp.where(low, 1.0, 0.0)
    ones_high = 1.0 - ones_low
    sel2 = jnp.where(sub8 < SUBLANES // 2, ones_low, ones_high).astype(BF16)
    sink_row = sub8 == 0

    def split_bf16(x):
        hi = x.astype(BF16)
        return hi, (x - hi.astype(F32)).astype(BF16)

    kk = jnp.concatenate([kvp_ref[:, :A_KV_WIDTH], kvc_ref[:, :A_KV_WIDTH]], axis=0)
    vv = jnp.concatenate([kvp_ref[:, A_KV_WIDTH:], kvc_ref[:, A_KV_WIDTH:]], axis=0)
    seg_r = lax.broadcasted_iota(jnp.int32, (A_KV_WIDTH, A_KV_WIDTH), 0) // hd
    seg_c = lax.broadcasted_iota(jnp.int32, (A_KV_WIDTH, A_KV_WIDTH), 1) // hd
    seg = (seg_r == seg_c).astype(BF16)
    hi, lo = split_bf16(kk * kk)
    kss = _dot(hi, seg) + _dot(lo, seg)
    kgain = jnp.concatenate([kg_ref[...]] * A_KV_HEADS, axis=1)
    khat = kk * lax.rsqrt(kss * (1.0 / hd) + EPS) * kgain
    qgain = jnp.concatenate([qg_ref[...]] * 2, axis=1) * (hd ** -0.5 * LOG2E)

    def head_pair_operands(x, h):
        tile = x[:, (h // 2) * LANES:(h // 2 + 1) * LANES]
        swapped = pltpu.roll(tile, hd, axis=1)
        in_low, in_high = (tile, swapped) if h % 2 == 0 else (swapped, tile)
        return jnp.where(low, in_low, 0.0), jnp.where(low, 0.0, in_high)

    for h in range(A_KV_HEADS):
        k_low, k_high = head_pair_operands(khat, h)
        v_low, v_high = head_pair_operands(vv, h)
        k_pad = (k_low.astype(BF16), k_high.astype(BF16))
        v_aug = tuple(
            jnp.concatenate([
                jnp.concatenate([v_p, jnp.broadcast_to(ones_p, v_p.shape)], axis=1),
                jnp.concatenate([jnp.zeros((SUBLANES, LANES), F32), jnp.broadcast_to(ones_p, (SUBLANES, LANES))], axis=1),
            ], axis=0).astype(BF16)
            for v_p, ones_p in ((v_low, ones_low), (v_high, ones_high)))
        for c in range(h * A_GROUP // 2, (h + 1) * A_GROUP // 2):
            qc = qug_ref[:, c * LANES:(c + 1) * LANES]
            hi, lo = split_bf16(qc * qc)
            qss = (lax.dot_general(sel2, hi, NT_DIMS, preferred_element_type=F32)
                   + lax.dot_general(sel2, lo, NT_DIMS, preferred_element_type=F32))
            qscale = lax.rsqrt(qss * (1.0 / hd) + EPS)
            qg = (qc * qgain).astype(BF16)
            acc = jnp.zeros((w, 2 * LANES), F32)
            for parity in range(2):
                st = lax.dot_general(k_pad[parity], qg, NT_DIMS, preferred_element_type=F32)
                r = qscale[parity * (SUBLANES // 2):parity * (SUBLANES // 2) + 1]
                st = jnp.where(visible, st * r, -jnp.inf)
                s_sink = sinks_ref[2 * c + parity] * LOG2E
                mx = jnp.maximum(jnp.max(st, axis=0, keepdims=True), s_sink)
                e = jnp.exp2(st - mx)
                e_sink = jnp.where(sink_row, jnp.exp2(s_sink - mx), 0.0)
                e_aug = jnp.concatenate([e, e_sink], axis=0).astype(BF16)
                acc = acc + lax.dot_general(e_aug, v_aug[parity], TN_DIMS, preferred_element_type=F32)
            o_ref[:, c * LANES:(c + 1) * LANES] = (acc[:, :LANES] / acc[:, LANES:]).astype(o_ref.dtype)

    row = lax.broadcasted_iota(jnp.int32, (w, w), 0)
    col = lax.broadcasted_iota(jnp.int32, (w, w), 1)
    causal = row >= col
    u0 = A_WIDTH
    v0 = A_WIDTH + B_WIDTH
    for g in range(B_GROUPS):
        sl = slice(g * B_GROUP_DIM, (g + 1) * B_GROUP_DIM)
        x = qug_ref[:, v0 + g * B_GROUP_DIM:v0 + (g + 1) * B_GROUP_DIM]
        mu = jnp.mean(x, axis=-1, keepdims=True)
        d = x - mu
        var = jnp.mean(d * d, axis=-1, keepdims=True)
        vn = d * lax.rsqrt(var + EPS) * lng_ref[:, sl] + lnb_ref[:, sl]
        wg = jnp.where(causal, ws_ref[g], 0.0).astype(BF16)
        sg = _dot(wg, vn.astype(BF16)) + bs_ref[:, g:g + 1]
        o_ref[:, u0 + g * B_GROUP_DIM:u0 + (g + 1) * B_GROUP_DIM] = (
            qug_ref[:, u0 + g * B_GROUP_DIM:u0 + (g + 1) * B_GROUP_DIM] * sg).astype(o_ref.dtype)


def _ab_mixer(proj, batch, seq, q_gain, k_gain, sinks, ln_g, ln_b, w_s, b_s):
    nb = seq // WINDOW
    kv_blk = (A_WIDTH + 2 * B_WIDTH) // (2 * A_KV_WIDTH)
    row_blk = lambda b, n: b * nb + n
    return pl.pallas_call(
        _ab_mixer_kernel,
        grid=(batch, nb),
        in_specs=[
            pl.BlockSpec(memory_space=pltpu.SMEM),
            pl.BlockSpec((WINDOW, A_WIDTH + 2 * B_WIDTH), lambda b, n: (row_blk(b, n), 0)),
            pl.BlockSpec((WINDOW, 2 * A_KV_WIDTH), lambda b, n: (b * nb + jnp.maximum(n - 1, 0), kv_blk)),
            pl.BlockSpec((WINDOW, 2 * A_KV_WIDTH), lambda b, n: (row_blk(b, n), kv_blk)),
            pl.BlockSpec((1, A_HEAD_DIM), lambda b, n: (0, 0)),
            pl.BlockSpec((1, A_HEAD_DIM), lambda b, n: (0, 0)),
            pl.BlockSpec((1, B_WIDTH), lambda b, n: (0, 0)),
            pl.BlockSpec((1, B_WIDTH), lambda b, n: (0, 0)),
            pl.BlockSpec((B_GROUPS, WINDOW, WINDOW), lambda b, n: (0, 0, 0)),
            pl.BlockSpec((WINDOW, B_GROUPS), lambda b, n: (0, 0)),
        ],
        out_specs=pl.BlockSpec((WINDOW, A_WIDTH + B_WIDTH), lambda b, n: (row_blk(b, n), 0)),
        out_shape=jax.ShapeDtypeStruct((batch * seq, A_WIDTH + B_WIDTH), BF16),
        compiler_params=_params(("parallel", "arbitrary"), VMEM_SMALL_MIB),
        name="ab_mixer",
    )(sinks, proj, proj, proj,
      q_gain.reshape(1, A_HEAD_DIM), k_gain.reshape(1, A_HEAD_DIM),
      ln_g.reshape(1, B_WIDTH), ln_b.reshape(1, B_WIDTH), w_s, b_s.T)


def _hgrn2_kernel(og_ref, qvg_ref, lf_ref, k_ref, o_ref, st_ref):
    c = C_CHUNK
    hw = C_HEADS_PER_STEP * C_DIM

    @pl.when(pl.program_id(2) == 0)
    def _():
        st_ref[...] = jnp.zeros_like(st_ref)

    row = lax.broadcasted_iota(jnp.int32, (c, c), 0)
    col = lax.broadcasted_iota(jnp.int32, (c, c), 1)
    row1 = lax.broadcasted_iota(jnp.int32, (c, 1), 0)
    cumsum_mat = (row >= col).astype(BF16)
    diff_bits = row ^ col
    level = jnp.zeros((c, c), jnp.int32)
    for b in range(1, c.bit_length() - 1):
        level = level + (diff_bits >= (1 << b)).astype(jnp.int32)
    level = jnp.where(col < row, level, jnp.where(col == row, -1, -2))
    og = og_ref[...]

    for h in range(C_HEADS_PER_STEP):
        sl = slice(h * C_DIM, (h + 1) * C_DIM)
        log2_f = lf_ref[:, sl]
        k = k_ref[:, sl].astype(F32)

        p1 = log2_f.astype(BF16)
        p2 = (log2_f - p1.astype(F32)).astype(BF16)
        cg = _dot(cumsum_mat, p1) + _dot(cumsum_mat, p2)

        q = qvg_ref[:, sl].astype(F32)
        v = qvg_ref[:, hw + h * C_DIM:hw + (h + 1) * C_DIM]
        st = st_ref[h]
        o = lax.dot_general((q * jnp.exp2(cg)).astype(BF16), st.astype(BF16), NT_DIMS,
                            preferred_element_type=F32)

        att = jnp.where(level == -1, jnp.sum(q * k, axis=-1, keepdims=True), 0.0)
        part = lax.dot_general((q * jnp.exp2(log2_f)).astype(BF16), k_ref[:, sl], NT_DIMS,
                               preferred_element_type=F32)
        att = jnp.where(level == 0, part, att)
        cg3 = cg.reshape(c // SUBLANES, SUBLANES, C_DIM)
        sub_row = lambda r: jnp.broadcast_to(cg3[:, r:r + 1, :], cg3.shape).reshape(c, C_DIM)
        n = 2
        while n < c:
            if 4 * n == SUBLANES:
                ref = jnp.where((row1 & (2 * n)) != 0, sub_row(3 * n - 1), sub_row(n - 1))
            elif 2 * n == SUBLANES:
                ref = sub_row(n - 1)
            else:
                ref = jnp.concatenate([jnp.broadcast_to(cg[lo + n - 1:lo + n, :], (2 * n, C_DIM))
                                       for lo in range(0, c, 2 * n)], axis=0)
            e = jnp.exp2(-jnp.abs(cg - ref))
            part = lax.dot_general((q * e).astype(BF16), (k * e).astype(BF16), NT_DIMS,
                                   preferred_element_type=F32)
            att = jnp.where(level == n.bit_length() - 1, part, att)
            n *= 2

        o = o + _dot(att.astype(BF16), v)

        last = cg[c - 1:c, :]
        kd = (k * jnp.exp2(last - cg)).astype(BF16)
        st_ref[h] = st * jnp.exp2(last) + lax.dot_general(v, kd, TN_DIMS, preferred_element_type=F32)

        y = o * lax.rsqrt(jnp.mean(o * o, axis=-1, keepdims=True) + EPS) * og
        gate = qvg_ref[:, 2 * hw + h * C_DIM:2 * hw + (h + 1) * C_DIM]
        o_ref[:, sl] = (y * gate.astype(F32)).astype(o_ref.dtype)


def _hgrn2(qvg, log2_f, k, o_gain, batch, seq):
    nc = seq // C_CHUNK
    hw = C_HEADS_PER_STEP * C_DIM
    nh = C_WIDTH // hw
    blk = lambda width: pl.BlockSpec((C_CHUNK, width), lambda b, g, c: (b * nc + c, g))
    return pl.pallas_call(
        _hgrn2_kernel,
        grid=(batch, nh, nc),
        in_specs=[pl.BlockSpec((1, C_DIM), lambda b, g, c: (0, 0)),
                  blk(3 * hw), blk(hw), blk(hw)],
        out_specs=blk(hw),
        out_shape=jax.ShapeDtypeStruct((batch * seq, C_WIDTH), BF16),
        scratch_shapes=[pltpu.VMEM((C_HEADS_PER_STEP, C_DIM, C_DIM), F32)],
        compiler_params=_params(("parallel", "parallel", "arbitrary"), VMEM_SMALL_MIB),
        name="hgrn2",
    )(o_gain.reshape(1, C_DIM), qvg, log2_f, k)


def _ffn(x, norm_g, w1, w3, w2, layer):
    h, w2b = _ffn_up(_rmsnorm(x, norm_g[layer]), w1, w3, w2, layer)
    return _matmul_res(h, w2b, x, tk=D_FF_PAD // 4)


def kernel(x, ffn1_norm, ffn1_w1, ffn1_w3, ffn1_w2, mix_norm, ffn2_norm, ffn2_w1, ffn2_w3, ffn2_w2, ab_w_in, ab_q_norm, ab_k_norm, ab_sinks, ab_v_ln_g, ab_v_ln_b, ab_w_s, ab_b_s, ab_w_out, c_w_in, c_lb_logits, c_o_norm, c_w_out):
    batch, seq, d = x.shape
    depth = ffn1_norm.shape[0]
    tn = 512
    x = x.reshape(batch * seq, d)
    for l in range(depth):
        x = _ffn(x, ffn1_norm, ffn1_w1, ffn1_w3, ffn1_w2, l)
        hn = _rmsnorm(x, mix_norm[l])
        j = l // 2
        if l % 2 == 0:
            kv_tile = A_WIDTH // tn
            n_tiles = AB_IN // tn
            o_tile = lambda t: jnp.where(t < kv_tile, t, jnp.where(t == kv_tile, n_tiles - 1, t - 1))
            proj = _wpipe_matmul("ab_in_proj", hn, ab_w_in, j, n_tiles, [F32], o_tile=o_tile,
                                 epilogue=functools.partial(_act_epilogue, act=_gelu, act_from=kv_tile + 1))
            mixed = _ab_mixer(proj, batch, seq, ab_q_norm[j], ab_k_norm[j], ab_sinks[j],
                              ab_v_ln_g[j], ab_v_ln_b[j], ab_w_s[j], ab_b_s[j])
            x = _wpipe_matmul("ab_out_proj", mixed, ab_w_out, j, d // tn, [F32], _residual_epilogue,
                              tiled_extra=[x], vmem_mib=VMEM_RESIDUAL_MIB)
        else:
            ct = C_WIDTH // tn
            gt = C_HEADS_PER_STEP * C_DIM // tn
            qvg = _wpipe_matmul("c_in_proj_qvg", hn, c_w_in, j, 3 * ct, [BF16],
                                functools.partial(_act_epilogue, act=_silu, act_from=2 * ct),
                                w_tile=lambda t: jnp.where(t < ct, t, t + ct),
                                o_tile=lambda t: (t % ct) // gt * (3 * gt) + t // ct * gt + t % gt)
            log2_f, k = _wpipe_matmul("c_in_proj_f", hn, c_w_in, j, ct, [F32, BF16],
                                      functools.partial(_forget_gate_epilogue, layer=l),
                                      w_tile=lambda t: t + ct, col_extra=[c_lb_logits])
            mixed = _hgrn2(qvg, log2_f, k, c_o_norm[j], batch, seq)
            x = _wpipe_matmul("c_out_proj", mixed, c_w_out, j, d // tn, [F32], _residual_epilogue,
                              tiled_extra=[x], vmem_mib=VMEM_RESIDUAL_MIB)
        x = _ffn(x, ffn2_norm, ffn2_w1, ffn2_w3, ffn2_w2, l)
    return x.reshape(batch, seq, d)
```

```python
import functools

import jax
import jax.numpy as jnp
from jax import lax
from jax.experimental import pallas as pl
from jax.experimental.pallas import tpu as pltpu

F32 = jnp.float32
BF16 = jnp.bfloat16

D_FF = 11008
FF_TILE = 256
D_FF_PAD = 11264
EPS = 1e-6

A_HEADS = 32
A_KV_HEADS = 4
A_GROUP = A_HEADS // A_KV_HEADS
A_HEAD_DIM = 64
WINDOW = 128
A_WIDTH = A_HEADS * A_HEAD_DIM
A_KV_WIDTH = A_KV_HEADS * A_HEAD_DIM
B_GROUPS = 16
B_GROUP_DIM = 128
B_WIDTH = B_GROUPS * B_GROUP_DIM
AB_IN = A_WIDTH + 2 * A_KV_WIDTH + 2 * B_WIDTH

C_HEADS = 32
C_DIM = 128
C_WIDTH = C_HEADS * C_DIM
C_CHUNK = 256
C_SUB = 128
C_HEADS_PER_STEP = 32

LANES = 128
SUBLANES = 8
MIB = 1 << 20
DOT_ROWS = 1024
VMEM_SMALL_MIB = 32
VMEM_MATMUL_MIB = 56
VMEM_PIPELINED_MIB = 60
VMEM_RESIDUAL_MIB = 62
LOG2E = 1.4426950408889634

NT_DIMS = (((1,), (1,)), ((), ()))
TN_DIMS = (((0,), (0,)), ((), ()))


def _params(semantics, vmem_mib):
    return pltpu.CompilerParams(dimension_semantics=semantics, vmem_limit_bytes=vmem_mib * MIB)


def _dot(a, b):
    return jnp.dot(a, b, preferred_element_type=F32)


def _dot_rows(a_ref, rows, w_ref):
    return _dot(a_ref[rows, :], w_ref[...])


def _gelu(y):
    return 0.5 * y * (1.0 + lax.erf(y * (2.0 ** -0.5)))


def _silu(y):
    half = 0.5 * y
    return half + half * jnp.tanh(half)


def _rmsnorm_kernel(x_ref, g_ref, o_ref):
    x = x_ref[...]
    r = lax.rsqrt(jnp.mean(x * x, axis=-1, keepdims=True) + EPS)
    o_ref[...] = (x * r * g_ref[...]).astype(o_ref.dtype)


def _rmsnorm(x, g, rows=512):
    m, d = x.shape
    return pl.pallas_call(
        _rmsnorm_kernel,
        grid=(m // rows,),
        in_specs=[pl.BlockSpec((rows, d), lambda i: (i, 0)),
                  pl.BlockSpec((1, d), lambda i: (0, 0))],
        out_specs=pl.BlockSpec((rows, d), lambda i: (i, 0)),
        out_shape=jax.ShapeDtypeStruct((m, d), BF16),
        compiler_params=_params(("parallel",), VMEM_SMALL_MIB),
        name="rmsnorm",
    )(x, g.reshape(1, d))


def _store_epilogue(j, chunks):
    for dot, _, out_refs in chunks:
        out_refs[0][...] = dot().astype(out_refs[0].dtype)


def _residual_epilogue(j, chunks):
    for dot, extra_refs, out_refs in chunks:
        out_refs[0][...] = extra_refs[0][...] + dot()


def _act_epilogue(j, chunks, *, act, act_from):
    @pl.when(j >= act_from)
    def _():
        for dot, _, out_refs in chunks:
            out_refs[0][...] = act(dot()).astype(out_refs[0].dtype)

    @pl.when(j < act_from)
    def _():
        _store_epilogue(j, chunks)


def _forget_gate_epilogue(j, chunks, *, layer):
    logits = chunks[0][1][0][...]
    lmax = jnp.max(logits, axis=0, keepdims=True)
    ex = jnp.exp(logits - lmax)
    prob = ex / jnp.sum(ex, axis=0, keepdims=True)
    lb = jnp.sum(prob[:layer + 1], axis=0, keepdims=True) - prob[0:1]
    c1 = 0.5 * (1.0 - lb)
    c0 = lb + c1
    for dot, _, out_refs in chunks:
        scaled = c1 * jnp.tanh(0.5 * dot())
        out_refs[0][...] = jnp.log2(c0 + scaled)
        out_refs[1][...] = (c1 - scaled).astype(out_refs[1].dtype)


def _wpipe_matmul_kernel(*refs, n_tiled, n_col, epilogue):
    a_ref, w_ref = refs[:2]
    tiled_refs = refs[2:2 + n_tiled]
    col_refs = refs[2 + n_tiled:2 + n_tiled + n_col]
    out_refs = refs[2 + n_tiled + n_col:-2]
    p = pl.program_id(0)
    slab = w_ref.shape[0]
    r0 = pl.multiple_of(pl.program_id(1) * slab, slab)

    def step(w_next_ref, w_cur_ref, compute):
        w_next_ref[pl.ds(r0, slab), :] = w_ref[...].astype(BF16)
        if compute:
            chunks = []
            for r in range(0, a_ref.shape[0], DOT_ROWS):
                rows = pl.ds(r, DOT_ROWS)
                chunks.append((functools.partial(_dot_rows, a_ref, rows, w_cur_ref),
                               [t.at[rows, :] for t in tiled_refs] + list(col_refs),
                               [o.at[rows, :] for o in out_refs]))
            epilogue(p - 1, chunks)

    even = (p % 2) == 0
    pl.when((p > 0) & even)(functools.partial(step, refs[-2], refs[-1], True))
    pl.when((p > 0) & jnp.logical_not(even))(functools.partial(step, refs[-1], refs[-2], True))
    pl.when(p == 0)(functools.partial(step, refs[-2], refs[-1], False))


def _wpipe_matmul(name, a, w, layer, n_tiles, out_dtypes, epilogue=_store_epilogue, w_tile=lambda j: j,
                  o_tile=lambda j: j, tiled_extra=(), col_extra=(), tm=2048, tn=512, vmem_mib=VMEM_PIPELINED_MIB):
    m, k = a.shape
    n_i = m // tm
    next_tile = lambda p: w_tile(jnp.minimum(p, n_tiles - 1))
    cur_tile = lambda p: o_tile(jnp.maximum(p - 1, 0))
    row_block = lambda p, i: jnp.where(p == 0, 0, i)
    out_block = pl.BlockSpec((tm, tn), lambda p, i: (row_block(p, i), cur_tile(p)))
    in_specs = [pl.BlockSpec((tm, k), lambda p, i: (row_block(p, i), 0)),
                pl.BlockSpec((None, k // n_i, tn), lambda p, i: (layer, i, next_tile(p)))]
    in_specs += [out_block for _ in tiled_extra]
    in_specs += [pl.BlockSpec((e.shape[0], tn), lambda p, i: (0, cur_tile(p))) for e in col_extra]
    wb = pltpu.VMEM((k, tn), BF16)
    outs = pl.pallas_call(
        functools.partial(_wpipe_matmul_kernel, n_tiled=len(tiled_extra), n_col=len(col_extra), epilogue=epilogue),
        grid=(n_tiles + 1, n_i),
        in_specs=in_specs,
        out_specs=[out_block for _ in out_dtypes],
        out_shape=[jax.ShapeDtypeStruct((m, n_tiles * tn), dt) for dt in out_dtypes],
        scratch_shapes=[wb, wb],
        compiler_params=_params(("arbitrary", "arbitrary"), vmem_mib),
        name=name,
    )(a, w, *tiled_extra, *col_extra)
    return outs if len(outs) > 1 else outs[0]


def _ffn_up_kernel(a_ref, w1_ref, w3_ref, w2_ref, h_ref, w2b_ref, w1e_ref, w3e_ref, w1o_ref, w3o_ref, *, n_tiles):
    p = pl.program_id(0)
    i = pl.program_id(1)
    t = p - 1
    slab = w1_ref.shape[0]
    r0 = pl.multiple_of(i * slab, slab)
    slab2 = w2_ref.shape[0]
    q0 = pl.multiple_of(i * slab2, slab2)

    def compute(w1_next_ref, w3_next_ref, w1_cur_ref, w3_cur_ref):
        w1_next_ref[pl.ds(r0, slab), :] = w1_ref[...].astype(BF16)
        w3_next_ref[pl.ds(r0, slab), :] = w3_ref[...].astype(BF16)
        for r in range(0, a_ref.shape[0], DOT_ROWS):
            rows = pl.ds(r, DOT_ROWS)
            a = a_ref[rows, :]
            y1 = _dot(a, w1_cur_ref[...])
            y3 = _dot(a, w3_cur_ref[...])
            h_ref[rows, :] = (_silu(y1) * (0.5 * y3)).astype(h_ref.dtype)
        w2b_ref[pl.ds(q0, slab2), :] = w2_ref[...].astype(BF16)

    real = (t >= 0) & (t < n_tiles)
    even = (p % 2) == 0
    pl.when(real & even)(functools.partial(compute, w1e_ref, w3e_ref, w1o_ref, w3o_ref))
    pl.when(real & jnp.logical_not(even))(functools.partial(compute, w1o_ref, w3o_ref, w1e_ref, w3e_ref))

    @pl.when(p == 0)
    def _():
        w1e_ref[pl.ds(r0, slab), :] = w1_ref[...].astype(BF16)
        w3e_ref[pl.ds(r0, slab), :] = w3_ref[...].astype(BF16)

    @pl.when(t == n_tiles)
    def _():
        h_ref[...] = jnp.zeros_like(h_ref)
        w2b_ref[pl.ds(q0, slab2), :] = jnp.zeros((slab2, w2b_ref.shape[1]), BF16)


def _ffn_up(a, w1, w3, w2, layer, tm=2048):
    m, k = a.shape
    tf = FF_TILE
    n_tiles = D_FF // tf
    n_i = m // tm
    n_pos = D_FF_PAD // tf + 1
    next_tile = lambda p: jnp.minimum(p, n_tiles - 1)
    cur_tile = lambda p: jnp.clip(p - 1, 0, n_tiles - 1)
    out_tile = lambda p: jnp.maximum(p - 1, 0)
    a_block = lambda p, i: jnp.where(p == 0, 0, jnp.where(p == n_pos - 1, n_i - 1, i))
    h_block = lambda p, i: jnp.where(p == 0, 0, i)
    wb = pltpu.VMEM((k, tf), BF16)
    return pl.pallas_call(
        functools.partial(_ffn_up_kernel, n_tiles=n_tiles),
        grid=(n_pos, n_i),
        in_specs=[pl.BlockSpec((tm, k), lambda p, i: (a_block(p, i), 0)),
                  pl.BlockSpec((None, k // n_i, tf), lambda p, i: (layer, i, next_tile(p))),
                  pl.BlockSpec((None, k // n_i, tf), lambda p, i: (layer, i, next_tile(p))),
                  pl.BlockSpec((None, tf // n_i, k), lambda p, i: (layer, cur_tile(p) * n_i + i, 0))],
        out_specs=[pl.BlockSpec((tm, tf), lambda p, i: (h_block(p, i), out_tile(p))),
                   pl.BlockSpec((tf, k), lambda p, i: (out_tile(p), 0))],
        out_shape=[jax.ShapeDtypeStruct((m, D_FF_PAD), BF16),
                   jax.ShapeDtypeStruct((D_FF_PAD, k), BF16)],
        scratch_shapes=[wb, wb, wb, wb],
        compiler_params=_params(("arbitrary", "arbitrary"), VMEM_PIPELINED_MIB),
        name="ffn_up",
    )(a, w1, w3, w2)


def _matmul_res_kernel(a_ref, w_ref, r_ref, o_ref):
    @pl.when(pl.program_id(2) == 0)
    def _():
        o_ref[...] = r_ref[...] + _dot(a_ref[...], w_ref[...])

    @pl.when(pl.program_id(2) > 0)
    def _():
        o_ref[...] += _dot(a_ref[...], w_ref[...])


def _matmul_res(a, w, res, tk, tm=1024, tn=1024):
    m, k = a.shape
    n = w.shape[1]
    return pl.pallas_call(
        _matmul_res_kernel,
        grid=(m // tm, n // tn, k // tk),
        in_specs=[pl.BlockSpec((tm, tk), lambda i, j, kk: (i, kk)),
                  pl.BlockSpec((tk, tn), lambda i, j, kk: (kk, j)),
                  pl.BlockSpec((tm, tn), lambda i, j, kk: (i, j))],
        out_specs=pl.BlockSpec((tm, tn), lambda i, j, kk: (i, j)),
        out_shape=jax.ShapeDtypeStruct((m, n), F32),
        compiler_params=_params(("parallel", "parallel", "arbitrary"), VMEM_MATMUL_MIB),
        name="matmul_res",
    )(a, w, res)


def _ab_mixer_kernel(sinks_ref, qug_ref, kvp_ref, kvc_ref, qg_ref, kg_ref,
                     lng_ref, lnb_ref, ws_ref, bs_ref, o_ref):
    n = pl.program_id(1)
    w = WINDOW
    hd = A_HEAD_DIM

    key = lax.broadcasted_iota(jnp.int32, (2 * w, w), 0)
    qry = lax.broadcasted_iota(jnp.int32, (2 * w, w), 1)
    dist = qry + w - key
    visible = (dist >= 0) & (dist < w) & ((key >= w) | (n > 0))
    low = lax.broadcasted_iota(jnp.int32, (1, LANES), 1) < hd
    sub8 = lax.broadcasted_iota(jnp.int32, (SUBLANES, LANES), 0)
    ones_low = jnp.where(low, 1.0, 0.0)
    ones_high = 1.0 - ones_low
    sel2 = jnp.where(sub8 < SUBLANES // 2, ones_low, ones_high).astype(BF16)
    sink_row = sub8 == 0

    def split_bf16(x):
        hi = x.astype(BF16)
        return hi, (x - hi.astype(F32)).astype(BF16)

    kk = jnp.concatenate([kvp_ref[:, :A_KV_WIDTH], kvc_ref[:, :A_KV_WIDTH]], axis=0)
    vv = jnp.concatenate([kvp_ref[:, A_KV_WIDTH:], kvc_ref[:, A_KV_WIDTH:]], axis=0)
    seg_r = lax.broadcasted_iota(jnp.int32, (A_KV_WIDTH, A_KV_WIDTH), 0) // hd
    seg_c = lax.broadcasted_iota(jnp.int32, (A_KV_WIDTH, A_KV_WIDTH), 1) // hd
    seg = (seg_r == seg_c).astype(BF16)
    hi, lo = split_bf16(kk * kk)
    kss = _dot(hi, seg) + _dot(lo, seg)
    kgain = jnp.concatenate([kg_ref[...]] * A_KV_HEADS, axis=1)
    khat = kk * lax.rsqrt(kss * (1.0 / hd) + EPS) * kgain
    qgain = jnp.concatenate([qg_ref[...]] * 2, axis=1) * (hd ** -0.5 * LOG2E)

    def head_pair_operands(x, h):
        tile = x[:, (h // 2) * LANES:(h // 2 + 1) * LANES]
        swapped = pltpu.roll(tile, hd, axis=1)
        in_low, in_high = (tile, swapped) if h % 2 == 0 else (swapped, tile)
        return jnp.where(low, in_low, 0.0), jnp.where(low, 0.0, in_high)

    for h in range(A_KV_HEADS):
        k_low, k_high = head_pair_operands(khat, h)
        v_low, v_high = head_pair_operands(vv, h)
        k_pad = (k_low.astype(BF16), k_high.astype(BF16))
        v_aug = tuple(
            jnp.concatenate([
                jnp.concatenate([v_p, jnp.broadcast_to(ones_p, v_p.shape)], axis=1),
                jnp.concatenate([jnp.zeros((SUBLANES, LANES), F32), jnp.broadcast_to(ones_p, (SUBLANES, LANES))], axis=1),
            ], axis=0).astype(BF16)
            for v_p, ones_p in ((v_low, ones_low), (v_high, ones_high)))
        for c in range(h * A_GROUP // 2, (h + 1) * A_GROUP // 2):
            qc = qug_ref[:, c * LANES:(c + 1) * LANES]
            hi, lo = split_bf16(qc * qc)
            qss = (lax.dot_general(sel2, hi, NT_DIMS, preferred_element_type=F32)
                   + lax.dot_general(sel2, lo, NT_DIMS, preferred_element_type=F32))
            qscale = lax.rsqrt(qss * (1.0 / hd) + EPS)
            qg = (qc * qgain).astype(BF16)
            acc = jnp.zeros((w, 2 * LANES), F32)
            for parity in range(2):
                st = lax.dot_general(k_pad[parity], qg, NT_DIMS, preferred_element_type=F32)
                r = qscale[parity * (SUBLANES // 2):parity * (SUBLANES // 2) + 1]
                st = jnp.where(visible, st * r, -jnp.inf)
                s_sink = sinks_ref[2 * c + parity] * LOG2E
                mx = jnp.maximum(jnp.max(st, axis=0, keepdims=True), s_sink)
                e = jnp.exp2(st - mx)
                e_sink = jnp.where(sink_row, jnp.exp2(s_sink - mx), 0.0)
                e_aug = jnp.concatenate([e, e_sink], axis=0).astype(BF16)
                acc = acc + lax.dot_general(e_aug, v_aug[parity], TN_DIMS, preferred_element_type=F32)
            o_ref[:, c * LANES:(c + 1) * LANES] = (acc[:, :LANES] / acc[:, LANES:]).astype(o_ref.dtype)

    row = lax.broadcasted_iota(jnp.int32, (w, w), 0)
    col = lax.broadcasted_iota(jnp.int32, (w, w), 1)
    causal = row >= col
    u0 = A_WIDTH
    v0 = A_WIDTH + B_WIDTH
    for g in range(B_GROUPS):
        sl = slice(g * B_GROUP_DIM, (g + 1) * B_GROUP_DIM)
        x = qug_ref[:, v0 + g * B_GROUP_DIM:v0 + (g + 1) * B_GROUP_DIM]
        mu = jnp.mean(x, axis=-1, keepdims=True)
        d = x - mu
        var = jnp.mean(d * d, axis=-1, keepdims=True)
        vn = d * lax.rsqrt(var + EPS) * lng_ref[:, sl] + lnb_ref[:, sl]
        wg = jnp.where(causal, ws_ref[g], 0.0).astype(BF16)
        sg = _dot(wg, vn.astype(BF16)) + bs_ref[:, g:g + 1]
        o_ref[:, u0 + g * B_GROUP_DIM:u0 + (g + 1) * B_GROUP_DIM] = (
            qug_ref[:, u0 + g * B_GROUP_DIM:u0 + (g + 1) * B_GROUP_DIM] * sg).astype(o_ref.dtype)


def _ab_mixer(proj, batch, seq, q_gain, k_gain, sinks, ln_g, ln_b, w_s, b_s):
    nb = seq // WINDOW
    kv_blk = (A_WIDTH + 2 * B_WIDTH) // (2 * A_KV_WIDTH)
    row_blk = lambda b, n: b * nb + n
    return pl.pallas_call(
        _ab_mixer_kernel,
        grid=(batch, nb),
        in_specs=[
            pl.BlockSpec(memory_space=pltpu.SMEM),
            pl.BlockSpec((WINDOW, A_WIDTH + 2 * B_WIDTH), lambda b, n: (row_blk(b, n), 0)),
            pl.BlockSpec((WINDOW, 2 * A_KV_WIDTH), lambda b, n: (b * nb + jnp.maximum(n - 1, 0), kv_blk)),
            pl.BlockSpec((WINDOW, 2 * A_KV_WIDTH), lambda b, n: (row_blk(b, n), kv_blk)),
            pl.BlockSpec((1, A_HEAD_DIM), lambda b, n: (0, 0)),
            pl.BlockSpec((1, A_HEAD_DIM), lambda b, n: (0, 0)),
            pl.BlockSpec((1, B_WIDTH), lambda b, n: (0, 0)),
            pl.BlockSpec((1, B_WIDTH), lambda b, n: (0, 0)),
            pl.BlockSpec((B_GROUPS, WINDOW, WINDOW), lambda b, n: (0, 0, 0)),
            pl.BlockSpec((WINDOW, B_GROUPS), lambda b, n: (0, 0)),
        ],
        out_specs=pl.BlockSpec((WINDOW, A_WIDTH + B_WIDTH), lambda b, n: (row_blk(b, n), 0)),
        out_shape=jax.ShapeDtypeStruct((batch * seq, A_WIDTH + B_WIDTH), BF16),
        compiler_params=_params(("parallel", "arbitrary"), VMEM_SMALL_MIB),
        name="ab_mixer",
    )(sinks, proj, proj, proj,
      q_gain.reshape(1, A_HEAD_DIM), k_gain.reshape(1, A_HEAD_DIM),
      ln_g.reshape(1, B_WIDTH), ln_b.reshape(1, B_WIDTH), w_s, b_s.T)


def _hgrn2_kernel(og_ref, qvg_ref, lf_ref, k_ref, o_ref, st_ref):
    c = C_SUB
    hw = C_HEADS_PER_STEP * C_DIM

    @pl.when(pl.program_id(2) == 0)
    def _():
        st_ref[...] = jnp.zeros_like(st_ref)

    row = lax.broadcasted_iota(jnp.int32, (c, c), 0)
    col = lax.broadcasted_iota(jnp.int32, (c, c), 1)
    row1 = lax.broadcasted_iota(jnp.int32, (c, 1), 0)
    cumsum_mat = (row >= col).astype(BF16)
    diff_bits = row ^ col
    level = jnp.zeros((c, c), jnp.int32)
    for b in range(1, c.bit_length() - 1):
        level = level + (diff_bits >= (1 << b)).astype(jnp.int32)
    level = jnp.where(col < row, level, jnp.where(col == row, -1, -2))
    og = og_ref[...]

    for h in range(C_HEADS_PER_STEP):
        sl = slice(h * C_DIM, (h + 1) * C_DIM)
        st = st_ref[h]
        for sub in range(C_CHUNK // c):
            rows = slice(sub * c, (sub + 1) * c)
            log2_f = lf_ref[rows, sl]
            k = k_ref[rows, sl].astype(F32)

            p1 = log2_f.astype(BF16)
            p2 = (log2_f - p1.astype(F32)).astype(BF16)
            cg = _dot(cumsum_mat, p1) + _dot(cumsum_mat, p2)

            q = qvg_ref[rows, sl].astype(F32)
            v = qvg_ref[rows, hw + h * C_DIM:hw + (h + 1) * C_DIM]
            o = lax.dot_general((q * jnp.exp2(cg)).astype(BF16), st.astype(BF16), NT_DIMS,
                                preferred_element_type=F32)

            att = jnp.where(level == -1, jnp.sum(q * k, axis=-1, keepdims=True), 0.0)
            part = lax.dot_general((q * jnp.exp2(log2_f)).astype(BF16), k_ref[rows, sl], NT_DIMS,
                                   preferred_element_type=F32)
            att = jnp.where(level == 0, part, att)
            cg3 = cg.reshape(c // SUBLANES, SUBLANES, C_DIM)
            sub_row = lambda r: jnp.broadcast_to(cg3[:, r:r + 1, :], cg3.shape).reshape(c, C_DIM)
            n = 2
            while n < c:
                if 4 * n == SUBLANES:
                    ref = jnp.where((row1 & (2 * n)) != 0, sub_row(3 * n - 1), sub_row(n - 1))
                elif 2 * n == SUBLANES:
                    ref = sub_row(n - 1)
                else:
                    ref = jnp.concatenate([jnp.broadcast_to(cg[lo + n - 1:lo + n, :], (2 * n, C_DIM))
                                           for lo in range(0, c, 2 * n)], axis=0)
                e = jnp.exp2(-jnp.abs(cg - ref))
                part = lax.dot_general((q * e).astype(BF16), (k * e).astype(BF16), NT_DIMS,
                                       preferred_element_type=F32)
                att = jnp.where(level == n.bit_length() - 1, part, att)
                n *= 2

            o = o + _dot(att.astype(BF16), v)

            last = cg[c - 1:c, :]
            kd = (k * jnp.exp2(last - cg)).astype(BF16)
            st = st * jnp.exp2(last) + lax.dot_general(v, kd, TN_DIMS, preferred_element_type=F32)

            y = o * lax.rsqrt(jnp.mean(o * o, axis=-1, keepdims=True) + EPS) * og
            gate = qvg_ref[rows, 2 * hw + h * C_DIM:2 * hw + (h + 1) * C_DIM]
            o_ref[rows, sl] = (y * gate.astype(F32)).astype(o_ref.dtype)
        st_ref[h] = st


def _hgrn2(qvg, log2_f, k, o_gain, batch, seq):
    nc = seq // C_CHUNK
    hw = C_HEADS_PER_STEP * C_DIM
    nh = C_WIDTH // hw
    blk = lambda width: pl.BlockSpec((C_CHUNK, width), lambda b, g, c: (b * nc + c, g))
    return pl.pallas_call(
        _hgrn2_kernel,
        grid=(batch, nh, nc),
        in_specs=[pl.BlockSpec((1, C_DIM), lambda b, g, c: (0, 0)),
                  blk(3 * hw), blk(hw), blk(hw)],
        out_specs=blk(hw),
        out_shape=jax.ShapeDtypeStruct((batch * seq, C_WIDTH), BF16),
        scratch_shapes=[pltpu.VMEM((C_HEADS_PER_STEP, C_DIM, C_DIM), F32)],
        compiler_params=_params(("parallel", "parallel", "arbitrary"), VMEM_MATMUL_MIB),
        name="hgrn2",
    )(o_gain.reshape(1, C_DIM), qvg, log2_f, k)


def _ffn(x, norm_g, w1, w3, w2, layer):
    h, w2b = _ffn_up(_rmsnorm(x, norm_g[layer]), w1, w3, w2, layer)
    return _matmul_res(h, w2b, x, tk=D_FF_PAD // 4)


def kernel(x, ffn1_norm, ffn1_w1, ffn1_w3, ffn1_w2, mix_norm, ffn2_norm, ffn2_w1, ffn2_w3, ffn2_w2, ab_w_in, ab_q_norm, ab_k_norm, ab_sinks, ab_v_ln_g, ab_v_ln_b, ab_w_s, ab_b_s, ab_w_out, c_w_in, c_lb_logits, c_o_norm, c_w_out):
    batch, seq, d = x.shape
    depth = ffn1_norm.shape[0]
    tn = 512
    x = x.reshape(batch * seq, d)
    for l in range(depth):
        x = _ffn(x, ffn1_norm, ffn1_w1, ffn1_w3, ffn1_w2, l)
        hn = _rmsnorm(x, mix_norm[l])
        j = l // 2
        if l % 2 == 0:
            kv_tile = A_WIDTH // tn
            n_tiles = AB_IN // tn
            o_tile = lambda t: jnp.where(t < kv_tile, t, jnp.where(t == kv_tile, n_tiles - 1, t - 1))
            proj = _wpipe_matmul("ab_in_proj", hn, ab_w_in, j, n_tiles, [F32], o_tile=o_tile,
                                 epilogue=functools.partial(_act_epilogue, act=_gelu, act_from=kv_tile + 1))
            mixed = _ab_mixer(proj, batch, seq, ab_q_norm[j], ab_k_norm[j], ab_sinks[j],
                              ab_v_ln_g[j], ab_v_ln_b[j], ab_w_s[j], ab_b_s[j])
            x = _wpipe_matmul("ab_out_proj", mixed, ab_w_out, j, d // tn, [F32], _residual_epilogue,
                              tiled_extra=[x], vmem_mib=VMEM_RESIDUAL_MIB)
        else:
            ct = C_WIDTH // tn
            gt = C_HEADS_PER_STEP * C_DIM // tn
            qvg = _wpipe_matmul("c_in_proj_qvg", hn, c_w_in, j, 3 * ct, [BF16],
                                functools.partial(_act_epilogue, act=_silu, act_from=2 * ct),
                                w_tile=lambda t: jnp.where(t < ct, t, t + ct),
                                o_tile=lambda t: (t % ct) // gt * (3 * gt) + t // ct * gt + t % gt)
            log2_f, k = _wpipe_matmul("c_in_proj_f", hn, c_w_in, j, ct, [F32, BF16],
                                      functools.partial(_forget_gate_epilogue, layer=l),
                                      w_tile=lambda t: t + ct, col_extra=[c_lb_logits])
            mixed = _hgrn2(qvg, log2_f, k, c_o_norm[j], batch, seq)
            x = _wpipe_matmul("c_out_proj", mixed, c_w_out, j, d // tn, [F32], _residual_epilogue,
                              tiled_extra=[x], vmem_mib=VMEM_RESIDUAL_MIB)
        x = _ffn(x, ffn2_norm, ffn2_w1, ffn2_w3, ffn2_w2, l)
    return x.reshape(batch, seq, d)
```

```python
import functools

import jax
import jax.numpy as jnp
from jax import lax
from jax.experimental import pallas as pl
from jax.experimental.pallas import tpu as pltpu

F32 = jnp.float32
BF16 = jnp.bfloat16

D_FF = 11008
FF_TILE = 256
D_FF_PAD = 11264
EPS = 1e-6

A_HEADS = 32
A_KV_HEADS = 4
A_GROUP = A_HEADS // A_KV_HEADS
A_HEAD_DIM = 64
WINDOW = 128
A_WIDTH = A_HEADS * A_HEAD_DIM
A_KV_WIDTH = A_KV_HEADS * A_HEAD_DIM
B_GROUPS = 16
B_GROUP_DIM = 128
B_WIDTH = B_GROUPS * B_GROUP_DIM
AB_IN = A_WIDTH + 2 * A_KV_WIDTH + 2 * B_WIDTH

C_HEADS = 32
C_DIM = 128
C_WIDTH = C_HEADS * C_DIM
C_CHUNK = 256
C_SUB = 128
C_HEADS_PER_STEP = 32

LANES = 128
SUBLANES = 8
MIB = 1 << 20
DOT_ROWS = 1024
VMEM_SMALL_MIB = 32
VMEM_MATMUL_MIB = 56
VMEM_PIPELINED_MIB = 60
VMEM_RESIDUAL_MIB = 62
LOG2E = 1.4426950408889634

NT_DIMS = (((1,), (1,)), ((), ()))
TN_DIMS = (((0,), (0,)), ((), ()))


def _params(semantics, vmem_mib):
    return pltpu.CompilerParams(dimension_semantics=semantics, vmem_limit_bytes=vmem_mib * MIB)


def _dot(a, b):
    return jnp.dot(a, b, preferred_element_type=F32)


def _dot_rows(a_ref, rows, w_ref):
    return _dot(a_ref[rows, :], w_ref[...])


def _gelu(y):
    return 0.5 * y * (1.0 + lax.erf(y * (2.0 ** -0.5)))


def _silu(y):
    half = 0.5 * y
    return half + half * jnp.tanh(half)


def _rmsnorm_kernel(x_ref, g_ref, o_ref):
    x = x_ref[...]
    r = lax.rsqrt(jnp.mean(x * x, axis=-1, keepdims=True) + EPS)
    o_ref[...] = (x * r * g_ref[...]).astype(o_ref.dtype)


def _rmsnorm(x, g, rows=512):
    m, d = x.shape
    return pl.pallas_call(
        _rmsnorm_kernel,
        grid=(m // rows,),
        in_specs=[pl.BlockSpec((rows, d), lambda i: (i, 0)),
                  pl.BlockSpec((1, d), lambda i: (0, 0))],
        out_specs=pl.BlockSpec((rows, d), lambda i: (i, 0)),
        out_shape=jax.ShapeDtypeStruct((m, d), BF16),
        compiler_params=_params(("parallel",), VMEM_SMALL_MIB),
        name="rmsnorm",
    )(x, g.reshape(1, d))


def _store_epilogue(j, chunks):
    for dot, _, out_refs in chunks:
        out_refs[0][...] = dot().astype(out_refs[0].dtype)


def _residual_epilogue(j, chunks):
    for dot, extra_refs, out_refs in chunks:
        out_refs[0][...] = extra_refs[0][...] + dot()


def _act_epilogue(j, chunks, *, act, act_from):
    @pl.when(j >= act_from)
    def _():
        for dot, _, out_refs in chunks:
            out_refs[0][...] = act(dot()).astype(out_refs[0].dtype)

    @pl.when(j < act_from)
    def _():
        _store_epilogue(j, chunks)


def _forget_gate_epilogue(j, chunks, *, layer):
    logits = chunks[0][1][0][...]
    lmax = jnp.max(logits, axis=0, keepdims=True)
    ex = jnp.exp(logits - lmax)
    prob = ex / jnp.sum(ex, axis=0, keepdims=True)
    lb = jnp.sum(prob[:layer + 1], axis=0, keepdims=True) - prob[0:1]
    c1 = 0.5 * (1.0 - lb)
    c0 = lb + c1
    for dot, _, out_refs in chunks:
        out_refs[0][...] = jnp.log2(c0 + c1 * jnp.tanh(0.5 * dot()))


def _wpipe_matmul_kernel(*refs, n_tiled, n_col, epilogue):
    a_ref, w_ref = refs[:2]
    tiled_refs = refs[2:2 + n_tiled]
    col_refs = refs[2 + n_tiled:2 + n_tiled + n_col]
    out_refs = refs[2 + n_tiled + n_col:-2]
    p = pl.program_id(0)
    slab = w_ref.shape[0]
    r0 = pl.multiple_of(pl.program_id(1) * slab, slab)

    def step(w_next_ref, w_cur_ref, compute):
        w_next_ref[pl.ds(r0, slab), :] = w_ref[...].astype(BF16)
        if compute:
            chunks = []
            for r in range(0, a_ref.shape[0], DOT_ROWS):
                rows = pl.ds(r, DOT_ROWS)
                chunks.append((functools.partial(_dot_rows, a_ref, rows, w_cur_ref),
                               [t.at[rows, :] for t in tiled_refs] + list(col_refs),
                               [o.at[rows, :] for o in out_refs]))
            epilogue(p - 1, chunks)

    even = (p % 2) == 0
    pl.when((p > 0) & even)(functools.partial(step, refs[-2], refs[-1], True))
    pl.when((p > 0) & jnp.logical_not(even))(functools.partial(step, refs[-1], refs[-2], True))
    pl.when(p == 0)(functools.partial(step, refs[-2], refs[-1], False))


def _wpipe_matmul(name, a, w, layer, n_tiles, out_dtypes, epilogue=_store_epilogue, w_tile=lambda j: j,
                  o_tile=lambda j: j, tiled_extra=(), col_extra=(), tm=2048, tn=512, vmem_mib=VMEM_PIPELINED_MIB):
    m, k = a.shape
    n_i = m // tm
    next_tile = lambda p: w_tile(jnp.minimum(p, n_tiles - 1))
    cur_tile = lambda p: o_tile(jnp.maximum(p - 1, 0))
    row_block = lambda p, i: jnp.where(p == 0, 0, i)
    out_block = pl.BlockSpec((tm, tn), lambda p, i: (row_block(p, i), cur_tile(p)))
    in_specs = [pl.BlockSpec((tm, k), lambda p, i: (row_block(p, i), 0)),
                pl.BlockSpec((None, k // n_i, tn), lambda p, i: (layer, i, next_tile(p)))]
    in_specs += [out_block for _ in tiled_extra]
    in_specs += [pl.BlockSpec((e.shape[0], tn), lambda p, i: (0, cur_tile(p))) for e in col_extra]
    wb = pltpu.VMEM((k, tn), BF16)
    outs = pl.pallas_call(
        functools.partial(_wpipe_matmul_kernel, n_tiled=len(tiled_extra), n_col=len(col_extra), epilogue=epilogue),
        grid=(n_tiles + 1, n_i),
        in_specs=in_specs,
        out_specs=[out_block for _ in out_dtypes],
        out_shape=[jax.ShapeDtypeStruct((m, n_tiles * tn), dt) for dt in out_dtypes],
        scratch_shapes=[wb, wb],
        compiler_params=_params(("arbitrary", "arbitrary"), vmem_mib),
        name=name,
    )(a, w, *tiled_extra, *col_extra)
    return outs if len(outs) > 1 else outs[0]


def _ffn_up_kernel(a_ref, w1_ref, w3_ref, w2_ref, h_ref, w2b_ref, w1e_ref, w3e_ref, w1o_ref, w3o_ref, *, n_tiles):
    p = pl.program_id(0)
    i = pl.program_id(1)
    t = p - 1
    slab = w1_ref.shape[0]
    r0 = pl.multiple_of(i * slab, slab)
    slab2 = w2_ref.shape[0]
    q0 = pl.multiple_of(i * slab2, slab2)

    def compute(w1_next_ref, w3_next_ref, w1_cur_ref, w3_cur_ref):
        w1_next_ref[pl.ds(r0, slab), :] = w1_ref[...].astype(BF16)
        w3_next_ref[pl.ds(r0, slab), :] = w3_ref[...].astype(BF16)
        for r in range(0, a_ref.shape[0], DOT_ROWS):
            rows = pl.ds(r, DOT_ROWS)
            a = a_ref[rows, :]
            y1 = _dot(a, w1_cur_ref[...])
            y3 = _dot(a, w3_cur_ref[...])
            h_ref[rows, :] = (_silu(y1) * (0.5 * y3)).astype(h_ref.dtype)
        w2b_ref[pl.ds(q0, slab2), :] = w2_ref[...].astype(BF16)

    real = (t >= 0) & (t < n_tiles)
    even = (p % 2) == 0
    pl.when(real & even)(functools.partial(compute, w1e_ref, w3e_ref, w1o_ref, w3o_ref))
    pl.when(real & jnp.logical_not(even))(functools.partial(compute, w1o_ref, w3o_ref, w1e_ref, w3e_ref))

    @pl.when(p == 0)
    def _():
        w1e_ref[pl.ds(r0, slab), :] = w1_ref[...].astype(BF16)
        w3e_ref[pl.ds(r0, slab), :] = w3_ref[...].astype(BF16)

    @pl.when(t == n_tiles)
    def _():
        h_ref[...] = jnp.zeros_like(h_ref)
        w2b_ref[pl.ds(q0, slab2), :] = jnp.zeros((slab2, w2b_ref.shape[1]), BF16)


def _ffn_up(a, w1, w3, w2, layer, tm=2048):
    m, k = a.shape
    tf = FF_TILE
    n_tiles = D_FF // tf
    n_i = m // tm
    n_pos = D_FF_PAD // tf + 1
    next_tile = lambda p: jnp.minimum(p, n_tiles - 1)
    cur_tile = lambda p: jnp.clip(p - 1, 0, n_tiles - 1)
    out_tile = lambda p: jnp.maximum(p - 1, 0)
    a_block = lambda p, i: jnp.where(p == 0, 0, jnp.where(p == n_pos - 1, n_i - 1, i))
    h_block = lambda p, i: jnp.where(p == 0, 0, i)
    wb = pltpu.VMEM((k, tf), BF16)
    return pl.pallas_call(
        functools.partial(_ffn_up_kernel, n_tiles=n_tiles),
        grid=(n_pos, n_i),
        in_specs=[pl.BlockSpec((tm, k), lambda p, i: (a_block(p, i), 0)),
                  pl.BlockSpec((None, k // n_i, tf), lambda p, i: (layer, i, next_tile(p))),
                  pl.BlockSpec((None, k // n_i, tf), lambda p, i: (layer, i, next_tile(p))),
                  pl.BlockSpec((None, tf // n_i, k), lambda p, i: (layer, cur_tile(p) * n_i + i, 0))],
        out_specs=[pl.BlockSpec((tm, tf), lambda p, i: (h_block(p, i), out_tile(p))),
                   pl.BlockSpec((tf, k), lambda p, i: (out_tile(p), 0))],
        out_shape=[jax.ShapeDtypeStruct((m, D_FF_PAD), BF16),
                   jax.ShapeDtypeStruct((D_FF_PAD, k), BF16)],
        scratch_shapes=[wb, wb, wb, wb],
        compiler_params=_params(("arbitrary", "arbitrary"), VMEM_PIPELINED_MIB),
        name="ffn_up",
    )(a, w1, w3, w2)


def _matmul_res_kernel(a_ref, w_ref, r_ref, o_ref):
    @pl.when(pl.program_id(2) == 0)
    def _():
        o_ref[...] = r_ref[...] + _dot(a_ref[...], w_ref[...])

    @pl.when(pl.program_id(2) > 0)
    def _():
        o_ref[...] += _dot(a_ref[...], w_ref[...])


def _matmul_res(a, w, res, tk, tm=1024, tn=1024):
    m, k = a.shape
    n = w.shape[1]
    return pl.pallas_call(
        _matmul_res_kernel,
        grid=(m // tm, n // tn, k // tk),
        in_specs=[pl.BlockSpec((tm, tk), lambda i, j, kk: (i, kk)),
                  pl.BlockSpec((tk, tn), lambda i, j, kk: (kk, j)),
                  pl.BlockSpec((tm, tn), lambda i, j, kk: (i, j))],
        out_specs=pl.BlockSpec((tm, tn), lambda i, j, kk: (i, j)),
        out_shape=jax.ShapeDtypeStruct((m, n), F32),
        compiler_params=_params(("parallel", "parallel", "arbitrary"), VMEM_MATMUL_MIB),
        name="matmul_res",
    )(a, w, res)


def _ab_mixer_kernel(sinks_ref, qug_ref, kvp_ref, kvc_ref, qg_ref, kg_ref,
                     lng_ref, lnb_ref, ws_ref, bs_ref, o_ref):
    n = pl.program_id(1)
    w = WINDOW
    hd = A_HEAD_DIM

    key = lax.broadcasted_iota(jnp.int32, (2 * w, w), 0)
    qry = lax.broadcasted_iota(jnp.int32, (2 * w, w), 1)
    dist = qry + w - key
    visible = (dist >= 0) & (dist < w) & ((key >= w) | (n > 0))
    low = lax.broadcasted_iota(jnp.int32, (1, LANES), 1) < hd
    sub8 = lax.broadcasted_iota(jnp.int32, (SUBLANES, LANES), 0)
    ones_low = jnp.where(low, 1.0, 0.0)
    ones_high = 1.0 - ones_low
    sel2 = jnp.where(sub8 < SUBLANES // 2, ones_low, ones_high).astype(BF16)
    sink_row = sub8 == 0

    def split_bf16(x):
        hi = x.astype(BF16)
        return hi, (x - hi.astype(F32)).astype(BF16)

    kk = jnp.concatenate([kvp_ref[:, :A_KV_WIDTH], kvc_ref[:, :A_KV_WIDTH]], axis=0)
    vv = jnp.concatenate([kvp_ref[:, A_KV_WIDTH:], kvc_ref[:, A_KV_WIDTH:]], axis=0)
    seg_r = lax.broadcasted_iota(jnp.int32, (A_KV_WIDTH, A_KV_WIDTH), 0) // hd
    seg_c = lax.broadcasted_iota(jnp.int32, (A_KV_WIDTH, A_KV_WIDTH), 1) // hd
    seg = (seg_r == seg_c).astype(BF16)
    hi, lo = split_bf16(kk * kk)
    kss = _dot(hi, seg) + _dot(lo, seg)
    kgain = jnp.concatenate([kg_ref[...]] * A_KV_HEADS, axis=1)
    khat = kk * lax.rsqrt(kss * (1.0 / hd) + EPS) * kgain
    qgain = jnp.concatenate([qg_ref[...]] * 2, axis=1) * (hd ** -0.5 * LOG2E)

    def head_pair_operands(x, h):
        tile = x[:, (h // 2) * LANES:(h // 2 + 1) * LANES]
        swapped = pltpu.roll(tile, hd, axis=1)
        in_low, in_high = (tile, swapped) if h % 2 == 0 else (swapped, tile)
        return jnp.where(low, in_low, 0.0), jnp.where(low, 0.0, in_high)

    for h in range(A_KV_HEADS):
        k_low, k_high = head_pair_operands(khat, h)
        v_low, v_high = head_pair_operands(vv, h)
        k_pad = (k_low.astype(BF16), k_high.astype(BF16))
        v_aug = tuple(
            jnp.concatenate([
                jnp.concatenate([v_p, jnp.broadcast_to(ones_p, v_p.shape)], axis=1),
                jnp.concatenate([jnp.zeros((SUBLANES, LANES), F32), jnp.broadcast_to(ones_p, (SUBLANES, LANES))], axis=1),
            ], axis=0).astype(BF16)
            for v_p, ones_p in ((v_low, ones_low), (v_high, ones_high)))
        for c in range(h * A_GROUP // 2, (h + 1) * A_GROUP // 2):
            qc = qug_ref[:, c * LANES:(c + 1) * LANES]
            hi, lo = split_bf16(qc * qc)
            qss = (lax.dot_general(sel2, hi, NT_DIMS, preferred_element_type=F32)
                   + lax.dot_general(sel2, lo, NT_DIMS, preferred_element_type=F32))
            qscale = lax.rsqrt(qss * (1.0 / hd) + EPS)
            qg = (qc * qgain).astype(BF16)
            acc = jnp.zeros((w, 2 * LANES), F32)
            for parity in range(2):
                st = lax.dot_general(k_pad[parity], qg, NT_DIMS, preferred_element_type=F32)
                r = qscale[parity * (SUBLANES // 2):parity * (SUBLANES // 2) + 1]
                st = jnp.where(visible, st * r, -jnp.inf)
                s_sink = sinks_ref[2 * c + parity] * LOG2E
                mx = jnp.maximum(jnp.max(st, axis=0, keepdims=True), s_sink)
                e = jnp.exp2(st - mx)
                e_sink = jnp.where(sink_row, jnp.exp2(s_sink - mx), 0.0)
                e_aug = jnp.concatenate([e, e_sink], axis=0).astype(BF16)
                acc = acc + lax.dot_general(e_aug, v_aug[parity], TN_DIMS, preferred_element_type=F32)
            o_ref[:, c * LANES:(c + 1) * LANES] = (acc[:, :LANES] / acc[:, LANES:]).astype(o_ref.dtype)

    row = lax.broadcasted_iota(jnp.int32, (w, w), 0)
    col = lax.broadcasted_iota(jnp.int32, (w, w), 1)
    causal = row >= col
    u0 = A_WIDTH
    v0 = A_WIDTH + B_WIDTH
    for g in range(B_GROUPS):
        sl = slice(g * B_GROUP_DIM, (g + 1) * B_GROUP_DIM)
        x = qug_ref[:, v0 + g * B_GROUP_DIM:v0 + (g + 1) * B_GROUP_DIM]
        mu = jnp.mean(x, axis=-1, keepdims=True)
        d = x - mu
        var = jnp.mean(d * d, axis=-1, keepdims=True)
        vn = d * lax.rsqrt(var + EPS) * lng_ref[:, sl] + lnb_ref[:, sl]
        wg = jnp.where(causal, ws_ref[g], 0.0).astype(BF16)
        sg = _dot(wg, vn.astype(BF16)) + bs_ref[:, g:g + 1]
        o_ref[:, u0 + g * B_GROUP_DIM:u0 + (g + 1) * B_GROUP_DIM] = (
            qug_ref[:, u0 + g * B_GROUP_DIM:u0 + (g + 1) * B_GROUP_DIM] * sg).astype(o_ref.dtype)


def _ab_mixer(proj, batch, seq, q_gain, k_gain, sinks, ln_g, ln_b, w_s, b_s):
    nb = seq // WINDOW
    kv_blk = (A_WIDTH + 2 * B_WIDTH) // (2 * A_KV_WIDTH)
    row_blk = lambda b, n: b * nb + n
    return pl.pallas_call(
        _ab_mixer_kernel,
        grid=(batch, nb),
        in_specs=[
            pl.BlockSpec(memory_space=pltpu.SMEM),
            pl.BlockSpec((WINDOW, A_WIDTH + 2 * B_WIDTH), lambda b, n: (row_blk(b, n), 0)),
            pl.BlockSpec((WINDOW, 2 * A_KV_WIDTH), lambda b, n: (b * nb + jnp.maximum(n - 1, 0), kv_blk)),
            pl.BlockSpec((WINDOW, 2 * A_KV_WIDTH), lambda b, n: (row_blk(b, n), kv_blk)),
            pl.BlockSpec((1, A_HEAD_DIM), lambda b, n: (0, 0)),
            pl.BlockSpec((1, A_HEAD_DIM), lambda b, n: (0, 0)),
            pl.BlockSpec((1, B_WIDTH), lambda b, n: (0, 0)),
            pl.BlockSpec((1, B_WIDTH), lambda b, n: (0, 0)),
            pl.BlockSpec((B_GROUPS, WINDOW, WINDOW), lambda b, n: (0, 0, 0)),
            pl.BlockSpec((WINDOW, B_GROUPS), lambda b, n: (0, 0)),
        ],
        out_specs=pl.BlockSpec((WINDOW, A_WIDTH + B_WIDTH), lambda b, n: (row_blk(b, n), 0)),
        out_shape=jax.ShapeDtypeStruct((batch * seq, A_WIDTH + B_WIDTH), BF16),
        compiler_params=_params(("parallel", "arbitrary"), VMEM_SMALL_MIB),
        name="ab_mixer",
    )(sinks, proj, proj, proj,
      q_gain.reshape(1, A_HEAD_DIM), k_gain.reshape(1, A_HEAD_DIM),
      ln_g.reshape(1, B_WIDTH), ln_b.reshape(1, B_WIDTH), w_s, b_s.T)


def _hgrn2_kernel(og_ref, qvg_ref, lf_ref, o_ref, st_ref):
    c = C_SUB
    hw = C_HEADS_PER_STEP * C_DIM

    @pl.when(pl.program_id(2) == 0)
    def _():
        st_ref[...] = jnp.zeros_like(st_ref)

    row = lax.broadcasted_iota(jnp.int32, (c, c), 0)
    col = lax.broadcasted_iota(jnp.int32, (c, c), 1)
    row1 = lax.broadcasted_iota(jnp.int32, (c, 1), 0)
    cumsum_mat = (row >= col).astype(BF16)
    diff_bits = row ^ col
    level = jnp.zeros((c, c), jnp.int32)
    for b in range(1, c.bit_length() - 1):
        level = level + (diff_bits >= (1 << b)).astype(jnp.int32)
    level = jnp.where(col < row, level, jnp.where(col == row, -1, -2))
    og = og_ref[...]

    for h in range(C_HEADS_PER_STEP):
        sl = slice(h * C_DIM, (h + 1) * C_DIM)
        st = st_ref[h]
        for sub in range(C_CHUNK // c):
            rows = slice(sub * c, (sub + 1) * c)
            log2_f = lf_ref[rows, sl]
            f = jnp.exp2(log2_f)
            k = 1.0 - f

            p1 = log2_f.astype(BF16)
            p2 = (log2_f - p1.astype(F32)).astype(BF16)
            cg = _dot(cumsum_mat, p1) + _dot(cumsum_mat, p2)

            q = qvg_ref[rows, sl].astype(F32)
            v = qvg_ref[rows, hw + h * C_DIM:hw + (h + 1) * C_DIM]
            o = lax.dot_general((q * jnp.exp2(cg)).astype(BF16), st.astype(BF16), NT_DIMS,
                                preferred_element_type=F32)

            att = jnp.where(level == -1, jnp.sum(q * k, axis=-1, keepdims=True), 0.0)
            part = lax.dot_general((q * f).astype(BF16), k.astype(BF16), NT_DIMS, preferred_element_type=F32)
            att = jnp.where(level == 0, part, att)
            cg3 = cg.reshape(c // SUBLANES, SUBLANES, C_DIM)
            sub_row = lambda r: jnp.broadcast_to(cg3[:, r:r + 1, :], cg3.shape).reshape(c, C_DIM)
            n = 2
            while n < c:
                if 4 * n == SUBLANES:
                    ref = jnp.where((row1 & (2 * n)) != 0, sub_row(3 * n - 1), sub_row(n - 1))
                elif 2 * n == SUBLANES:
                    ref = sub_row(n - 1)
                else:
                    ref = jnp.concatenate([jnp.broadcast_to(cg[lo + n - 1:lo + n, :], (2 * n, C_DIM))
                                           for lo in range(0, c, 2 * n)], axis=0)
                e = jnp.exp2(-jnp.abs(cg - ref))
                part = lax.dot_general((q * e).astype(BF16), (k * e).astype(BF16), NT_DIMS,
                                       preferred_element_type=F32)
                att = jnp.where(level == n.bit_length() - 1, part, att)
                n *= 2

            o = o + _dot(att.astype(BF16), v)

            last = cg[c - 1:c, :]
            kd = (k * jnp.exp2(last - cg)).astype(BF16)
            st = st * jnp.exp2(last) + lax.dot_general(v, kd, TN_DIMS, preferred_element_type=F32)

            y = o * lax.rsqrt(jnp.mean(o * o, axis=-1, keepdims=True) + EPS) * og
            gate = qvg_ref[rows, 2 * hw + h * C_DIM:2 * hw + (h + 1) * C_DIM]
            o_ref[rows, sl] = (y * gate.astype(F32)).astype(o_ref.dtype)
        st_ref[h] = st


def _hgrn2(qvg, log2_f, o_gain, batch, seq):
    nc = seq // C_CHUNK
    hw = C_HEADS_PER_STEP * C_DIM
    nh = C_WIDTH // hw
    blk = lambda width: pl.BlockSpec((C_CHUNK, width), lambda b, g, c: (b * nc + c, g))
    return pl.pallas_call(
        _hgrn2_kernel,
        grid=(batch, nh, nc),
        in_specs=[pl.BlockSpec((1, C_DIM), lambda b, g, c: (0, 0)),
                  blk(3 * hw), blk(hw)],
        out_specs=blk(hw),
        out_shape=jax.ShapeDtypeStruct((batch * seq, C_WIDTH), BF16),
        scratch_shapes=[pltpu.VMEM((C_HEADS_PER_STEP, C_DIM, C_DIM), F32)],
        compiler_params=_params(("parallel", "parallel", "arbitrary"), VMEM_MATMUL_MIB),
        name="hgrn2",
    )(o_gain.reshape(1, C_DIM), qvg, log2_f)


def _ffn(x, norm_g, w1, w3, w2, layer):
    h, w2b = _ffn_up(_rmsnorm(x, norm_g[layer]), w1, w3, w2, layer)
    return _matmul_res(h, w2b, x, tk=D_FF_PAD // 4)


def kernel(x, ffn1_norm, ffn1_w1, ffn1_w3, ffn1_w2, mix_norm, ffn2_norm, ffn2_w1, ffn2_w3, ffn2_w2, ab_w_in, ab_q_norm, ab_k_norm, ab_sinks, ab_v_ln_g, ab_v_ln_b, ab_w_s, ab_b_s, ab_w_out, c_w_in, c_lb_logits, c_o_norm, c_w_out):
    batch, seq, d = x.shape
    depth = ffn1_norm.shape[0]
    tn = 512
    x = x.reshape(batch * seq, d)
    for l in range(depth):
        x = _ffn(x, ffn1_norm, ffn1_w1, ffn1_w3, ffn1_w2, l)
        hn = _rmsnorm(x, mix_norm[l])
        j = l // 2
        if l % 2 == 0:
            kv_tile = A_WIDTH // tn
            n_tiles = AB_IN // tn
            o_tile = lambda t: jnp.where(t < kv_tile, t, jnp.where(t == kv_tile, n_tiles - 1, t - 1))
            proj = _wpipe_matmul("ab_in_proj", hn, ab_w_in, j, n_tiles, [F32], o_tile=o_tile,
                                 epilogue=functools.partial(_act_epilogue, act=_gelu, act_from=kv_tile + 1))
            mixed = _ab_mixer(proj, batch, seq, ab_q_norm[j], ab_k_norm[j], ab_sinks[j],
                              ab_v_ln_g[j], ab_v_ln_b[j], ab_w_s[j], ab_b_s[j])
            x = _wpipe_matmul("ab_out_proj", mixed, ab_w_out, j, d // tn, [F32], _residual_epilogue,
                              tiled_extra=[x], vmem_mib=VMEM_RESIDUAL_MIB)
        else:
            ct = C_WIDTH // tn
            gt = C_HEADS_PER_STEP * C_DIM // tn
            qvg = _wpipe_matmul("c_in_proj_qvg", hn, c_w_in, j, 3 * ct, [BF16],
                                functools.partial(_act_epilogue, act=_silu, act_from=2 * ct),
                                w_tile=lambda t: jnp.where(t < ct, t, t + ct),
                                o_tile=lambda t: (t % ct) // gt * (3 * gt) + t // ct * gt + t % gt)
            log2_f = _wpipe_matmul("c_in_proj_f", hn, c_w_in, j, ct, [F32],
                                      functools.partial(_forget_gate_epilogue, layer=l),
                                      w_tile=lambda t: t + ct, col_extra=[c_lb_logits])
            mixed = _hgrn2(qvg, log2_f, c_o_norm[j], batch, seq)
            x = _wpipe_matmul("c_out_proj", mixed, c_w_out, j, d // tn, [F32], _residual_epilogue,
                              tiled_extra=[x], vmem_mib=VMEM_RESIDUAL_MIB)
        x = _ffn(x, ffn2_norm, ffn2_w1, ffn2_w3, ffn2_w2, l)
    return x.reshape(batch, seq, d)
```

```python
import functools

import jax
import jax.numpy as jnp
from jax import lax
from jax.experimental import pallas as pl
from jax.experimental.pallas import tpu as pltpu

F32 = jnp.float32
BF16 = jnp.bfloat16

D_FF = 11008
FF_TILE = 256
D_FF_PAD = 11264
EPS = 1e-6

A_HEADS = 32
A_KV_HEADS = 4
A_GROUP = A_HEADS // A_KV_HEADS
A_HEAD_DIM = 64
WINDOW = 128
A_WIDTH = A_HEADS * A_HEAD_DIM
A_KV_WIDTH = A_KV_HEADS * A_HEAD_DIM
B_GROUPS = 16
B_GROUP_DIM = 128
B_WIDTH = B_GROUPS * B_GROUP_DIM
AB_IN = A_WIDTH + 2 * A_KV_WIDTH + 2 * B_WIDTH

C_HEADS = 32
C_DIM = 128
C_WIDTH = C_HEADS * C_DIM
C_CHUNK = 256
C_SUB = 128
C_HEADS_PER_STEP = 32

LANES = 128
SUBLANES = 8
MIB = 1 << 20
DOT_ROWS = 1024
VMEM_SMALL_MIB = 32
VMEM_MATMUL_MIB = 56
VMEM_PIPELINED_MIB = 60
VMEM_RESIDUAL_MIB = 62
LOG2E = 1.4426950408889634

NT_DIMS = (((1,), (1,)), ((), ()))
TN_DIMS = (((0,), (0,)), ((), ()))


def _params(semantics, vmem_mib):
    return pltpu.CompilerParams(dimension_semantics=semantics, vmem_limit_bytes=vmem_mib * MIB)


def _dot(a, b):
    return jnp.dot(a, b, preferred_element_type=F32)


def _dot_rows(a_ref, rows, w_ref):
    return _dot(a_ref[rows, :], w_ref[...])


def _gelu(y):
    return 0.5 * y * (1.0 + lax.erf(y * (2.0 ** -0.5)))


def _silu(y):
    half = 0.5 * y
    return half + half * jnp.tanh(half)


def _rmsnorm_kernel(x_ref, g_ref, o_ref):
    x = x_ref[...]
    r = lax.rsqrt(jnp.mean(x * x, axis=-1, keepdims=True) + EPS)
    o_ref[...] = (x * r * g_ref[...]).astype(o_ref.dtype)


def _rmsnorm(x, g, rows=512):
    m, d = x.shape
    return pl.pallas_call(
        _rmsnorm_kernel,
        grid=(m // rows,),
        in_specs=[pl.BlockSpec((rows, d), lambda i: (i, 0)),
                  pl.BlockSpec((1, d), lambda i: (0, 0))],
        out_specs=pl.BlockSpec((rows, d), lambda i: (i, 0)),
        out_shape=jax.ShapeDtypeStruct((m, d), BF16),
        compiler_params=_params(("parallel",), VMEM_SMALL_MIB),
        name="rmsnorm",
    )(x, g.reshape(1, d))


def _store_epilogue(j, chunks):
    for dot, _, out_refs in chunks:
        out_refs[0][...] = dot().astype(out_refs[0].dtype)


def _residual_epilogue(j, chunks):
    for dot, extra_refs, out_refs in chunks:
        out_refs[0][...] = extra_refs[0][...] + dot()


def _act_epilogue(j, chunks, *, act, act_from):
    @pl.when(j >= act_from)
    def _():
        for dot, _, out_refs in chunks:
            out_refs[0][...] = act(dot()).astype(out_refs[0].dtype)

    @pl.when(j < act_from)
    def _():
        _store_epilogue(j, chunks)


def _forget_gate_epilogue(j, chunks, *, layer):
    logits = chunks[0][1][0][...]
    lmax = jnp.max(logits, axis=0, keepdims=True)
    ex = jnp.exp(logits - lmax)
    prob = ex / jnp.sum(ex, axis=0, keepdims=True)
    lb = jnp.sum(prob[:layer + 1], axis=0, keepdims=True) - prob[0:1]
    c1 = 0.5 * (1.0 - lb)
    c0 = lb + c1
    for dot, _, out_refs in chunks:
        scaled = c1 * jnp.tanh(0.5 * dot())
        out_refs[0][...] = jnp.log2(c0 + scaled)
        out_refs[1][...] = (c1 - scaled).astype(out_refs[1].dtype)


def _wpipe_matmul_kernel(*refs, n_tiled, n_col, epilogue):
    a_ref, w_ref = refs[:2]
    tiled_refs = refs[2:2 + n_tiled]
    col_refs = refs[2 + n_tiled:2 + n_tiled + n_col]
    out_refs = refs[2 + n_tiled + n_col:-2]
    p = pl.program_id(0)
    slab = w_ref.shape[0]
    r0 = pl.multiple_of(pl.program_id(1) * slab, slab)

    def step(w_next_ref, w_cur_ref, compute):
        w_next_ref[pl.ds(r0, slab), :] = w_ref[...].astype(BF16)
        if compute:
            chunks = []
            for r in range(0, a_ref.shape[0], DOT_ROWS):
                rows = pl.ds(r, DOT_ROWS)
                chunks.append((functools.partial(_dot_rows, a_ref, rows, w_cur_ref),
                               [t.at[rows, :] for t in tiled_refs] + list(col_refs),
                               [o.at[rows, :] for o in out_refs]))
            epilogue(p - 1, chunks)

    even = (p % 2) == 0
    pl.when((p > 0) & even)(functools.partial(step, refs[-2], refs[-1], True))
    pl.when((p > 0) & jnp.logical_not(even))(functools.partial(step, refs[-1], refs[-2], True))
    pl.when(p == 0)(functools.partial(step, refs[-2], refs[-1], False))


def _wpipe_matmul(name, a, w, layer, n_tiles, out_dtypes, epilogue=_store_epilogue, w_tile=lambda j: j,
                  o_tile=lambda j: j, tiled_extra=(), col_extra=(), tm=2048, tn=512, vmem_mib=VMEM_PIPELINED_MIB):
    m, k = a.shape
    n_i = m // tm
    next_tile = lambda p: w_tile(jnp.minimum(p, n_tiles - 1))
    cur_tile = lambda p: o_tile(jnp.maximum(p - 1, 0))
    row_block = lambda p, i: jnp.where(p == 0, 0, i)
    out_block = pl.BlockSpec((tm, tn), lambda p, i: (row_block(p, i), cur_tile(p)))
    in_specs = [pl.BlockSpec((tm, k), lambda p, i: (row_block(p, i), 0)),
                pl.BlockSpec((None, k // n_i, tn), lambda p, i: (layer, i, next_tile(p)))]
    in_specs += [out_block for _ in tiled_extra]
    in_specs += [pl.BlockSpec((e.shape[0], tn), lambda p, i: (0, cur_tile(p))) for e in col_extra]
    wb = pltpu.VMEM((k, tn), BF16)
    outs = pl.pallas_call(
        functools.partial(_wpipe_matmul_kernel, n_tiled=len(tiled_extra), n_col=len(col_extra), epilogue=epilogue),
        grid=(n_tiles + 1, n_i),
        in_specs=in_specs,
        out_specs=[out_block for _ in out_dtypes],
        out_shape=[jax.ShapeDtypeStruct((m, n_tiles * tn), dt) for dt in out_dtypes],
        scratch_shapes=[wb, wb],
        compiler_params=_params(("arbitrary", "arbitrary"), vmem_mib),
        name=name,
    )(a, w, *tiled_extra, *col_extra)
    return outs if len(outs) > 1 else outs[0]


def _ffn_up_kernel(a_ref, w1_ref, w3_ref, w2_ref, h_ref, w2b_ref, w1e_ref, w3e_ref, w1o_ref, w3o_ref, *, n_tiles):
    p = pl.program_id(0)
    i = pl.program_id(1)
    t = p - 1
    slab = w1_ref.shape[0]
    r0 = pl.multiple_of(i * slab, slab)
    slab2 = w2_ref.shape[0]
    q0 = pl.multiple_of(i * slab2, slab2)

    def compute(w1_next_ref, w3_next_ref, w1_cur_ref, w3_cur_ref):
        w1_next_ref[pl.ds(r0, slab), :] = w1_ref[...].astype(BF16)
        w3_next_ref[pl.ds(r0, slab), :] = w3_ref[...].astype(BF16)
        for r in range(0, a_ref.shape[0], DOT_ROWS):
            rows = pl.ds(r, DOT_ROWS)
            a = a_ref[rows, :]
            y1 = _dot(a, w1_cur_ref[...])
            y3 = _dot(a, w3_cur_ref[...])
            h_ref[rows, :] = (_silu(y1) * (0.5 * y3)).astype(h_ref.dtype)
        w2b_ref[pl.ds(q0, slab2), :] = w2_ref[...].astype(BF16)

    real = (t >= 0) & (t < n_tiles)
    even = (p % 2) == 0
    pl.when(real & even)(functools.partial(compute, w1e_ref, w3e_ref, w1o_ref, w3o_ref))
    pl.when(real & jnp.logical_not(even))(functools.partial(compute, w1o_ref, w3o_ref, w1e_ref, w3e_ref))

    @pl.when(p == 0)
    def _():
        w1e_ref[pl.ds(r0, slab), :] = w1_ref[...].astype(BF16)
        w3e_ref[pl.ds(r0, slab), :] = w3_ref[...].astype(BF16)

    @pl.when(t == n_tiles)
    def _():
        h_ref[...] = jnp.zeros_like(h_ref)
        w2b_ref[pl.ds(q0, slab2), :] = jnp.zeros((slab2, w2b_ref.shape[1]), BF16)


def _ffn_up(a, w1, w3, w2, layer, tm=2048):
    m, k = a.shape
    tf = FF_TILE
    n_tiles = D_FF // tf
    n_i = m // tm
    n_pos = D_FF_PAD // tf + 1
    next_tile = lambda p: jnp.minimum(p, n_tiles - 1)
    cur_tile = lambda p: jnp.clip(p - 1, 0, n_tiles - 1)
    out_tile = lambda p: jnp.maximum(p - 1, 0)
    a_block = lambda p, i: jnp.where(p == 0, 0, jnp.where(p == n_pos - 1, n_i - 1, i))
    h_block = lambda p, i: jnp.where(p == 0, 0, i)
    wb = pltpu.VMEM((k, tf), BF16)
    return pl.pallas_call(
        functools.partial(_ffn_up_kernel, n_tiles=n_tiles),
        grid=(n_pos, n_i),
        in_specs=[pl.BlockSpec((tm, k), lambda p, i: (a_block(p, i), 0)),
                  pl.BlockSpec((None, k // n_i, tf), lambda p, i: (layer, i, next_tile(p))),
                  pl.BlockSpec((None, k // n_i, tf), lambda p, i: (layer, i, next_tile(p))),
                  pl.BlockSpec((None, tf // n_i, k), lambda p, i: (layer, cur_tile(p) * n_i + i, 0))],
        out_specs=[pl.BlockSpec((tm, tf), lambda p, i: (h_block(p, i), out_tile(p))),
                   pl.BlockSpec((tf, k), lambda p, i: (out_tile(p), 0))],
        out_shape=[jax.ShapeDtypeStruct((m, D_FF_PAD), BF16),
                   jax.ShapeDtypeStruct((D_FF_PAD, k), BF16)],
        scratch_shapes=[wb, wb, wb, wb],
        compiler_params=_params(("arbitrary", "arbitrary"), VMEM_PIPELINED_MIB),
        name="ffn_up",
    )(a, w1, w3, w2)


def _matmul_res_body(indices, a_ref, w_ref, r_ref, o_ref):
    @pl.when(indices[2] == 0)
    def _():
        o_ref[...] = r_ref[...] + _dot(a_ref[...], w_ref[...])

    @pl.when(indices[2] > 0)
    def _():
        o_ref[...] += _dot(a_ref[...], w_ref[...])


def _matmul_res(a, w, res, tk, tm=1024, tn=1024):
    m, k = a.shape
    n = w.shape[1]
    stream = pl.Buffered(3)
    pipeline = pltpu.emit_pipeline(
        _matmul_res_body,
        grid=(m // tm, n // tn, k // tk),
        in_specs=[pl.BlockSpec((tm, tk), lambda i, j, kk: (i, kk), pipeline_mode=stream),
                  pl.BlockSpec((tk, tn), lambda i, j, kk: (kk, j), pipeline_mode=stream),
                  pl.BlockSpec((tm, tn), lambda i, j, kk: (i, j))],
        out_specs=[pl.BlockSpec((tm, tn), lambda i, j, kk: (i, j))],
        _explicit_indices=True,
    )

    def outer(a_hbm, w_hbm, r_hbm, o_hbm):
        pipeline(a_hbm, w_hbm, r_hbm, o_hbm)

    return pl.pallas_call(
        outer,
        in_specs=[pl.BlockSpec(memory_space=pl.ANY)] * 3,
        out_specs=pl.BlockSpec(memory_space=pl.ANY),
        out_shape=jax.ShapeDtypeStruct((m, n), F32),
        compiler_params=pltpu.CompilerParams(vmem_limit_bytes=VMEM_MATMUL_MIB * MIB),
        name="matmul_res",
    )(a, w, res)


def _ab_mixer_kernel(sinks_ref, qug_ref, kvp_ref, kvc_ref, qg_ref, kg_ref,
                     lng_ref, lnb_ref, ws_ref, bs_ref, o_ref):
    n = pl.program_id(1)
    w = WINDOW
    hd = A_HEAD_DIM

    key = lax.broadcasted_iota(jnp.int32, (2 * w, w), 0)
    qry = lax.broadcasted_iota(jnp.int32, (2 * w, w), 1)
    dist = qry + w - key
    visible = (dist >= 0) & (dist < w) & ((key >= w) | (n > 0))
    low = lax.broadcasted_iota(jnp.int32, (1, LANES), 1) < hd
    sub8 = lax.broadcasted_iota(jnp.int32, (SUBLANES, LANES), 0)
    ones_low = jnp.where(low, 1.0, 0.0)
    ones_high = 1.0 - ones_low
    sel2 = jnp.where(sub8 < SUBLANES // 2, ones_low, ones_high).astype(BF16)
    sink_row = sub8 == 0

    def split_bf16(x):
        hi = x.astype(BF16)
        return hi, (x - hi.astype(F32)).astype(BF16)

    kk = jnp.concatenate([kvp_ref[:, :A_KV_WIDTH], kvc_ref[:, :A_KV_WIDTH]], axis=0)
    vv = jnp.concatenate([kvp_ref[:, A_KV_WIDTH:], kvc_ref[:, A_KV_WIDTH:]], axis=0)
    seg_r = lax.broadcasted_iota(jnp.int32, (A_KV_WIDTH, A_KV_WIDTH), 0) // hd
    seg_c = lax.broadcasted_iota(jnp.int32, (A_KV_WIDTH, A_KV_WIDTH), 1) // hd
    seg = (seg_r == seg_c).astype(BF16)
    hi, lo = split_bf16(kk * kk)
    kss = _dot(hi, seg) + _dot(lo, seg)
    kgain = jnp.concatenate([kg_ref[...]] * A_KV_HEADS, axis=1)
    khat = kk * lax.rsqrt(kss * (1.0 / hd) + EPS) * kgain
    qgain = jnp.concatenate([qg_ref[...]] * 2, axis=1) * (hd ** -0.5 * LOG2E)

    def head_pair_operands(x, h):
        tile = x[:, (h // 2) * LANES:(h // 2 + 1) * LANES]
        swapped = pltpu.roll(tile, hd, axis=1)
        in_low, in_high = (tile, swapped) if h % 2 == 0 else (swapped, tile)
        return jnp.where(low, in_low, 0.0), jnp.where(low, 0.0, in_high)

    for h in range(A_KV_HEADS):
        k_low, k_high = head_pair_operands(khat, h)
        v_low, v_high = head_pair_operands(vv, h)
        k_pad = (k_low.astype(BF16), k_high.astype(BF16))
        v_aug = tuple(
            jnp.concatenate([
                jnp.concatenate([v_p, jnp.broadcast_to(ones_p, v_p.shape)], axis=1),
                jnp.concatenate([jnp.zeros((SUBLANES, LANES), F32), jnp.broadcast_to(ones_p, (SUBLANES, LANES))], axis=1),
            ], axis=0).astype(BF16)
            for v_p, ones_p in ((v_low, ones_low), (v_high, ones_high)))
        for c in range(h * A_GROUP // 2, (h + 1) * A_GROUP // 2):
            qc = qug_ref[:, c * LANES:(c + 1) * LANES]
            hi, lo = split_bf16(qc * qc)
            qss = (lax.dot_general(sel2, hi, NT_DIMS, preferred_element_type=F32)
                   + lax.dot_general(sel2, lo, NT_DIMS, preferred_element_type=F32))
            qscale = lax.rsqrt(qss * (1.0 / hd) + EPS)
            qg = (qc * qgain).astype(BF16)
            acc = jnp.zeros((w, 2 * LANES), F32)
            for parity in range(2):
                st = lax.dot_general(k_pad[parity], qg, NT_DIMS, preferred_element_type=F32)
                r = qscale[parity * (SUBLANES // 2):parity * (SUBLANES // 2) + 1]
                st = jnp.where(visible, st * r, -jnp.inf)
                s_sink = sinks_ref[2 * c + parity] * LOG2E
                mx = jnp.maximum(jnp.max(st, axis=0, keepdims=True), s_sink)
                e = jnp.exp2(st - mx)
                e_sink = jnp.where(sink_row, jnp.exp2(s_sink - mx), 0.0)
                e_aug = jnp.concatenate([e, e_sink], axis=0).astype(BF16)
                acc = acc + lax.dot_general(e_aug, v_aug[parity], TN_DIMS, preferred_element_type=F32)
            o_ref[:, c * LANES:(c + 1) * LANES] = (acc[:, :LANES] / acc[:, LANES:]).astype(o_ref.dtype)

    row = lax.broadcasted_iota(jnp.int32, (w, w), 0)
    col = lax.broadcasted_iota(jnp.int32, (w, w), 1)
    causal = row >= col
    u0 = A_WIDTH
    v0 = A_WIDTH + B_WIDTH
    for g in range(B_GROUPS):
        sl = slice(g * B_GROUP_DIM, (g + 1) * B_GROUP_DIM)
        x = qug_ref[:, v0 + g * B_GROUP_DIM:v0 + (g + 1) * B_GROUP_DIM]
        mu = jnp.mean(x, axis=-1, keepdims=True)
        d = x - mu
        var = jnp.mean(d * d, axis=-1, keepdims=True)
        vn = d * lax.rsqrt(var + EPS) * lng_ref[:, sl] + lnb_ref[:, sl]
        wg = jnp.where(causal, ws_ref[g], 0.0).astype(BF16)
        sg = _dot(wg, vn.astype(BF16)) + bs_ref[:, g:g + 1]
        o_ref[:, u0 + g * B_GROUP_DIM:u0 + (g + 1) * B_GROUP_DIM] = (
            qug_ref[:, u0 + g * B_GROUP_DIM:u0 + (g + 1) * B_GROUP_DIM] * sg).astype(o_ref.dtype)


def _ab_mixer(proj, batch, seq, q_gain, k_gain, sinks, ln_g, ln_b, w_s, b_s):
    nb = seq // WINDOW
    kv_blk = (A_WIDTH + 2 * B_WIDTH) // (2 * A_KV_WIDTH)
    row_blk = lambda b, n: b * nb + n
    return pl.pallas_call(
        _ab_mixer_kernel,
        grid=(batch, nb),
        in_specs=[
            pl.BlockSpec(memory_space=pltpu.SMEM),
            pl.BlockSpec((WINDOW, A_WIDTH + 2 * B_WIDTH), lambda b, n: (row_blk(b, n), 0)),
            pl.BlockSpec((WINDOW, 2 * A_KV_WIDTH), lambda b, n: (b * nb + jnp.maximum(n - 1, 0), kv_blk)),
            pl.BlockSpec((WINDOW, 2 * A_KV_WIDTH), lambda b, n: (row_blk(b, n), kv_blk)),
            pl.BlockSpec((1, A_HEAD_DIM), lambda b, n: (0, 0)),
            pl.BlockSpec((1, A_HEAD_DIM), lambda b, n: (0, 0)),
            pl.BlockSpec((1, B_WIDTH), lambda b, n: (0, 0)),
            pl.BlockSpec((1, B_WIDTH), lambda b, n: (0, 0)),
            pl.BlockSpec((B_GROUPS, WINDOW, WINDOW), lambda b, n: (0, 0, 0)),
            pl.BlockSpec((WINDOW, B_GROUPS), lambda b, n: (0, 0)),
        ],
        out_specs=pl.BlockSpec((WINDOW, A_WIDTH + B_WIDTH), lambda b, n: (row_blk(b, n), 0)),
        out_shape=jax.ShapeDtypeStruct((batch * seq, A_WIDTH + B_WIDTH), BF16),
        compiler_params=_params(("parallel", "arbitrary"), VMEM_SMALL_MIB),
        name="ab_mixer",
    )(sinks, proj, proj, proj,
      q_gain.reshape(1, A_HEAD_DIM), k_gain.reshape(1, A_HEAD_DIM),
      ln_g.reshape(1, B_WIDTH), ln_b.reshape(1, B_WIDTH), w_s, b_s.T)


def _hgrn2_kernel(og_ref, qvg_ref, lf_ref, k_ref, o_ref, st_ref):
    c = C_SUB
    hw = C_HEADS_PER_STEP * C_DIM

    @pl.when(pl.program_id(2) == 0)
    def _():
        st_ref[...] = jnp.zeros_like(st_ref)

    row = lax.broadcasted_iota(jnp.int32, (c, c), 0)
    col = lax.broadcasted_iota(jnp.int32, (c, c), 1)
    row1 = lax.broadcasted_iota(jnp.int32, (c, 1), 0)
    cumsum_mat = (row >= col).astype(BF16)
    diff_bits = row ^ col
    level = jnp.zeros((c, c), jnp.int32)
    for b in range(1, c.bit_length() - 1):
        level = level + (diff_bits >= (1 << b)).astype(jnp.int32)
    level = jnp.where(col < row, level, jnp.where(col == row, -1, -2))
    og = og_ref[...]

    for h in range(C_HEADS_PER_STEP):
        sl = slice(h * C_DIM, (h + 1) * C_DIM)
        st = st_ref[h]
        for sub in range(C_CHUNK // c):
            rows = slice(sub * c, (sub + 1) * c)
            log2_f = lf_ref[rows, sl]
            k = k_ref[rows, sl].astype(F32)

            p1 = log2_f.astype(BF16)
            p2 = (log2_f - p1.astype(F32)).astype(BF16)
            cg = _dot(cumsum_mat, p1) + _dot(cumsum_mat, p2)

            q = qvg_ref[rows, sl].astype(F32)
            v = qvg_ref[rows, hw + h * C_DIM:hw + (h + 1) * C_DIM]
            o = lax.dot_general((q * jnp.exp2(cg)).astype(BF16), st.astype(BF16), NT_DIMS,
                                preferred_element_type=F32)

            att = jnp.where(level == -1, jnp.sum(q * k, axis=-1, keepdims=True), 0.0)
            part = lax.dot_general((q * jnp.exp2(log2_f)).astype(BF16), k_ref[rows, sl], NT_DIMS,
                                   preferred_element_type=F32)
            att = jnp.where(level == 0, part, att)
            cg3 = cg.reshape(c // SUBLANES, SUBLANES, C_DIM)
            sub_row = lambda r: jnp.broadcast_to(cg3[:, r:r + 1, :], cg3.shape).reshape(c, C_DIM)
            n = 2
            while n < c:
                if 4 * n == SUBLANES:
                    ref = jnp.where((row1 & (2 * n)) != 0, sub_row(3 * n - 1), sub_row(n - 1))
                elif 2 * n == SUBLANES:
                    ref = sub_row(n - 1)
                else:
                    ref = jnp.concatenate([jnp.broadcast_to(cg[lo + n - 1:lo + n, :], (2 * n, C_DIM))
                                           for lo in range(0, c, 2 * n)], axis=0)
                e = jnp.exp2(-jnp.abs(cg - ref))
                part = lax.dot_general((q * e).astype(BF16), (k * e).astype(BF16), NT_DIMS,
                                       preferred_element_type=F32)
                att = jnp.where(level == n.bit_length() - 1, part, att)
                n *= 2

            o = o + _dot(att.astype(BF16), v)

            last = cg[c - 1:c, :]
            kd = (k * jnp.exp2(last - cg)).astype(BF16)
            st = st * jnp.exp2(last) + lax.dot_general(v, kd, TN_DIMS, preferred_element_type=F32)

            y = o * lax.rsqrt(jnp.mean(o * o, axis=-1, keepdims=True) + EPS) * og
            gate = qvg_ref[rows, 2 * hw + h * C_DIM:2 * hw + (h + 1) * C_DIM]
            o_ref[rows, sl] = (y * gate.astype(F32)).astype(o_ref.dtype)
        st_ref[h] = st


def _hgrn2(qvg, log2_f, k, o_gain, batch, seq):
    nc = seq // C_CHUNK
    hw = C_HEADS_PER_STEP * C_DIM
    nh = C_WIDTH // hw
    blk = lambda width: pl.BlockSpec((C_CHUNK, width), lambda b, g, c: (b * nc + c, g))
    return pl.pallas_call(
        _hgrn2_kernel,
        grid=(batch, nh, nc),
        in_specs=[pl.BlockSpec((1, C_DIM), lambda b, g, c: (0, 0)),
                  blk(3 * hw), blk(hw), blk(hw)],
        out_specs=blk(hw),
        out_shape=jax.ShapeDtypeStruct((batch * seq, C_WIDTH), BF16),
        scratch_shapes=[pltpu.VMEM((C_HEADS_PER_STEP, C_DIM, C_DIM), F32)],
        compiler_params=_params(("parallel", "parallel", "arbitrary"), VMEM_MATMUL_MIB),
        name="hgrn2",
    )(o_gain.reshape(1, C_DIM), qvg, log2_f, k)


def _ffn(x, norm_g, w1, w3, w2, layer):
    h, w2b = _ffn_up(_rmsnorm(x, norm_g[layer]), w1, w3, w2, layer)
    return _matmul_res(h, w2b, x, tk=D_FF_PAD // 4)


def kernel(x, ffn1_norm, ffn1_w1, ffn1_w3, ffn1_w2, mix_norm, ffn2_norm, ffn2_w1, ffn2_w3, ffn2_w2, ab_w_in, ab_q_norm, ab_k_norm, ab_sinks, ab_v_ln_g, ab_v_ln_b, ab_w_s, ab_b_s, ab_w_out, c_w_in, c_lb_logits, c_o_norm, c_w_out):
    batch, seq, d = x.shape
    depth = ffn1_norm.shape[0]
    tn = 512
    x = x.reshape(batch * seq, d)
    for l in range(depth):
        x = _ffn(x, ffn1_norm, ffn1_w1, ffn1_w3, ffn1_w2, l)
        hn = _rmsnorm(x, mix_norm[l])
        j = l // 2
        if l % 2 == 0:
            kv_tile = A_WIDTH // tn
            n_tiles = AB_IN // tn
            o_tile = lambda t: jnp.where(t < kv_tile, t, jnp.where(t == kv_tile, n_tiles - 1, t - 1))
            proj = _wpipe_matmul("ab_in_proj", hn, ab_w_in, j, n_tiles, [F32], o_tile=o_tile,
                                 epilogue=functools.partial(_act_epilogue, act=_gelu, act_from=kv_tile + 1))
            mixed = _ab_mixer(proj, batch, seq, ab_q_norm[j], ab_k_norm[j], ab_sinks[j],
                              ab_v_ln_g[j], ab_v_ln_b[j], ab_w_s[j], ab_b_s[j])
            x = _wpipe_matmul("ab_out_proj", mixed, ab_w_out, j, d // tn, [F32], _residual_epilogue,
                              tiled_extra=[x], vmem_mib=VMEM_RESIDUAL_MIB)
        else:
            ct = C_WIDTH // tn
            gt = C_HEADS_PER_STEP * C_DIM // tn
            qvg = _wpipe_matmul("c_in_proj_qvg", hn, c_w_in, j, 3 * ct, [BF16],
                                functools.partial(_act_epilogue, act=_silu, act_from=2 * ct),
                                w_tile=lambda t: jnp.where(t < ct, t, t + ct),
                                o_tile=lambda t: (t % ct) // gt * (3 * gt) + t // ct * gt + t % gt)
            log2_f, k = _wpipe_matmul("c_in_proj_f", hn, c_w_in, j, ct, [F32, BF16],
                                      functools.partial(_forget_gate_epilogue, layer=l),
                                      w_tile=lambda t: t + ct, col_extra=[c_lb_logits])
            mixed = _hgrn2(qvg, log2_f, k, c_o_norm[j], batch, seq)
            x = _wpipe_matmul("c_out_proj", mixed, c_w_out, j, d // tn, [F32], _residual_epilogue,
                              tiled_extra=[x], vmem_mib=VMEM_RESIDUAL_MIB)
        x = _ffn(x, ffn2_norm, ffn2_w1, ffn2_w3, ffn2_w2, l)
    return x.reshape(batch, seq, d)
```

```python
import functools

import jax
import jax.numpy as jnp
from jax import lax
from jax.experimental import pallas as pl
from jax.experimental.pallas import tpu as pltpu

F32 = jnp.float32
BF16 = jnp.bfloat16

D_FF = 11008
FF_TILE = 256
D_FF_PAD = 11264
EPS = 1e-6

A_HEADS = 32
A_KV_HEADS = 4
A_GROUP = A_HEADS // A_KV_HEADS
A_HEAD_DIM = 64
WINDOW = 128
AB_BLOCKS_PER_STEP = 2
A_WIDTH = A_HEADS * A_HEAD_DIM
A_KV_WIDTH = A_KV_HEADS * A_HEAD_DIM
B_GROUPS = 16
B_GROUP_DIM = 128
B_WIDTH = B_GROUPS * B_GROUP_DIM
AB_IN = A_WIDTH + 2 * A_KV_WIDTH + 2 * B_WIDTH

C_HEADS = 32
C_DIM = 128
C_WIDTH = C_HEADS * C_DIM
C_CHUNK = 256
C_SUB = 128
C_HEADS_PER_STEP = 32

LANES = 128
SUBLANES = 8
MIB = 1 << 20
DOT_ROWS = 1024
VMEM_SMALL_MIB = 32
VMEM_MATMUL_MIB = 56
VMEM_PIPELINED_MIB = 60
VMEM_RESIDUAL_MIB = 62
LOG2E = 1.4426950408889634

NT_DIMS = (((1,), (1,)), ((), ()))
TN_DIMS = (((0,), (0,)), ((), ()))


def _params(semantics, vmem_mib):
    return pltpu.CompilerParams(dimension_semantics=semantics, vmem_limit_bytes=vmem_mib * MIB)


def _dot(a, b):
    return jnp.dot(a, b, preferred_element_type=F32)


def _dot_rows(a_ref, rows, w_ref):
    return _dot(a_ref[rows, :], w_ref[...])


def _gelu(y):
    return 0.5 * y * (1.0 + lax.erf(y * (2.0 ** -0.5)))


def _silu(y):
    half = 0.5 * y
    return half + half * jnp.tanh(half)


def _rmsnorm_kernel(x_ref, g_ref, o_ref):
    x = x_ref[...]
    r = lax.rsqrt(jnp.mean(x * x, axis=-1, keepdims=True) + EPS)
    o_ref[...] = (x * r * g_ref[...]).astype(o_ref.dtype)


def _rmsnorm(x, g, rows=512):
    m, d = x.shape
    return pl.pallas_call(
        _rmsnorm_kernel,
        grid=(m // rows,),
        in_specs=[pl.BlockSpec((rows, d), lambda i: (i, 0)),
                  pl.BlockSpec((1, d), lambda i: (0, 0))],
        out_specs=pl.BlockSpec((rows, d), lambda i: (i, 0)),
        out_shape=jax.ShapeDtypeStruct((m, d), BF16),
        compiler_params=_params(("parallel",), VMEM_SMALL_MIB),
        name="rmsnorm",
    )(x, g.reshape(1, d))


def _store_epilogue(j, chunks):
    for dot, _, out_refs in chunks:
        out_refs[0][...] = dot().astype(out_refs[0].dtype)


def _residual_epilogue(j, chunks):
    for dot, extra_refs, out_refs in chunks:
        out_refs[0][...] = extra_refs[0][...] + dot()


def _act_epilogue(j, chunks, *, act, act_from):
    @pl.when(j >= act_from)
    def _():
        for dot, _, out_refs in chunks:
            out_refs[0][...] = act(dot()).astype(out_refs[0].dtype)

    @pl.when(j < act_from)
    def _():
        _store_epilogue(j, chunks)


def _forget_gate_epilogue(j, chunks, *, layer):
    logits = chunks[0][1][0][...]
    lmax = jnp.max(logits, axis=0, keepdims=True)
    ex = jnp.exp(logits - lmax)
    prob = ex / jnp.sum(ex, axis=0, keepdims=True)
    lb = jnp.sum(prob[:layer + 1], axis=0, keepdims=True) - prob[0:1]
    c1 = 0.5 * (1.0 - lb)
    c0 = lb + c1
    for dot, _, out_refs in chunks:
        scaled = c1 * jnp.tanh(0.5 * dot())
        out_refs[0][...] = jnp.log2(c0 + scaled)
        out_refs[1][...] = (c1 - scaled).astype(out_refs[1].dtype)


def _wpipe_matmul_kernel(*refs, n_tiled, n_col, epilogue):
    a_ref, w_ref = refs[:2]
    tiled_refs = refs[2:2 + n_tiled]
    col_refs = refs[2 + n_tiled:2 + n_tiled + n_col]
    out_refs = refs[2 + n_tiled + n_col:-2]
    p = pl.program_id(0)
    slab = w_ref.shape[0]
    r0 = pl.multiple_of(pl.program_id(1) * slab, slab)

    def step(w_next_ref, w_cur_ref, compute):
        w_next_ref[pl.ds(r0, slab), :] = w_ref[...].astype(BF16)
        if compute:
            chunks = []
            for r in range(0, a_ref.shape[0], DOT_ROWS):
                rows = pl.ds(r, DOT_ROWS)
                chunks.append((functools.partial(_dot_rows, a_ref, rows, w_cur_ref),
                               [t.at[rows, :] for t in tiled_refs] + list(col_refs),
                               [o.at[rows, :] for o in out_refs]))
            epilogue(p - 1, chunks)

    even = (p % 2) == 0
    pl.when((p > 0) & even)(functools.partial(step, refs[-2], refs[-1], True))
    pl.when((p > 0) & jnp.logical_not(even))(functools.partial(step, refs[-1], refs[-2], True))
    pl.when(p == 0)(functools.partial(step, refs[-2], refs[-1], False))


def _wpipe_matmul(name, a, w, layer, n_tiles, out_dtypes, epilogue=_store_epilogue, w_tile=lambda j: j,
                  o_tile=lambda j: j, tiled_extra=(), col_extra=(), tm=2048, tn=512, vmem_mib=VMEM_PIPELINED_MIB):
    m, k = a.shape
    n_i = m // tm
    next_tile = lambda p: w_tile(jnp.minimum(p, n_tiles - 1))
    cur_tile = lambda p: o_tile(jnp.maximum(p - 1, 0))
    row_block = lambda p, i: jnp.where(p == 0, 0, i)
    out_block = pl.BlockSpec((tm, tn), lambda p, i: (row_block(p, i), cur_tile(p)))
    in_specs = [pl.BlockSpec((tm, k), lambda p, i: (row_block(p, i), 0)),
                pl.BlockSpec((None, k // n_i, tn), lambda p, i: (layer, i, next_tile(p)))]
    in_specs += [out_block for _ in tiled_extra]
    in_specs += [pl.BlockSpec((e.shape[0], tn), lambda p, i: (0, cur_tile(p))) for e in col_extra]
    wb = pltpu.VMEM((k, tn), BF16)
    outs = pl.pallas_call(
        functools.partial(_wpipe_matmul_kernel, n_tiled=len(tiled_extra), n_col=len(col_extra), epilogue=epilogue),
        grid=(n_tiles + 1, n_i),
        in_specs=in_specs,
        out_specs=[out_block for _ in out_dtypes],
        out_shape=[jax.ShapeDtypeStruct((m, n_tiles * tn), dt) for dt in out_dtypes],
        scratch_shapes=[wb, wb],
        compiler_params=_params(("arbitrary", "arbitrary"), vmem_mib),
        name=name,
    )(a, w, *tiled_extra, *col_extra)
    return outs if len(outs) > 1 else outs[0]


def _ffn_up_kernel(a_ref, w1_ref, w3_ref, w2_ref, h_ref, w2b_ref, w1e_ref, w3e_ref, w1o_ref, w3o_ref, *, n_tiles):
    p = pl.program_id(0)
    i = pl.program_id(1)
    t = p - 1
    slab = w1_ref.shape[0]
    r0 = pl.multiple_of(i * slab, slab)
    slab2 = w2_ref.shape[0]
    q0 = pl.multiple_of(i * slab2, slab2)

    def compute(w1_next_ref, w3_next_ref, w1_cur_ref, w3_cur_ref):
        w1_next_ref[pl.ds(r0, slab), :] = w1_ref[...].astype(BF16)
        w3_next_ref[pl.ds(r0, slab), :] = w3_ref[...].astype(BF16)
        for r in range(0, a_ref.shape[0], DOT_ROWS):
            rows = pl.ds(r, DOT_ROWS)
            a = a_ref[rows, :]
            y1 = _dot(a, w1_cur_ref[...])
            y3 = _dot(a, w3_cur_ref[...])
            h_ref[rows, :] = (_silu(y1) * (0.5 * y3)).astype(h_ref.dtype)
        w2b_ref[pl.ds(q0, slab2), :] = w2_ref[...].astype(BF16)

    real = (t >= 0) & (t < n_tiles)
    even = (p % 2) == 0
    pl.when(real & even)(functools.partial(compute, w1e_ref, w3e_ref, w1o_ref, w3o_ref))
    pl.when(real & jnp.logical_not(even))(functools.partial(compute, w1o_ref, w3o_ref, w1e_ref, w3e_ref))

    @pl.when(p == 0)
    def _():
        w1e_ref[pl.ds(r0, slab), :] = w1_ref[...].astype(BF16)
        w3e_ref[pl.ds(r0, slab), :] = w3_ref[...].astype(BF16)

    @pl.when(t == n_tiles)
    def _():
        h_ref[...] = jnp.zeros_like(h_ref)
        w2b_ref[pl.ds(q0, slab2), :] = jnp.zeros((slab2, w2b_ref.shape[1]), BF16)


def _ffn_up(a, w1, w3, w2, layer, tm=2048):
    m, k = a.shape
    tf = FF_TILE
    n_tiles = D_FF // tf
    n_i = m // tm
    n_pos = D_FF_PAD // tf + 1
    next_tile = lambda p: jnp.minimum(p, n_tiles - 1)
    cur_tile = lambda p: jnp.clip(p - 1, 0, n_tiles - 1)
    out_tile = lambda p: jnp.maximum(p - 1, 0)
    a_block = lambda p, i: jnp.where(p == 0, 0, jnp.where(p == n_pos - 1, n_i - 1, i))
    h_block = lambda p, i: jnp.where(p == 0, 0, i)
    wb = pltpu.VMEM((k, tf), BF16)
    return pl.pallas_call(
        functools.partial(_ffn_up_kernel, n_tiles=n_tiles),
        grid=(n_pos, n_i),
        in_specs=[pl.BlockSpec((tm, k), lambda p, i: (a_block(p, i), 0)),
                  pl.BlockSpec((None, k // n_i, tf), lambda p, i: (layer, i, next_tile(p))),
                  pl.BlockSpec((None, k // n_i, tf), lambda p, i: (layer, i, next_tile(p))),
                  pl.BlockSpec((None, tf // n_i, k), lambda p, i: (layer, cur_tile(p) * n_i + i, 0))],
        out_specs=[pl.BlockSpec((tm, tf), lambda p, i: (h_block(p, i), out_tile(p))),
                   pl.BlockSpec((tf, k), lambda p, i: (out_tile(p), 0))],
        out_shape=[jax.ShapeDtypeStruct((m, D_FF_PAD), BF16),
                   jax.ShapeDtypeStruct((D_FF_PAD, k), BF16)],
        scratch_shapes=[wb, wb, wb, wb],
        compiler_params=_params(("arbitrary", "arbitrary"), VMEM_PIPELINED_MIB),
        name="ffn_up",
    )(a, w1, w3, w2)


def _matmul_res_kernel(a_ref, w_ref, r_ref, o_ref):
    @pl.when(pl.program_id(2) == 0)
    def _():
        o_ref[...] = r_ref[...] + _dot(a_ref[...], w_ref[...])

    @pl.when(pl.program_id(2) > 0)
    def _():
        o_ref[...] += _dot(a_ref[...], w_ref[...])


def _matmul_res(a, w, res, tk, tm=1024, tn=1024):
    m, k = a.shape
    n = w.shape[1]
    return pl.pallas_call(
        _matmul_res_kernel,
        grid=(m // tm, n // tn, k // tk),
        in_specs=[pl.BlockSpec((tm, tk), lambda i, j, kk: (i, kk)),
                  pl.BlockSpec((tk, tn), lambda i, j, kk: (kk, j)),
                  pl.BlockSpec((tm, tn), lambda i, j, kk: (i, j))],
        out_specs=pl.BlockSpec((tm, tn), lambda i, j, kk: (i, j)),
        out_shape=jax.ShapeDtypeStruct((m, n), F32),
        compiler_params=_params(("parallel", "parallel", "arbitrary"), VMEM_MATMUL_MIB),
        name="matmul_res",
    )(a, w, res)


def _ab_mixer_kernel(sinks_ref, qug_ref, kvp_ref, kvc_ref, qg_ref, kg_ref,
                     lng_ref, lnb_ref, ws_ref, bs_ref, o_ref):
    n = pl.program_id(1)
    w = WINDOW
    hd = A_HEAD_DIM
    for sub in range(o_ref.shape[0] // w):
        rows = slice(sub * w, (sub + 1) * w)
        kv_cur = kvc_ref[rows, :]
        kv_prev = kvp_ref[...] if sub == 0 else kvc_ref[(sub - 1) * w:sub * w, :]

        key = lax.broadcasted_iota(jnp.int32, (2 * w, w), 0)
        qry = lax.broadcasted_iota(jnp.int32, (2 * w, w), 1)
        dist = qry + w - key
        visible = (dist >= 0) & (dist < w) & ((key >= w) | (n > 0) | (sub > 0))
        low = lax.broadcasted_iota(jnp.int32, (1, LANES), 1) < hd
        sub8 = lax.broadcasted_iota(jnp.int32, (SUBLANES, LANES), 0)
        ones_low = jnp.where(low, 1.0, 0.0)
        ones_high = 1.0 - ones_low
        sel2 = jnp.where(sub8 < SUBLANES // 2, ones_low, ones_high).astype(BF16)
        sink_row = sub8 == 0

        def split_bf16(x):
            hi = x.astype(BF16)
            return hi, (x - hi.astype(F32)).astype(BF16)

        kk = jnp.concatenate([kv_prev[:, :A_KV_WIDTH], kv_cur[:, :A_KV_WIDTH]], axis=0)
        vv = jnp.concatenate([kv_prev[:, A_KV_WIDTH:], kv_cur[:, A_KV_WIDTH:]], axis=0)
        seg_r = lax.broadcasted_iota(jnp.int32, (A_KV_WIDTH, A_KV_WIDTH), 0) // hd
        seg_c = lax.broadcasted_iota(jnp.int32, (A_KV_WIDTH, A_KV_WIDTH), 1) // hd
        seg = (seg_r == seg_c).astype(BF16)
        hi, lo = split_bf16(kk * kk)
        kss = _dot(hi, seg) + _dot(lo, seg)
        kgain = jnp.concatenate([kg_ref[...]] * A_KV_HEADS, axis=1)
        khat = kk * lax.rsqrt(kss * (1.0 / hd) + EPS) * kgain
        qgain = jnp.concatenate([qg_ref[...]] * 2, axis=1) * (hd ** -0.5 * LOG2E)

        def head_pair_operands(x, h):
            tile = x[:, (h // 2) * LANES:(h // 2 + 1) * LANES]
            swapped = pltpu.roll(tile, hd, axis=1)
            in_low, in_high = (tile, swapped) if h % 2 == 0 else (swapped, tile)
            return jnp.where(low, in_low, 0.0), jnp.where(low, 0.0, in_high)

        for h in range(A_KV_HEADS):
            k_low, k_high = head_pair_operands(khat, h)
            v_low, v_high = head_pair_operands(vv, h)
            k_pad = (k_low.astype(BF16), k_high.astype(BF16))
            v_aug = tuple(
                jnp.concatenate([
                    jnp.concatenate([v_p, jnp.broadcast_to(ones_p, v_p.shape)], axis=1),
                    jnp.concatenate([jnp.zeros((SUBLANES, LANES), F32), jnp.broadcast_to(ones_p, (SUBLANES, LANES))], axis=1),
                ], axis=0).astype(BF16)
                for v_p, ones_p in ((v_low, ones_low), (v_high, ones_high)))
            for c in range(h * A_GROUP // 2, (h + 1) * A_GROUP // 2):
                qc = qug_ref[rows, c * LANES:(c + 1) * LANES]
                hi, lo = split_bf16(qc * qc)
                qss = (lax.dot_general(sel2, hi, NT_DIMS, preferred_element_type=F32)
                       + lax.dot_general(sel2, lo, NT_DIMS, preferred_element_type=F32))
                qscale = lax.rsqrt(qss * (1.0 / hd) + EPS)
                qg = (qc * qgain).astype(BF16)
                acc = jnp.zeros((w, 2 * LANES), F32)
                for parity in range(2):
                    st = lax.dot_general(k_pad[parity], qg, NT_DIMS, preferred_element_type=F32)
                    r = qscale[parity * (SUBLANES // 2):parity * (SUBLANES // 2) + 1]
                    st = jnp.where(visible, st * r, -jnp.inf)
                    s_sink = sinks_ref[2 * c + parity] * LOG2E
                    mx = jnp.maximum(jnp.max(st, axis=0, keepdims=True), s_sink)
                    e = jnp.exp2(st - mx)
                    e_sink = jnp.where(sink_row, jnp.exp2(s_sink - mx), 0.0)
                    e_aug = jnp.concatenate([e, e_sink], axis=0).astype(BF16)
                    acc = acc + lax.dot_general(e_aug, v_aug[parity], TN_DIMS, preferred_element_type=F32)
                o_ref[rows, c * LANES:(c + 1) * LANES] = (acc[:, :LANES] / acc[:, LANES:]).astype(o_ref.dtype)

        row = lax.broadcasted_iota(jnp.int32, (w, w), 0)
        col = lax.broadcasted_iota(jnp.int32, (w, w), 1)
        causal = row >= col
        u0 = A_WIDTH
        v0 = A_WIDTH + B_WIDTH
        for g in range(B_GROUPS):
            sl = slice(g * B_GROUP_DIM, (g + 1) * B_GROUP_DIM)
            x = qug_ref[rows, v0 + g * B_GROUP_DIM:v0 + (g + 1) * B_GROUP_DIM]
            mu = jnp.mean(x, axis=-1, keepdims=True)
            d = x - mu
            var = jnp.mean(d * d, axis=-1, keepdims=True)
            vn = d * lax.rsqrt(var + EPS) * lng_ref[:, sl] + lnb_ref[:, sl]
            wg = jnp.where(causal, ws_ref[g], 0.0).astype(BF16)
            sg = _dot(wg, vn.astype(BF16)) + bs_ref[:, g:g + 1]
            o_ref[rows, u0 + g * B_GROUP_DIM:u0 + (g + 1) * B_GROUP_DIM] = (
                qug_ref[rows, u0 + g * B_GROUP_DIM:u0 + (g + 1) * B_GROUP_DIM] * sg).astype(o_ref.dtype)


def _ab_mixer(proj, batch, seq, q_gain, k_gain, sinks, ln_g, ln_b, w_s, b_s):
    rows = AB_BLOCKS_PER_STEP * WINDOW
    nb = seq // rows
    kv_blk = (A_WIDTH + 2 * B_WIDTH) // (2 * A_KV_WIDTH)
    row_blk = lambda b, n: b * nb + n
    prev_blk = lambda b, n: jnp.maximum((b * nb + n) * AB_BLOCKS_PER_STEP - 1, b * nb * AB_BLOCKS_PER_STEP)
    return pl.pallas_call(
        _ab_mixer_kernel,
        grid=(batch, nb),
        in_specs=[
            pl.BlockSpec(memory_space=pltpu.SMEM),
            pl.BlockSpec((rows, A_WIDTH + 2 * B_WIDTH), lambda b, n: (row_blk(b, n), 0)),
            pl.BlockSpec((WINDOW, 2 * A_KV_WIDTH), lambda b, n: (prev_blk(b, n), kv_blk)),
            pl.BlockSpec((rows, 2 * A_KV_WIDTH), lambda b, n: (row_blk(b, n), kv_blk)),
            pl.BlockSpec((1, A_HEAD_DIM), lambda b, n: (0, 0)),
            pl.BlockSpec((1, A_HEAD_DIM), lambda b, n: (0, 0)),
            pl.BlockSpec((1, B_WIDTH), lambda b, n: (0, 0)),
            pl.BlockSpec((1, B_WIDTH), lambda b, n: (0, 0)),
            pl.BlockSpec((B_GROUPS, WINDOW, WINDOW), lambda b, n: (0, 0, 0)),
            pl.BlockSpec((WINDOW, B_GROUPS), lambda b, n: (0, 0)),
        ],
        out_specs=pl.BlockSpec((rows, A_WIDTH + B_WIDTH), lambda b, n: (row_blk(b, n), 0)),
        out_shape=jax.ShapeDtypeStruct((batch * seq, A_WIDTH + B_WIDTH), BF16),
        compiler_params=_params(("parallel", "arbitrary"), VMEM_SMALL_MIB),
        name="ab_mixer",
    )(sinks, proj, proj, proj,
      q_gain.reshape(1, A_HEAD_DIM), k_gain.reshape(1, A_HEAD_DIM),
      ln_g.reshape(1, B_WIDTH), ln_b.reshape(1, B_WIDTH), w_s, b_s.T)


def _hgrn2_kernel(og_ref, qvg_ref, lf_ref, k_ref, o_ref, st_ref):
    c = C_SUB
    hw = C_HEADS_PER_STEP * C_DIM

    @pl.when(pl.program_id(2) == 0)
    def _():
        st_ref[...] = jnp.zeros_like(st_ref)

    row = lax.broadcasted_iota(jnp.int32, (c, c), 0)
    col = lax.broadcasted_iota(jnp.int32, (c, c), 1)
    row1 = lax.broadcasted_iota(jnp.int32, (c, 1), 0)
    cumsum_mat = (row >= col).astype(BF16)
    diff_bits = row ^ col
    level = jnp.zeros((c, c), jnp.int32)
    for b in range(1, c.bit_length() - 1):
        level = level + (diff_bits >= (1 << b)).astype(jnp.int32)
    level = jnp.where(col < row, level, jnp.where(col == row, -1, -2))
    og = og_ref[...]

    for h in range(C_HEADS_PER_STEP):
        sl = slice(h * C_DIM, (h + 1) * C_DIM)
        st = st_ref[h]
        for sub in range(C_CHUNK // c):
            rows = slice(sub * c, (sub + 1) * c)
            log2_f = lf_ref[rows, sl]
            k = k_ref[rows, sl].astype(F32)

            p1 = log2_f.astype(BF16)
            p2 = (log2_f - p1.astype(F32)).astype(BF16)
            cg = _dot(cumsum_mat, p1) + _dot(cumsum_mat, p2)

            q = qvg_ref[rows, sl].astype(F32)
            v = qvg_ref[rows, hw + h * C_DIM:hw + (h + 1) * C_DIM]
            o = lax.dot_general((q * jnp.exp2(cg)).astype(BF16), st.astype(BF16), NT_DIMS,
                                preferred_element_type=F32)

            att = jnp.where(level == -1, jnp.sum(q * k, axis=-1, keepdims=True), 0.0)
            part = lax.dot_general((q * jnp.exp2(log2_f)).astype(BF16), k_ref[rows, sl], NT_DIMS,
                                   preferred_element_type=F32)
            att = jnp.where(level == 0, part, att)
            cg3 = cg.reshape(c // SUBLANES, SUBLANES, C_DIM)
            sub_row = lambda r: jnp.broadcast_to(cg3[:, r:r + 1, :], cg3.shape).reshape(c, C_DIM)
            n = 2
            while n < c:
                if 4 * n == SUBLANES:
                    ref = jnp.where((row1 & (2 * n)) != 0, sub_row(3 * n - 1), sub_row(n - 1))
                elif 2 * n == SUBLANES:
                    ref = sub_row(n - 1)
                else:
                    ref = jnp.concatenate([jnp.broadcast_to(cg[lo + n - 1:lo + n, :], (2 * n, C_DIM))
                                           for lo in range(0, c, 2 * n)], axis=0)
                e = jnp.exp2(-jnp.abs(cg - ref))
                part = lax.dot_general((q * e).astype(BF16), (k * e).astype(BF16), NT_DIMS,
                                       preferred_element_type=F32)
                att = jnp.where(level == n.bit_length() - 1, part, att)
                n *= 2

            o = o + _dot(att.astype(BF16), v)

            last = cg[c - 1:c, :]
            kd = (k * jnp.exp2(last - cg)).astype(BF16)
            st = st * jnp.exp2(last) + lax.dot_general(v, kd, TN_DIMS, preferred_element_type=F32)

            y = o * lax.rsqrt(jnp.mean(o * o, axis=-1, keepdims=True) + EPS) * og
            gate = qvg_ref[rows, 2 * hw + h * C_DIM:2 * hw + (h + 1) * C_DIM]
            o_ref[rows, sl] = (y * gate.astype(F32)).astype(o_ref.dtype)
        st_ref[h] = st


def _hgrn2(qvg, log2_f, k, o_gain, batch, seq):
    nc = seq // C_CHUNK
    hw = C_HEADS_PER_STEP * C_DIM
    nh = C_WIDTH // hw
    blk = lambda width: pl.BlockSpec((C_CHUNK, width), lambda b, g, c: (b * nc + c, g))
    return pl.pallas_call(
        _hgrn2_kernel,
        grid=(batch, nh, nc),
        in_specs=[pl.BlockSpec((1, C_DIM), lambda b, g, c: (0, 0)),
                  blk(3 * hw), blk(hw), blk(hw)],
        out_specs=blk(hw),
        out_shape=jax.ShapeDtypeStruct((batch * seq, C_WIDTH), BF16),
        scratch_shapes=[pltpu.VMEM((C_HEADS_PER_STEP, C_DIM, C_DIM), F32)],
        compiler_params=_params(("parallel", "parallel", "arbitrary"), VMEM_MATMUL_MIB),
        name="hgrn2",
    )(o_gain.reshape(1, C_DIM), qvg, log2_f, k)


def _ffn(x, norm_g, w1, w3, w2, layer):
    h, w2b = _ffn_up(_rmsnorm(x, norm_g[layer]), w1, w3, w2, layer)
    return _matmul_res(h, w2b, x, tk=D_FF_PAD // 4)


def kernel(x, ffn1_norm, ffn1_w1, ffn1_w3, ffn1_w2, mix_norm, ffn2_norm, ffn2_w1, ffn2_w3, ffn2_w2, ab_w_in, ab_q_norm, ab_k_norm, ab_sinks, ab_v_ln_g, ab_v_ln_b, ab_w_s, ab_b_s, ab_w_out, c_w_in, c_lb_logits, c_o_norm, c_w_out):
    batch, seq, d = x.shape
    depth = ffn1_norm.shape[0]
    tn = 512
    x = x.reshape(batch * seq, d)
    for l in range(depth):
        x = _ffn(x, ffn1_norm, ffn1_w1, ffn1_w3, ffn1_w2, l)
        hn = _rmsnorm(x, mix_norm[l])
        j = l // 2
        if l % 2 == 0:
            kv_tile = A_WIDTH // tn
            n_tiles = AB_IN // tn
            o_tile = lambda t: jnp.where(t < kv_tile, t, jnp.where(t == kv_tile, n_tiles - 1, t - 1))
            proj = _wpipe_matmul("ab_in_proj", hn, ab_w_in, j, n_tiles, [F32], o_tile=o_tile,
                                 epilogue=functools.partial(_act_epilogue, act=_gelu, act_from=kv_tile + 1))
            mixed = _ab_mixer(proj, batch, seq, ab_q_norm[j], ab_k_norm[j], ab_sinks[j],
                              ab_v_ln_g[j], ab_v_ln_b[j], ab_w_s[j], ab_b_s[j])
            x = _wpipe_matmul("ab_out_proj", mixed, ab_w_out, j, d // tn, [F32], _residual_epilogue,
                              tiled_extra=[x], vmem_mib=VMEM_RESIDUAL_MIB)
        else:
            ct = C_WIDTH // tn
            gt = C_HEADS_PER_STEP * C_DIM // tn
            qvg = _wpipe_matmul("c_in_proj_qvg", hn, c_w_in, j, 3 * ct, [BF16],
                                functools.partial(_act_epilogue, act=_silu, act_from=2 * ct),
                                w_tile=lambda t: jnp.where(t < ct, t, t + ct),
                                o_tile=lambda t: (t % ct) // gt * (3 * gt) + t // ct * gt + t % gt)
            log2_f, k = _wpipe_matmul("c_in_proj_f", hn, c_w_in, j, ct, [F32, BF16],
                                      functools.partial(_forget_gate_epilogue, layer=l),
                                      w_tile=lambda t: t + ct, col_extra=[c_lb_logits])
            mixed = _hgrn2(qvg, log2_f, k, c_o_norm[j], batch, seq)
            x = _wpipe_matmul("c_out_proj", mixed, c_w_out, j, d // tn, [F32], _residual_epilogue,
                              tiled_extra=[x], vmem_mib=VMEM_RESIDUAL_MIB)
        x = _ffn(x, ffn2_norm, ffn2_w1, ffn2_w3, ffn2_w2, l)
    return x.reshape(batch, seq, d)
```
